```python
import jax
import jax.numpy as jnp
from jax import lax
import numpy as np

D_MODEL = 1024
BATCH = 8
SEQ = 2048
DEPTH = 4

GRID_W = 64
CTX_LEN = 256
RMS_EPS = 1e-6
N_MOD = 6

NA_HEADS = 8
NA_HEAD_DIM = 64
NA_WIDTH = NA_HEADS * NA_HEAD_DIM
WIN_ROWS = 8
WIN_COLS = 16
NA_QCB = 16
NA_KCB = 32

HG_HEADS = 4
HG_DK = 128
HG_DV = 128
HG_KEY_WIDTH = HG_HEADS * HG_DK
HG_WIDTH = HG_HEADS * HG_DV
HG_CHUNK = 64

GQA_Q_HEADS = 8
GQA_KV_HEADS = 2
GQA_HEAD_DIM = 64
GQA_Q_WIDTH = GQA_Q_HEADS * GQA_HEAD_DIM
GQA_KV_WIDTH = GQA_KV_HEADS * GQA_HEAD_DIM
GQA_BLOCK = 128
ROPE_THETA = 10000.0

N_BRANCHES = 3
BRANCH_W = 512

N_GROUPS = 4
EXPERTS_PER_GROUP = 4
N_EXPERTS = N_GROUPS * EXPERTS_PER_GROUP
TOP_K = 2
EXPERT_HIDDEN = 512

IN_SPLITS = (NA_WIDTH, NA_WIDTH, NA_WIDTH,
             HG_KEY_WIDTH, HG_KEY_WIDTH, HG_KEY_WIDTH, HG_WIDTH, HG_WIDTH,
             GQA_Q_WIDTH, GQA_KV_WIDTH, GQA_KV_WIDTH,
             D_MODEL, D_MODEL, D_MODEL)
IN_COLS = 3 * NA_WIDTH + 3 * HG_KEY_WIDTH + 2 * HG_WIDTH + GQA_Q_WIDTH + 2 * GQA_KV_WIDTH + N_BRANCHES * D_MODEL

kernel_name = 'hybrid_na_hgrn2_gqa_hmoe_dit'


def rmsnorm(x, g):
    xf = x.astype(jnp.float32)
    y = xf * lax.rsqrt(jnp.mean(xf * xf, axis=-1, keepdims=True) + RMS_EPS)
    return (y * g.astype(jnp.float32)).astype(x.dtype)


def modulate(h, shift, scale):
    return h * (1 + scale) + shift


def split_heads(z, dh):
    return z.reshape(z.shape[:-1] + (z.shape[-1] // dh, dh))


def split_in(z):
    return jnp.split(z, np.cumsum(IN_SPLITS)[:-1].tolist(), axis=-1)


def axial_rope_tables(n_tokens, dtype):
    t = jnp.arange(n_tokens)
    pos = jnp.stack([t // GRID_W, t % GRID_W], axis=-1).astype(jnp.float32)
    n_freq = GQA_HEAD_DIM // 4
    inv_freq = jnp.power(ROPE_THETA, -jnp.arange(n_freq, dtype=jnp.float32) / n_freq)
    ang = pos[:, :, None] * inv_freq
    ang = jnp.concatenate([ang, ang], axis=-1)
    return jnp.cos(ang).astype(dtype), jnp.sin(ang).astype(dtype)


def apply_axial_rope(x, cos, sin):
    xa = x.reshape(x.shape[:-1] + (2, GQA_HEAD_DIM // 2))
    x1, x2 = jnp.split(xa, 2, axis=-1)
    rot = jnp.concatenate([-x2, x1], axis=-1)
    out = xa * cos[None, :, None] + rot * sin[None, :, None]
    return out.reshape(x.shape)


def grouped_attention(q, k, v):
    scale = q.shape[-1] ** -0.5
    s = jnp.einsum('bqgrd,bkgd->bgrqk', q, k).astype(jnp.float32) * scale
    p = jax.nn.softmax(s, axis=-1).astype(v.dtype)
    return jnp.einsum('bgrqk,bkgd->bqgrd', p, v)


def blocked_grouped_attention(q, k, v):
    B, L, G, R, hd = q.shape
    nb = L // GQA_BLOCK
    qb = q.reshape(B, nb, GQA_BLOCK, G, R, hd).transpose(1, 0, 2, 3, 4, 5)
    ob = lax.map(lambda qq: grouped_attention(qq, k, v), qb)
    return ob.transpose(1, 0, 2, 3, 4, 5).reshape(B, L, G, R, hd)


def neighbourhood_attention(q, k, v, k_ctx, v_ctx, rpb):
    B, L, H, hd = q.shape
    rows = L // GRID_W
    wr = min(WIN_ROWS, rows)
    ncb = GRID_W // NA_QCB
    n_loc = wr * NA_KCB
    scale = hd ** -0.5
    qg = q.reshape(B, rows, GRID_W, H, hd)
    kg = k.reshape(B, rows, GRID_W, H, hd)
    vg = v.reshape(B, rows, GRID_W, H, hd)
    qcol = np.arange(GRID_W).reshape(ncb, NA_QCB)
    kc0 = np.clip(np.arange(ncb) * NA_QCB - WIN_COLS // 2, 0, GRID_W - NA_KCB)
    kcol = kc0[:, None] + np.arange(NA_KCB)[None, :]
    c0 = np.clip(qcol - WIN_COLS // 2, 0, GRID_W - WIN_COLS)
    col_ok = (kcol[:, None, :] >= c0[:, :, None]) & (kcol[:, None, :] < c0[:, :, None] + WIN_COLS)
    dcol = kcol[:, None, :] - qcol[:, :, None] + WIN_COLS - 1
    mask = np.broadcast_to(col_ok[:, :, None, :], (ncb, NA_QCB, wr, NA_KCB)).reshape(ncb, NA_QCB, n_loc)
    rpb = rpb.astype(jnp.float32)

    def one_row(r):
        r0 = jnp.clip(r - wr // 2, 0, rows - wr)
        q_r = lax.dynamic_index_in_dim(qg, r, axis=1, keepdims=False)
        k_r = lax.dynamic_slice_in_dim(kg, r0, wr, axis=1)[:, :, kcol]
        v_r = lax.dynamic_slice_in_dim(vg, r0, wr, axis=1)[:, :, kcol]
        k_r = k_r.transpose(0, 2, 1, 3, 4, 5).reshape(B, ncb, n_loc, H, hd)
        v_r = v_r.transpose(0, 2, 1, 3, 4, 5).reshape(B, ncb, n_loc, H, hd)
        q_b = q_r.reshape(B, ncb, NA_QCB, H, hd)
        drow = r0 + jnp.arange(wr) - r + WIN_ROWS - 1
        bias = rpb[:, drow][:, :, dcol]
        bias = bias.transpose(0, 2, 3, 1, 4).reshape(H, ncb, NA_QCB, n_loc)
        s_loc = jnp.einsum('bjqhd,bjkhd->bhjqk', q_b, k_r).astype(jnp.float32) * scale + bias
        s_loc = jnp.where(mask, s_loc, -jnp.inf)
        s_ctx = jnp.einsum('bjqhd,bkhd->bhjqk', q_b, k_ctx).astype(jnp.float32) * scale
        p = jax.nn.softmax(jnp.concatenate([s_loc, s_ctx], axis=-1), axis=-1).astype(v.dtype)
        o = (jnp.einsum('bhjqk,bjkhd->bjqhd', p[..., :n_loc], v_r)
             + jnp.einsum('bhjqk,bkhd->bjqhd', p[..., n_loc:], v_ctx))
        return o.reshape(B, GRID_W, H, hd)

    o = lax.map(one_row, jnp.arange(rows))
    return o.transpose(1, 0, 2, 3, 4).reshape(B, L, H * hd)


def chunked_gated_scan(q, k, v, log_f, s0):
    B, L, H, _ = q.shape
    dv = v.shape[-1]
    n = L // HG_CHUNK

    def chunks(a):
        return a.reshape(B, n, HG_CHUNK, H, a.shape[-1]).transpose(1, 0, 3, 2, 4)

    tri = np.tril(np.ones((HG_CHUNK, HG_CHUNK), dtype=bool))[:, :, None]

    def step(S, inp):
        qc, kc, vc, gc = inp
        b = jnp.cumsum(gc, axis=2)
        decay = jnp.exp(jnp.where(tri, b[:, :, :, None, :] - b[:, :, None, :, :], -jnp.inf))
        a = jnp.einsum('bhtd,bhsd,bhtsd->bhts', qc, kc, decay)
        o = jnp.einsum('bhts,bhsv->bhtv', a, vc) + jnp.einsum('bhtd,bhdv->bhtv', qc * jnp.exp(b), S)
        b_last = b[:, :, -1:, :]
        S = jnp.exp(b_last)[:, :, 0, :, None] * S + jnp.einsum('bhsd,bhsv->bhdv', kc * jnp.exp(b_last - b), vc)
        return S, o

    s_fin, o = lax.scan(step, s0, (chunks(q), chunks(k), chunks(v), chunks(log_f)))
    return o.transpose(1, 0, 3, 2, 4).reshape(B, L, H, dv), s_fin


def hgrn2_scans(q_raw, ff_raw, fb_raw, i_raw, lb, s_fwd, s_bwd):
    q = split_heads(jax.nn.silu(q_raw.astype(jnp.float32)), HG_DK)
    v = split_heads(i_raw.astype(jnp.float32), HG_DV)

    def forget(f_raw, lower):
        f = lower + (1.0 - lower) * jax.nn.sigmoid(f_raw.astype(jnp.float32))
        return split_heads(1.0 - f, HG_DK), split_heads(jnp.log(f), HG_DK)

    k_f, g_f = forget(ff_raw, lb[0])
    k_b, g_b = forget(fb_raw, lb[1])
    rev = lambda a: jnp.flip(a, axis=1)
    o_f, s_fwd = chunked_gated_scan(q, k_f, v, g_f, s_fwd)
    o_b, s_bwd = chunked_gated_scan(rev(q), rev(k_b), rev(v), rev(g_b), s_bwd)
    return o_f + rev(o_b), s_fwd, s_bwd


def hgrn2_readout(o, g_raw, norm_g):
    y = rmsnorm(o, norm_g) * jax.nn.silu(split_heads(g_raw, HG_DV).astype(jnp.float32))
    return y.reshape(o.shape[:2] + (HG_WIDTH,)).astype(g_raw.dtype)


def group_queries(q):
    return q.reshape(q.shape[:2] + (GQA_KV_HEADS, GQA_Q_HEADS // GQA_KV_HEADS, GQA_HEAD_DIM))


def branch_merge(branches, gate_logits, w_branch, w_out):
    y = 0
    for i in range(N_BRANCHES):
        y = y + jax.nn.sigmoid(gate_logits[i]) * (branches[i] @ w_branch[i])
    return y @ w_out


def token_mixer(h_lat, h_ctx, w_in, rpb, lb, hg_norm_g, q_norm_g, k_norm_g, w_branch, w_out, cos, sin, with_ctx):
    B, L, _ = h_lat.shape
    n_ctx = h_ctx.shape[1]
    zl = split_in(h_lat @ w_in)
    zc = split_in(h_ctx @ w_in)
    na_q, na_k, na_v = [split_heads(z, NA_HEAD_DIM) for z in zl[0:3]]
    na_qc, na_kc, na_vc = [split_heads(z, NA_HEAD_DIM) for z in zc[0:3]]
    a_lat = neighbourhood_attention(na_q, na_k, na_v, na_kc, na_vc, rpb)
    s0 = jnp.zeros((B, HG_HEADS, HG_DK, HG_DV), jnp.float32)
    o_ctx, s_fwd, s_bwd = hgrn2_scans(zc[3], zc[4], zc[5], zc[6], lb, s0, s0)
    o_lat, _, _ = hgrn2_scans(zl[3], zl[4], zl[5], zl[6], lb, s_fwd, s_bwd)
    b_lat = hgrn2_readout(o_lat, zl[7], hg_norm_g)
    q_lat = apply_axial_rope(rmsnorm(split_heads(zl[8], GQA_HEAD_DIM), q_norm_g), cos, sin)
    k_lat = apply_axial_rope(rmsnorm(split_heads(zl[9], GQA_HEAD_DIM), k_norm_g), cos, sin)
    q_ctx = rmsnorm(split_heads(zc[8], GQA_HEAD_DIM), q_norm_g)
    k_ctx = rmsnorm(split_heads(zc[9], GQA_HEAD_DIM), k_norm_g)
    v_lat = split_heads(zl[10], GQA_HEAD_DIM)
    v_ctx = split_heads(zc[10], GQA_HEAD_DIM)
    k_all = jnp.concatenate([k_ctx, k_lat], axis=1)
    v_all = jnp.concatenate([v_ctx, v_lat], axis=1)
    c_lat = blocked_grouped_attention(group_queries(q_lat), k_all, v_all).reshape(B, L, GQA_Q_WIDTH)
    y_lat = branch_merge((a_lat, b_lat, c_lat), zl[11:14], w_branch, w_out)
    if not with_ctx:
        return y_lat, None
    a_ctx = grouped_attention(na_qc[:, :, :, None, :], na_kc, na_vc).reshape(B, n_ctx, NA_WIDTH)
    b_ctx = hgrn2_readout(o_ctx, zc[7], hg_norm_g)
    c_ctx_out = grouped_attention(group_queries(q_ctx), k_ctx, v_ctx).reshape(B, n_ctx, GQA_Q_WIDTH)
    y_ctx = branch_merge((a_ctx, b_ctx, c_ctx_out), zc[11:14], w_branch, w_out)
    return y_lat, y_ctx


def hierarchical_moe(h, w_gr, b_gr, w_er, b_er, w_gate, w_up, w_down):
    n = h.shape[0]
    g_logits = (h @ w_gr + b_gr).astype(jnp.float32)
    g_val, g_idx = lax.top_k(g_logits, 1)
    p_group = jnp.exp(g_val - jax.nn.logsumexp(g_logits, axis=-1, keepdims=True))
    e_logits = (h @ w_er + b_er).astype(jnp.float32).reshape(n, N_GROUPS, EXPERTS_PER_GROUP)
    e_in = jnp.take_along_axis(e_logits, g_idx[:, :, None], axis=1)[:, 0]
    e_val, e_idx = lax.top_k(e_in, TOP_K)
    w_sel = jax.nn.softmax(e_val, axis=-1) * p_group
    expert_id = g_idx * EXPERTS_PER_GROUP + e_idx
    combine = jnp.einsum('nk,nke->ne', w_sel, jax.nn.one_hot(expert_id, N_EXPERTS, dtype=jnp.float32)).astype(h.dtype)
    y = jnp.zeros_like(h)
    for e in range(N_EXPERTS):
        he = jax.nn.silu(h @ w_gate[e]) * (h @ w_up[e])
        y = y + combine[:, e:e + 1] * (he @ w_down[e])
    return y


def setup_inputs(seed: int = 0) -> dict:
    key = jax.random.key(seed)
    ks = iter(jax.random.split(key, 32))
    d = D_MODEL

    def nrm(shape, s):
        return jax.random.normal(next(ks), shape, jnp.float32) * s

    return {
        'x': nrm((BATCH, SEQ, d), 1.0),
        'c': nrm((BATCH, d), 1.0),
        'ctx': nrm((BATCH, CTX_LEN, d), 1.0),
        'c_ctx': nrm((d,), 1.0),
        'w_ada': nrm((DEPTH, d, N_MOD * d), 0.5 * d ** -0.5),
        'b_ada': nrm((DEPTH, N_MOD * d), 0.02),
        'g_mix': 1.0 + nrm((DEPTH, d), 0.1),
        'g_ffn': 1.0 + nrm((DEPTH, d), 0.1),
        'w_in': nrm((DEPTH, d, IN_COLS), d ** -0.5),
        'na_rpb': nrm((DEPTH, NA_HEADS, 2 * WIN_ROWS - 1, 2 * WIN_COLS - 1), 0.1),
        'hg_lb_logits': nrm((DEPTH, 2, HG_KEY_WIDTH), 1.0),
        'hg_norm_g': 1.0 + nrm((DEPTH, HG_DV), 0.1),
        'gqa_qnorm_g': 1.0 + nrm((DEPTH, GQA_HEAD_DIM), 0.1),
        'gqa_knorm_g': 1.0 + nrm((DEPTH, GQA_HEAD_DIM), 0.1),
        'w_branch': nrm((DEPTH, N_BRANCHES, BRANCH_W, d), BRANCH_W ** -0.5),
        'w_out': nrm((DEPTH, d, d), d ** -0.5),
        'w_group_router': nrm((DEPTH, d, N_GROUPS), d ** -0.5),
        'b_group_router': nrm((DEPTH, N_GROUPS), 0.01),
        'w_expert_router': nrm((DEPTH, d, N_EXPERTS), d ** -0.5),
        'b_expert_router': nrm((DEPTH, N_EXPERTS), 0.01),
        'w_exp_gate': nrm((DEPTH, N_EXPERTS, d, EXPERT_HIDDEN), d ** -0.5),
        'w_exp_up': nrm((DEPTH, N_EXPERTS, d, EXPERT_HIDDEN), d ** -0.5),
        'w_exp_down': nrm((DEPTH, N_EXPERTS, EXPERT_HIDDEN, d), EXPERT_HIDDEN ** -0.5),
        'g_final': 1.0 + nrm((d,), 0.1),
    }


def reference(x, c, ctx, c_ctx, w_ada, b_ada, g_mix, g_ffn, w_in, na_rpb, hg_lb_logits, hg_norm_g,
              gqa_qnorm_g, gqa_knorm_g, w_branch, w_out, w_group_router, b_group_router,
              w_expert_router, b_expert_router, w_exp_gate, w_exp_up, w_exp_down, g_final):
    B, L, d = x.shape
    cos, sin = axial_rope_tables(L, x.dtype)
    p_lb = jax.nn.softmax(hg_lb_logits.astype(jnp.float32), axis=0)
    lower_bounds = jnp.cumsum(p_lb, axis=0) - p_lb[0:1]
    silu_c = jax.nn.silu(c)
    silu_cc = jax.nn.silu(c_ctx)[None, :]
    for l in range(DEPTH):
        last = l == DEPTH - 1
        mod = jnp.split((silu_c @ w_ada[l] + b_ada[l])[:, None, :], N_MOD, axis=-1)
        mod_c = jnp.split((silu_cc @ w_ada[l] + b_ada[l])[:, None, :], N_MOD, axis=-1)
        h_lat = modulate(rmsnorm(x, g_mix[l]), mod[0], mod[1])
        h_ctx = modulate(rmsnorm(ctx, g_mix[l]), mod_c[0], mod_c[1])
        y_lat, y_ctx = token_mixer(h_lat, h_ctx, w_in[l], na_rpb[l], lower_bounds[l], hg_norm_g[l],
                                   gqa_qnorm_g[l], gqa_knorm_g[l], w_branch[l], w_out[l], cos, sin, not last)
        x = x + mod[2] * y_lat
        f_lat = modulate(rmsnorm(x, g_ffn[l]), mod[3], mod[4]).reshape(-1, d)
        if last:
            y = hierarchical_moe(f_lat, w_group_router[l], b_group_router[l], w_expert_router[l],
                                 b_expert_router[l], w_exp_gate[l], w_exp_up[l], w_exp_down[l])
            x = x + mod[5] * y.reshape(x.shape)
        else:
            ctx = ctx + mod_c[2] * y_ctx
            f_ctx = modulate(rmsnorm(ctx, g_ffn[l]), mod_c[3], mod_c[4]).reshape(-1, d)
            n_lat = f_lat.shape[0]
            y = hierarchical_moe(jnp.concatenate([f_lat, f_ctx], axis=0), w_group_router[l], b_group_router[l],
                                 w_expert_router[l], b_expert_router[l], w_exp_gate[l], w_exp_up[l], w_exp_down[l])
            x = x + mod[5] * y[:n_lat].reshape(x.shape)
            ctx = ctx + mod_c[5] * y[n_lat:].reshape(ctx.shape)
    return rmsnorm(x, g_final)
```

```python
import functools

import numpy as np
import jax
import jax.numpy as jnp
from jax import lax
from jax.experimental import pallas as pl
from jax.experimental.pallas import tpu as pltpu

F32 = jnp.float32
BF16 = jnp.bfloat16

RMS_EPS = 1e-6
N_MOD = 6
GRID_W = 64

NA_HEADS = 8
NA_HEAD_DIM = 64
NA_WIDTH = NA_HEADS * NA_HEAD_DIM
WIN_ROWS = 8
WIN_COLS = 16

HG_HEADS = 4
HG_DK = 128
HG_WIDTH = HG_HEADS * HG_DK
HG_CHUNK = 64

GQA_Q_HEADS = 8
GQA_KV_HEADS = 2
GQA_HEAD_DIM = 64
GQA_Q_WIDTH = GQA_Q_HEADS * GQA_HEAD_DIM
GQA_KV_WIDTH = GQA_KV_HEADS * GQA_HEAD_DIM
ROPE_THETA = 10000.0

N_GROUPS = 4
EXPERTS_PER_GROUP = 4
N_EXPERTS = N_GROUPS * EXPERTS_PER_GROUP
EXPERT_HIDDEN = 512

LANES = 128
ROW_TILE = 256
MASK_NEG = -1e30

COL_NA_Q = 0
COL_NA_K = 512
COL_NA_V = 1024
COL_HG_Q = 1536
COL_HG_FF = 2048
COL_HG_FB = 2560
COL_HG_I = 3072
COL_HG_G = 3584
COL_GQA_Q = 4096
COL_GQA_K = 4608
COL_GQA_V = 4736
COL_RAW_GATES = 4864
COL_GATES = 5120
IN_COLS_PAD = 8192
IN_TILE = 512

VMEM_LIMIT = 56 * 1024 * 1024


def _dot(a, b):
    return jnp.dot(a, b, preferred_element_type=F32)


def _dot_nt(a, b):
    return lax.dot_general(a, b, (((1,), (1,)), ((), ())), preferred_element_type=F32)


def _dot_tn(a, b):
    return lax.dot_general(a, b, (((0,), (0,)), ((), ())), preferred_element_type=F32)


def _split_bf16(x):
    hi = x.astype(BF16)
    lo = (x - hi.astype(F32)).astype(BF16)
    return hi, lo


def _silu(x):
    return x * jax.nn.sigmoid(x)


def _rms(x):
    return x * lax.rsqrt(jnp.mean(x * x, axis=-1, keepdims=True) + RMS_EPS)


def _params(semantics, vmem=VMEM_LIMIT):
    return pltpu.CompilerParams(dimension_semantics=semantics, vmem_limit_bytes=vmem)


def _ada_kernel(c_ref, w_ref, b_ref, o_ref):
    sc = _silu(c_ref[...]).astype(BF16)
    o_ref[0] = _dot(sc, w_ref[0].astype(BF16)) + b_ref[0]


def _ada(c_all, w_ada, b_ada):
    depth, d, n = w_ada.shape
    rows = c_all.shape[0]
    tn = 1536
    return pl.pallas_call(
        _ada_kernel,
        out_shape=jax.ShapeDtypeStruct((depth, rows, n), F32),
        grid=(depth, n // tn),
        in_specs=[
            pl.BlockSpec((rows, d), lambda l, j: (0, 0)),
            pl.BlockSpec((1, d, tn), lambda l, j: (l, 0, j)),
            pl.BlockSpec((1, 1, tn), lambda l, j: (l, 0, j)),
        ],
        out_specs=pl.BlockSpec((1, rows, tn), lambda l, j: (l, 0, j)),
        compiler_params=_params(("arbitrary", "arbitrary")),
        name="ada_mod",
    )(c_all, w_ada, b_ada.reshape(depth, 1, n))


def _inproj_kernel(s_ref, g_ref, modc_ref, mod_ref, w_ref, z_ref, zf_ref, h_scr, *, n_ctx):
    j = pl.program_id(1)
    t = s_ref.shape[1]

    @pl.when(j == 0)
    def _():
        g = g_ref[...]

        def norm_mod(x, shift, scale):
            return (_rms(x) * g) * (1.0 + scale) + shift

        h_scr[0:n_ctx, :] = norm_mod(s_ref[0, 0:n_ctx, :], modc_ref[0:1, :], modc_ref[1:2, :]).astype(BF16)
        shift = mod_ref[0, 0:1, :]
        scale = mod_ref[0, 1:2, :]

        def body(r, carry):
            rows = pl.ds(pl.multiple_of(n_ctx + r * ROW_TILE, ROW_TILE), ROW_TILE)
            h_scr[rows, :] = norm_mod(s_ref[0, rows, :], shift, scale).astype(BF16)
            return carry

        lax.fori_loop(0, (t - n_ctx) // ROW_TILE, body, 0)

    f_lo = COL_HG_FF // IN_TILE
    is_f = jnp.logical_or(j == f_lo, j == f_lo + 1)

    def mm(r, carry):
        rows = pl.ds(pl.multiple_of(r * ROW_TILE, ROW_TILE), ROW_TILE)
        acc = _dot(h_scr[rows, :], w_ref[...])
        z_ref[0, rows, :] = acc.astype(BF16)

        @pl.when(is_f)
        def _():
            zf_ref[0, rows, :] = acc

        return carry

    lax.fori_loop(0, t // ROW_TILE, mm, 0)


def _inproj(s, g, modc, mod, w_pad, n_ctx):
    b, t, d = s.shape
    n = w_pad.shape[1]
    f_lo = COL_HG_FF // IN_TILE
    return pl.pallas_call(
        functools.partial(_inproj_kernel, n_ctx=n_ctx),
        out_shape=(jax.ShapeDtypeStruct((b, t, n), BF16),
                   jax.ShapeDtypeStruct((b, t, 2 * IN_TILE), F32)),
        grid=(b, n // IN_TILE),
        in_specs=[
            pl.BlockSpec((1, t, d), lambda i, j: (i, 0, 0)),
            pl.BlockSpec((1, d), lambda i, j: (0, 0)),
            pl.BlockSpec((N_MOD, d), lambda i, j: (0, 0)),
            pl.BlockSpec((1, N_MOD, d), lambda i, j: (i, 0, 0)),
            pl.BlockSpec((d, IN_TILE), lambda i, j: (0, j)),
        ],
        out_specs=(
            pl.BlockSpec((1, t, IN_TILE), lambda i, j: (i, 0, j)),
            pl.BlockSpec((1, t, IN_TILE), lambda i, j: (i, 0, jnp.clip(j - f_lo, 0, 1))),
        ),
        scratch_shapes=[pltpu.VMEM((t, d), BF16)],
        compiler_params=_params(("arbitrary", "arbitrary")),
        name="in_proj",
    )(s, g, modc, mod, w_pad)


def _na_bias_table(rpb):
    qc = np.arange(GRID_W)[:, None]
    kc = np.arange(GRID_W)[None, :]
    c0 = np.clip(qc - WIN_COLS // 2, 0, GRID_W - WIN_COLS)
    col_ok = (kc >= c0) & (kc < c0 + WIN_COLS)
    dcol = np.clip(kc - qc + WIN_COLS - 1, 0, 2 * WIN_COLS - 2)
    drow = np.arange(WIN_ROWS)[None, :] - np.arange(WIN_ROWS)[:, None] + WIN_ROWS - 1
    tab = rpb.astype(F32)[:, drow][:, :, :, dcol]
    tab = jnp.where(col_ok[None, None, None], tab, MASK_NEG)
    h = rpb.shape[0]
    return tab.transpose(0, 1, 3, 2, 4).reshape(h, WIN_ROWS, GRID_W, WIN_ROWS * GRID_W)


def _na_kernel(q_ref, k_ref, v_ref, bias_ref, o_ref, *, n_ctx, rows):
    i = pl.program_id(1)
    n_cq = n_ctx // GRID_W
    lane = lax.broadcasted_iota(jnp.int32, (GRID_W, LANES), 1)
    lo_half = lane < NA_HEAD_DIM

    def attend(q2, parts):
        outs = []
        for hh in range(2):
            keep = lo_half if hh == 0 else jnp.logical_not(lo_half)
            qm = jnp.where(keep, q2, jnp.zeros_like(q2))
            scores = []
            for kk, _, bias in parts:
                s = _dot_nt(qm, kk)
                if bias is not None:
                    s = s + bias[hh]
                scores.append(s)
            m = functools.reduce(jnp.maximum, [jnp.max(s, axis=-1, keepdims=True) for s in scores])
            den = 0.0
            acc = 0.0
            for s, (_, vv, _) in zip(scores, parts):
                p = jnp.exp(s - m)
                den = den + jnp.sum(p, axis=-1, keepdims=True)
                acc = acc + _dot(p.astype(BF16), vv)
            outs.append(acc / den)
        return jnp.where(lo_half, outs[0], outs[1])

    scale = NA_HEAD_DIM ** -0.5

    @pl.when(i < n_cq)
    def _():
        for hp in range(NA_HEADS // 2):
            cols = slice(hp * LANES, (hp + 1) * LANES)
            q2 = q_ref[0, :, cols] * scale
            parts = [(k_ref[0, 0:n_ctx, cols], v_ref[0, 0:n_ctx, cols], None)]
            o_ref[0, :, cols] = attend(q2, parts).astype(o_ref.dtype)

    @pl.when(i >= n_cq)
    def _():
        r = i - n_cq
        r0 = jnp.clip(r - WIN_ROWS // 2, 0, rows - WIN_ROWS)
        koff = pl.multiple_of(n_ctx + r0 * GRID_W, GRID_W)
        krows = pl.ds(koff, WIN_ROWS * GRID_W)
        for hp in range(NA_HEADS // 2):
            cols = slice(hp * LANES, (hp + 1) * LANES)
            q2 = q_ref[0, :, cols] * scale
            bias = (bias_ref[2 * hp, 0], bias_ref[2 * hp + 1, 0])
            parts = [(k_ref[0, krows, cols], v_ref[0, krows, cols], bias),
                     (k_ref[0, 0:n_ctx, cols], v_ref[0, 0:n_ctx, cols], None)]
            o_ref[0, :, cols] = attend(q2, parts).astype(o_ref.dtype)


def _na_attention(z, bias_tab, n_ctx):
    b, t, _ = z.shape
    rows = (t - n_ctx) // GRID_W
    n_cq = n_ctx // GRID_W
    wk = WIN_ROWS * GRID_W

    def pattern(i):
        r = jnp.maximum(i - n_cq, 0)
        return r - jnp.clip(r - WIN_ROWS // 2, 0, rows - WIN_ROWS)

    return pl.pallas_call(
        functools.partial(_na_kernel, n_ctx=n_ctx, rows=rows),
        out_shape=jax.ShapeDtypeStruct((b, t, NA_WIDTH), BF16),
        grid=(b, t // GRID_W),
        in_specs=[
            pl.BlockSpec((1, GRID_W, NA_WIDTH), lambda bi, i: (bi, i, COL_NA_Q // NA_WIDTH)),
            pl.BlockSpec((1, t, NA_WIDTH), lambda bi, i: (bi, 0, COL_NA_K // NA_WIDTH)),
            pl.BlockSpec((1, t, NA_WIDTH), lambda bi, i: (bi, 0, COL_NA_V // NA_WIDTH)),
            pl.BlockSpec((NA_HEADS, 1, GRID_W, wk), lambda bi, i: (0, pattern(i), 0, 0)),
        ],
        out_specs=pl.BlockSpec((1, GRID_W, NA_WIDTH), lambda bi, i: (bi, i, 0)),
        compiler_params=_params(("arbitrary", "arbitrary")),
        name="na_attention",
    )(z, z, z, bias_tab)


def _rope_tables(n_tokens):
    t = jnp.arange(n_tokens)
    pos = jnp.stack([t // GRID_W, t % GRID_W], axis=-1).astype(F32)
    n_freq = GQA_HEAD_DIM // 4
    inv_freq = jnp.power(ROPE_THETA, -jnp.arange(n_freq, dtype=F32) / n_freq)
    ang = pos[:, :, None] * inv_freq
    ang = jnp.concatenate([ang, ang], axis=-1).reshape(n_tokens, GQA_HEAD_DIM)
    cos, sin = jnp.cos(ang), jnp.sin(ang)
    first = (np.arange(GQA_HEAD_DIM) % (2 * n_freq)) < n_freq
    sin_a = jnp.where(first, -sin, 0.0)
    sin_b = jnp.where(first, 0.0, sin)
    rep = LANES // GQA_HEAD_DIM
    return tuple(jnp.tile(a, (1, rep)) for a in (cos, sin_a, sin_b))


def _gqa_kernel(q_ref, k_ref, v_ref, cos_ref, sa_ref, sb_ref, gq_ref, gk_ref, o_ref,
                kk_scr, vv_scr, *, n_ctx):
    j = pl.program_id(1)
    t = k_ref.shape[1]
    tq = q_ref.shape[1]
    hd = GQA_HEAD_DIM
    quarter = hd // 4

    r_i = lax.broadcasted_iota(jnp.int32, (LANES, LANES), 0)
    c_i = lax.broadcasted_iota(jnp.int32, (LANES, LANES), 1)
    head_ones = jnp.where((r_i // hd) == (c_i // hd), 1.0, 0.0).astype(BF16)
    lo_half = lax.broadcasted_iota(jnp.int32, (tq, LANES), 1) < hd

    def head_rms(x, g):
        hi, lo = _split_bf16(x * x)
        ms = (_dot(hi, head_ones) + _dot(lo, head_ones)) * (1.0 / hd)
        return (x * lax.rsqrt(ms + RMS_EPS)) * g

    def rope(x, rows):
        return (x * cos_ref[rows, :]
                + pltpu.roll(x, LANES - quarter, 1) * sa_ref[rows, :]
                + pltpu.roll(x, quarter, 1) * sb_ref[rows, :])

    @pl.when(j == 0)
    def _():
        zero = jnp.zeros((tq, LANES), F32)
        for r in range(t // tq):
            rows = slice(r * tq, (r + 1) * tq)
            k = head_rms(k_ref[0, rows, :].astype(F32), gk_ref[...])
            if r * tq >= n_ctx:
                k = rope(k, slice(r * tq - n_ctx, (r + 1) * tq - n_ctx))
            v = v_ref[0, rows, :].astype(F32)
            k_sw = pltpu.roll(k, hd, 1)
            v_sw = pltpu.roll(v, hd, 1)
            for scr, a, a_sw in ((kk_scr, k, k_sw), (vv_scr, v, v_sw)):
                scr[0, rows, :] = jnp.where(lo_half, a, zero).astype(BF16)
                scr[1, rows, :] = jnp.where(lo_half, zero, a_sw).astype(BF16)
                scr[2, rows, :] = jnp.where(lo_half, a_sw, zero).astype(BF16)
                scr[3, rows, :] = jnp.where(lo_half, zero, a).astype(BF16)

    scale = hd ** -0.5

    def tile(nk, rope_rows):
        for c in range(GQA_Q_WIDTH // LANES):
            qc = head_rms(q_ref[0, :, c * LANES:(c + 1) * LANES].astype(F32), gq_ref[...])
            if rope_rows is not None:
                qc = rope(qc, rope_rows)
            qc = (qc * scale).astype(BF16)
            grp = (2 * c) // (GQA_Q_HEADS // GQA_KV_HEADS)
            acc = jnp.zeros((tq, LANES), F32)
            for hh in range(2):
                slot = 2 * grp + hh
                s = _dot_nt(qc, kk_scr[slot, 0:nk, :])
                m = jnp.max(s, axis=-1, keepdims=True)
                p = jnp.exp(s - m)
                den = jnp.sum(p, axis=-1, keepdims=True)
                acc = acc + _dot(p.astype(BF16), vv_scr[slot, 0:nk, :]) / den
            o_ref[0, :, c * LANES:(c + 1) * LANES] = acc.astype(o_ref.dtype)

    @pl.when(j == 0)
    def _():
        tile(n_ctx, None)

    @pl.when(j > 0)
    def _():
        tile(t, pl.ds(pl.multiple_of((j - 1) * tq, tq), tq))


def _gqa_attention(z, tables, gq, gk, n_ctx):
    b, t, _ = z.shape
    tq = ROW_TILE
    assert n_ctx == tq
    n_lat = t - n_ctx
    cos, sa, sb = tables
    tab_spec = pl.BlockSpec((n_lat, LANES), lambda bi, j: (0, 0))
    g_spec = pl.BlockSpec((1, LANES), lambda bi, j: (0, 0))
    return pl.pallas_call(
        functools.partial(_gqa_kernel, n_ctx=n_ctx),
        out_shape=jax.ShapeDtypeStruct((b, t, GQA_Q_WIDTH), BF16),
        grid=(b, t // tq),
        in_specs=[
            pl.BlockSpec((1, tq, GQA_Q_WIDTH), lambda bi, j: (bi, j, COL_GQA_Q // GQA_Q_WIDTH)),
            pl.BlockSpec((1, t, GQA_KV_WIDTH), lambda bi, j: (bi, 0, COL_GQA_K // GQA_KV_WIDTH)),
            pl.BlockSpec((1, t, GQA_KV_WIDTH), lambda bi, j: (bi, 0, COL_GQA_V // GQA_KV_WIDTH)),
            tab_spec, tab_spec, tab_spec, g_spec, g_spec,
        ],
        out_specs=pl.BlockSpec((1, tq, GQA_Q_WIDTH), lambda bi, j: (bi, j, 0)),
        scratch_shapes=[pltpu.VMEM((4, t, LANES), BF16), pltpu.VMEM((4, t, LANES), BF16)],
        compiler_params=_params(("arbitrary", "arbitrary")),
        name="gqa_attention",
    )(z, z, z, cos, sa, sb, gq, gk)


def _hgrn_kernel(q_ref, ff_ref, fb_ref, i_ref, g_ref, lbl_ref, ng_ref, o_ref,
                 of_scr, ob_scr, st_scr, *, layer, n_ctx):
    t = q_ref.shape[1]
    c = HG_CHUNK
    n = t // c
    nc = n_ctx // c
    depth = lbl_ref.shape[0]

    def lower_bound(direction):
        logits = [lbl_ref[d, direction, 0] for d in range(depth)]
        m = functools.reduce(jnp.maximum, logits)
        e = [jnp.exp(x - m) for x in logits]
        tot = functools.reduce(lambda a, b: a + b, e)
        p = [x / tot for x in e]
        cum = functools.reduce(lambda a, b: a + b, p[:layer + 1])
        return cum - p[0]

    row = lax.broadcasted_iota(jnp.int32, (c, c), 0)
    col = lax.broadcasted_iota(jnp.int32, (c, c), 1)

    def stream(off, f_ref, lb, causal, slot, out_scr):
        keep = (col <= row) if causal else (col >= row)
        tri = jnp.where(keep, 1.0, 0.0).astype(BF16)
        last = c - 1 if causal else 0
        rows = pl.ds(off, c)
        f = lb + (1.0 - lb) * jax.nn.sigmoid(f_ref[0, rows, :])
        kk = 1.0 - f
        g_hi, g_lo = _split_bf16(jnp.log(f))
        bcum = _dot(tri, g_hi) + _dot(tri, g_lo)
        b_mid = bcum[c // 2:c // 2 + 1, :]
        b_last = bcum[last:last + 1, :]
        q = _silu(q_ref[0, rows, :].astype(F32))
        v = i_ref[0, rows, :]
        qe = (q * jnp.exp(bcum - b_mid)).astype(BF16)
        ke = (kk * jnp.exp(b_mid - bcum)).astype(BF16)
        a = jnp.where(keep, _dot_nt(qe, ke), 0.0)
        st = st_scr[slot]
        o = _dot(a.astype(BF16), v) + _dot_nt((q * jnp.exp(bcum)).astype(BF16), st.astype(BF16))
        out_scr[rows, :] = o
        kd = (kk * jnp.exp(b_last - bcum)).astype(BF16)
        st_scr[slot] = st * jnp.exp(b_last) + _dot_tn(v, kd)

    lb_f = lower_bound(0)
    lb_b = lower_bound(1)
    st_scr[...] = jnp.zeros_like(st_scr)

    def body(k, carry):
        stream(pl.multiple_of(k * c, c), ff_ref, lb_f, True, 0, of_scr)
        kb = jnp.where(k < nc, nc - 1 - k, n + nc - 1 - k)
        stream(pl.multiple_of(kb * c, c), fb_ref, lb_b, False, 1, ob_scr)
        return carry

    lax.fori_loop(0, n, body, 0)

    ng = ng_ref[...]

    def readout(r, carry):
        rows = pl.ds(pl.multiple_of(r * ROW_TILE, ROW_TILE), ROW_TILE)
        o = of_scr[rows, :] + ob_scr[rows, :]
        y = (_rms(o) * ng) * _silu(g_ref[0, rows, :].astype(F32))
        o_ref[0, rows, :] = y.astype(o_ref.dtype)
        return carry

    lax.fori_loop(0, t // ROW_TILE, readout, 0)


def _hgrn(z, zf, lb_logits, norm_g, layer, n_ctx):
    b, t, _ = z.shape
    depth = lb_logits.shape[0]
    dk = HG_DK
    lbl = lb_logits.astype(F32).reshape(depth, 2, HG_HEADS, 1, dk)

    def zcol(base):
        return pl.BlockSpec((1, t, dk), lambda bi, h: (bi, 0, base // dk + h))

    return pl.pallas_call(
        functools.partial(_hgrn_kernel, layer=layer, n_ctx=n_ctx),
        out_shape=jax.ShapeDtypeStruct((b, t, HG_WIDTH), BF16),
        grid=(b, HG_HEADS),
        in_specs=[
            zcol(COL_HG_Q),
            pl.BlockSpec((1, t, dk), lambda bi, h: (bi, 0, h)),
            pl.BlockSpec((1, t, dk), lambda bi, h: (bi, 0, HG_HEADS + h)),
            zcol(COL_HG_I),
            zcol(COL_HG_G),
            pl.BlockSpec((depth, 2, 1, 1, dk), lambda bi, h: (0, 0, h, 0, 0)),
            pl.BlockSpec((1, dk), lambda bi, h: (0, 0)),
        ],
        out_specs=pl.BlockSpec((1, t, dk), lambda bi, h: (bi, 0, h)),
        scratch_shapes=[pltpu.VMEM((t, dk), F32), pltpu.VMEM((t, dk), F32),
                        pltpu.VMEM((2, dk, dk), F32)],
        compiler_params=_params(("arbitrary", "arbitrary")),
        name="hgrn2_scan",
    )(z, zf, zf, z, z, lbl, norm_g.reshape(1, dk).astype(F32))


def _merge_kernel(s_ref, a_ref, b_ref, c_ref, ga_ref, gb_ref, gc_ref, mod_ref, modc_ref,
                  wb_ref, wo_ref, gf_ref, wr_ref, br_ref, s_out, f_out, comb_out):
    is_ctx = pl.program_id(1) == 0

    def mod_row(k):
        return jnp.where(is_ctx, modc_ref[k:k + 1, :], mod_ref[0, k:k + 1, :])

    y = 0.0
    for idx, (br, gr) in enumerate(((a_ref, ga_ref), (b_ref, gb_ref), (c_ref, gc_ref))):
        y = y + jax.nn.sigmoid(gr[0].astype(F32)) * _dot(br[0], wb_ref[idx])
    x = s_ref[0] + mod_row(2) * _dot(y.astype(BF16), wo_ref[...])
    s_out[0] = x
    f = (_rms(x) * gf_ref[...]) * (1.0 + mod_row(4)) + mod_row(3)
    f_out[0] = f.astype(f_out.dtype)

    f_hi, f_lo = _split_bf16(f)
    w_hi, w_lo = _split_bf16(wr_ref[...])
    logits = _dot(f_hi, w_hi) + _dot(f_lo, w_hi) + _dot(f_hi, w_lo) + br_ref[...]

    lane = lax.broadcasted_iota(jnp.int32, logits.shape, 1)
    is_group = jnp.logical_and(lane >= N_EXPERTS, lane < N_EXPERTS + N_GROUPS)

    def first_argmax(x, x_max):
        return jnp.min(jnp.where(x == x_max, lane, LANES), axis=-1, keepdims=True)

    gl = jnp.where(is_group, logits, MASK_NEG)
    g_max = jnp.max(gl, axis=-1, keepdims=True)
    g_idx = first_argmax(gl, g_max) - N_EXPERTS
    p_group = 1.0 / jnp.sum(jnp.exp(gl - g_max), axis=-1, keepdims=True)
    in_group = jnp.logical_and(lane < N_EXPERTS, (lane // EXPERTS_PER_GROUP) == g_idx)
    e1 = jnp.where(in_group, logits, MASK_NEG)
    v1 = jnp.max(e1, axis=-1, keepdims=True)
    i1 = first_argmax(e1, v1)
    e2 = jnp.where(lane == i1, MASK_NEG, e1)
    v2 = jnp.max(e2, axis=-1, keepdims=True)
    i2 = first_argmax(e2, v2)
    r21 = jnp.exp(v2 - v1)
    w1 = 1.0 / (1.0 + r21)
    w2 = r21 * w1
    comb_out[0] = (jnp.where(lane == i1, w1 * p_group, 0.0)
                   + jnp.where(lane == i2, w2 * p_group, 0.0))


def _merge(s, a, bb, cc, z, mod, modc, wb, wo, gf, wr, br):
    b, t, d = s.shape
    tm = ROW_TILE
    bw = a.shape[-1]
    gate0 = COL_GATES // d

    def rows(width, colblk=0):
        return pl.BlockSpec((1, tm, width), lambda bi, j: (bi, j, colblk))

    def whole(shape):
        return pl.BlockSpec(shape, lambda bi, j: (0,) * len(shape))

    return pl.pallas_call(
        _merge_kernel,
        out_shape=(jax.ShapeDtypeStruct((b, t, d), F32),
                   jax.ShapeDtypeStruct((b, t, d), BF16),
                   jax.ShapeDtypeStruct((b, t, LANES), F32)),
        grid=(b, t // tm),
        in_specs=[
            rows(d), rows(bw), rows(bw), rows(bw),
            rows(d, gate0), rows(d, gate0 + 1), rows(d, gate0 + 2),
            pl.BlockSpec((1, N_MOD, d), lambda bi, j: (bi, 0, 0)),
            whole((N_MOD, d)),
            whole(wb.shape), whole(wo.shape), whole((1, d)), whole(wr.shape), whole((1, LANES)),
        ],
        out_specs=(rows(d), rows(d), rows(LANES)),
        compiler_params=_params(("arbitrary", "arbitrary")),
        name="merge_router",
    )(s, a, bb, cc, z, z, z, mod, modc, wb, wo, gf, wr, br)


MOE_ROWS = 768


def _moe_kernel(f_ref, comb_ref, s_ref, mod_ref, modc_ref, wg_ref, wu_ref, wd_ref, o_ref, acc_scr,
                *, n_ctx, tiles_per_batch):
    e = pl.program_id(1)
    tm = f_ref.shape[1]

    @pl.when(e == 0)
    def _():
        acc_scr[...] = jnp.zeros_like(acc_scr)

    wg = wg_ref[0, 0].astype(BF16)
    wu = wu_ref[0, 0].astype(BF16)
    wd = wd_ref[0, 0].astype(BF16)
    lane = lax.broadcasted_iota(jnp.int32, (ROW_TILE, LANES), 1)

    def body(r, carry):
        rows = pl.ds(pl.multiple_of(r * ROW_TILE, ROW_TILE), ROW_TILE)
        x = f_ref[0, rows, :]
        he = (_silu(_dot(x, wg)) * _dot(x, wu)).astype(BF16)
        y = _dot(he, wd)
        cw = jnp.sum(jnp.where(lane == e, comb_ref[0, rows, :], 0.0), axis=-1, keepdims=True)
        acc_scr[rows, :] += cw * y
        return carry

    lax.fori_loop(0, tm // ROW_TILE, body, 0)

    @pl.when(e == N_EXPERTS - 1)
    def _():
        first = (pl.program_id(0) % tiles_per_batch) == 0
        row = lax.broadcasted_iota(jnp.int32, (tm, 1), 0)
        is_ctx = jnp.logical_and(first, row < n_ctx)
        scale = jnp.where(is_ctx, modc_ref[5:6, :], mod_ref[0, 5:6, :])
        o_ref[0] = s_ref[0] + scale * acc_scr[...]


def _moe(f, comb, s, mod, modc, w_gate, w_up, w_down, layer, n_ctx):
    b, t, d = s.shape
    tm = MOE_ROWS
    tpb = t // tm
    hid = w_gate.shape[-1]

    def rows(width):
        return pl.BlockSpec((1, tm, width), lambda i, e: (i // tpb, i % tpb, 0))

    return pl.pallas_call(
        functools.partial(_moe_kernel, n_ctx=n_ctx, tiles_per_batch=tpb),
        out_shape=jax.ShapeDtypeStruct((b, t, d), F32),
        grid=(b * tpb, N_EXPERTS),
        in_specs=[
            rows(d), rows(LANES), rows(d),
            pl.BlockSpec((1, N_MOD, d), lambda i, e: (i // tpb, 0, 0)),
            pl.BlockSpec((N_MOD, d), lambda i, e: (0, 0)),
            pl.BlockSpec((1, 1, d, hid), lambda i, e: (layer, e, 0, 0)),
            pl.BlockSpec((1, 1, d, hid), lambda i, e: (layer, e, 0, 0)),
            pl.BlockSpec((1, 1, hid, d), lambda i, e: (layer, e, 0, 0)),
        ],
        out_specs=rows(d),
        scratch_shapes=[pltpu.VMEM((tm, d), F32)],
        compiler_params=_params(("arbitrary", "arbitrary")),
        name="moe_experts",
    )(f, comb, s, mod, modc, w_gate, w_up, w_down)


def _final_kernel(s_ref, g_ref, o_ref):
    o_ref[0] = _rms(s_ref[0]) * g_ref[...]


def _final_norm(s, g, n_ctx):
    b, t, d = s.shape
    tm = ROW_TILE
    skip = n_ctx // tm
    return pl.pallas_call(
        _final_kernel,
        out_shape=jax.ShapeDtypeStruct((b, t - n_ctx, d), F32),
        grid=(b, (t - n_ctx) // tm),
        in_specs=[pl.BlockSpec((1, tm, d), lambda bi, j: (bi, j + skip, 0)),
                  pl.BlockSpec((1, d), lambda bi, j: (0, 0))],
        out_specs=pl.BlockSpec((1, tm, d), lambda bi, j: (bi, j, 0)),
        compiler_params=_params(("arbitrary", "arbitrary")),
        name="final_norm",
    )(s, g)


def kernel(x, c, ctx, c_ctx, w_ada, b_ada, g_mix, g_ffn, w_in, na_rpb, hg_lb_logits, hg_norm_g,
           gqa_qnorm_g, gqa_knorm_g, w_branch, w_out, w_group_router, b_group_router,
           w_expert_router, b_expert_router, w_exp_gate, w_exp_up, w_exp_down, g_final):
    b, n_lat, d = x.shape
    n_ctx = ctx.shape[1]
    depth = w_in.shape[0]
    assert n_ctx % ROW_TILE == 0 and n_lat % ROW_TILE == 0 and (n_ctx + n_lat) % MOE_ROWS == 0

    s = jnp.concatenate([ctx, x], axis=1)

    c_rows = 16
    c_all = jnp.concatenate([c, c_ctx[None, :], jnp.zeros((c_rows - b - 1, d), c.dtype)], axis=0)
    mod_all = _ada(c_all, w_ada, b_ada).reshape(depth, c_rows, N_MOD, d)

    tables = _rope_tables(n_lat)
    rep = LANES // GQA_HEAD_DIM

    for l in range(depth):
        mod = mod_all[l, :b]
        modc = mod_all[l, b]
        w_pad = jnp.concatenate(
            [w_in[l, :, :COL_RAW_GATES],
             jnp.zeros((d, COL_GATES - COL_RAW_GATES), w_in.dtype),
             w_in[l, :, COL_RAW_GATES:]], axis=1).astype(BF16)
        z, zf = _inproj(s, g_mix[l].reshape(1, d), modc, mod, w_pad, n_ctx)

        a = _na_attention(z, _na_bias_table(na_rpb[l]), n_ctx)
        cc = _gqa_attention(z, tables,
                            jnp.tile(gqa_qnorm_g[l].astype(F32), rep).reshape(1, LANES),
                            jnp.tile(gqa_knorm_g[l].astype(F32), rep).reshape(1, LANES), n_ctx)
        bb = _hgrn(z, zf, hg_lb_logits, hg_norm_g[l], l, n_ctx)

        wr = jnp.concatenate(
            [w_expert_router[l], w_group_router[l],
             jnp.zeros((d, LANES - N_EXPERTS - N_GROUPS), F32)], axis=1)
        br = jnp.concatenate(
            [b_expert_router[l], b_group_router[l],
             jnp.zeros((LANES - N_EXPERTS - N_GROUPS,), F32)]).reshape(1, LANES)
        s, f, comb = _merge(s, a, bb, cc, z, mod, modc, w_branch[l].astype(BF16),
                            w_out[l].astype(BF16), g_ffn[l].reshape(1, d), wr, br)
        s = _moe(f, comb, s, mod, modc, w_exp_gate, w_exp_up, w_exp_down, l, n_ctx)

    return _final_norm(s, g_final.reshape(1, d), n_ctx)
```

```python
import functools

import numpy as np
import jax
import jax.numpy as jnp
from jax import lax
from jax.experimental import pallas as pl
from jax.experimental.pallas import tpu as pltpu

F32 = jnp.float32
BF16 = jnp.bfloat16

RMS_EPS = 1e-6
N_MOD = 6
GRID_W = 64

NA_HEADS = 8
NA_HEAD_DIM = 64
NA_WIDTH = NA_HEADS * NA_HEAD_DIM
WIN_ROWS = 8
WIN_COLS = 16
NA_QROWS = 4
NA_KROWS = 12

HG_HEADS = 4
HG_DK = 128
HG_WIDTH = HG_HEADS * HG_DK
HG_CHUNK = 64

GQA_Q_HEADS = 8
GQA_KV_HEADS = 2
GQA_HEAD_DIM = 64
GQA_Q_WIDTH = GQA_Q_HEADS * GQA_HEAD_DIM
GQA_KV_WIDTH = GQA_KV_HEADS * GQA_HEAD_DIM
ROPE_THETA = 10000.0

N_GROUPS = 4
EXPERTS_PER_GROUP = 4
N_EXPERTS = N_GROUPS * EXPERTS_PER_GROUP
TOP_K = 2

LANES = 128
SUBLANES = 8
ROW_TILE = 256
WIDE_ROWS = 768
MASK_NEG = -1e30

COL_NA_Q = 0
COL_NA_K = 512
COL_NA_V = 1024
COL_HG_Q = 1536
COL_HG_FF = 2048
COL_HG_FB = 2560
COL_HG_I = 3072
COL_HG_G = 3584
COL_GQA_Q = 4096
COL_GQA_K = 4608
COL_GQA_V = 4736
COL_RAW_GATES = 4864
COL_GATES = 5120
IN_COLS_PAD = 8192
IN_TILE = 512

VMEM_LIMIT = 56 * 1024 * 1024


def _dot(a, b):
    return jnp.dot(a, b, preferred_element_type=F32)


def _dot_nt(a, b):
    return lax.dot_general(a, b, (((1,), (1,)), ((), ())), preferred_element_type=F32)


def _dot_tn(a, b):
    return lax.dot_general(a, b, (((0,), (0,)), ((), ())), preferred_element_type=F32)


def _split_bf16(x):
    hi = x.astype(BF16)
    lo = (x - hi.astype(F32)).astype(BF16)
    return hi, lo


def _silu(x):
    return x * jax.nn.sigmoid(x)


def _rms(x):
    return x * lax.rsqrt(jnp.mean(x * x, axis=-1, keepdims=True) + RMS_EPS)


def _params(semantics, vmem=VMEM_LIMIT):
    return pltpu.CompilerParams(dimension_semantics=semantics, vmem_limit_bytes=vmem)


def _ada_kernel(c_ref, w_ref, b_ref, o_ref):
    sc = _silu(c_ref[...]).astype(BF16)
    o_ref[0] = _dot(sc, w_ref[0].astype(BF16)) + b_ref[0]


def _ada(c_all, w_ada, b_ada):
    depth, d, n = w_ada.shape
    rows = c_all.shape[0]
    tn = 1536
    return pl.pallas_call(
        _ada_kernel,
        out_shape=jax.ShapeDtypeStruct((depth, rows, n), F32),
        grid=(depth, n // tn),
        in_specs=[
            pl.BlockSpec((rows, d), lambda l, j: (0, 0)),
            pl.BlockSpec((1, d, tn), lambda l, j: (l, 0, j)),
            pl.BlockSpec((1, 1, tn), lambda l, j: (l, 0, j)),
        ],
        out_specs=pl.BlockSpec((1, rows, tn), lambda l, j: (l, 0, j)),
        compiler_params=_params(("arbitrary", "arbitrary")),
        name="ada_mod",
    )(c_all, w_ada, b_ada.reshape(depth, 1, n))


def _inproj_kernel(*refs, n_ctx, tiles_per_batch, residual):
    if residual:
        (s_ref, y_ref, modp_ref, modcp_ref, g_ref, modc_ref, mod_ref, w_ref,
         z_ref, zf_ref, s_out, h_scr) = refs
    else:
        s_ref, g_ref, modc_ref, mod_ref, w_ref, z_ref, zf_ref, h_scr = refs
    i = pl.program_id(0)
    j = pl.program_id(1)
    tm = s_ref.shape[1]

    @pl.when(j == 0)
    def _():
        g = g_ref[...]
        first = (i % tiles_per_batch) == 0
        for r in range(tm // ROW_TILE):
            rows = slice(r * ROW_TILE, (r + 1) * ROW_TILE)
            is_ctx = jnp.logical_and(first, r * ROW_TILE < n_ctx)

            def pick(ctx_ref, lat_ref, k):
                return jnp.where(is_ctx, ctx_ref[k:k + 1, :], lat_ref[0, k:k + 1, :])

            x = s_ref[0, rows, :]
            if residual:
                x = x + pick(modcp_ref, modp_ref, 5) * y_ref[0, rows, :]
                s_out[0, rows, :] = x
            h = (_rms(x) * g) * (1.0 + pick(modc_ref, mod_ref, 1)) + pick(modc_ref, mod_ref, 0)
            h_scr[rows, :] = h.astype(BF16)

    acc = _dot(h_scr[...], w_ref[...])
    z_ref[0] = acc.astype(BF16)
    f_lo = COL_HG_FF // IN_TILE

    @pl.when(jnp.logical_or(j == f_lo, j == f_lo + 1))
    def _():
        zf_ref[0] = acc


def _inproj(s, g, modc, mod, w_pad, n_ctx, prev=None):
    b, t, d = s.shape
    n = w_pad.shape[1]
    tm = WIDE_ROWS
    tpb = t // tm
    f_lo = COL_HG_FF // IN_TILE
    residual = prev is not None

    row_spec = pl.BlockSpec((1, tm, d), lambda i, j: (i // tpb, i % tpb, 0))
    mod_spec = pl.BlockSpec((1, N_MOD, d), lambda i, j: (i // tpb, 0, 0))
    modc_spec = pl.BlockSpec((N_MOD, d), lambda i, j: (0, 0))
    in_specs = [row_spec]
    args = [s]
    if residual:
        in_specs += [row_spec, mod_spec, modc_spec]
        args += [prev[0], prev[1], prev[2]]
    in_specs += [pl.BlockSpec((1, d), lambda i, j: (0, 0)), modc_spec, mod_spec,
                 pl.BlockSpec((d, IN_TILE), lambda i, j: (0, j))]
    args += [g, modc, mod, w_pad]
    out_shape = [jax.ShapeDtypeStruct((b, t, n), BF16),
                 jax.ShapeDtypeStruct((b, t, 2 * IN_TILE), F32)]
    out_specs = [
        pl.BlockSpec((1, tm, IN_TILE), lambda i, j: (i // tpb, i % tpb, j)),
        pl.BlockSpec((1, tm, IN_TILE), lambda i, j: (i // tpb, i % tpb, jnp.clip(j - f_lo, 0, 1))),
    ]
    if residual:
        out_shape.append(jax.ShapeDtypeStruct((b, t, d), F32))
        out_specs.append(row_spec)
    outs = pl.pallas_call(
        functools.partial(_inproj_kernel, n_ctx=n_ctx, tiles_per_batch=tpb, residual=residual),
        out_shape=tuple(out_shape),
        grid=(b * tpb, n // IN_TILE),
        in_specs=in_specs,
        out_specs=tuple(out_specs),
        scratch_shapes=[pltpu.VMEM((tm, d), BF16)],
        compiler_params=_params(("arbitrary", "arbitrary")),
        name="in_proj",
    )(*args)
    return outs if residual else (outs[0], outs[1], s)


def _na_block_start(blk, rows):
    return jnp.clip(NA_QROWS * blk - WIN_ROWS // 2, 0, rows - NA_KROWS)


def _na_bias_table(rpb, rows):
    n_blk = rows // NA_QROWS
    assert rows % NA_QROWS == 0 and n_blk >= 3 and rows >= NA_KROWS
    qc = np.arange(GRID_W)[:, None]
    kc = np.arange(GRID_W)[None, :]
    c0 = np.clip(qc - WIN_COLS // 2, 0, GRID_W - WIN_COLS)
    col_ok = (kc >= c0) & (kc < c0 + WIN_COLS)
    dcol = np.clip(kc - qc + WIN_COLS - 1, 0, 2 * WIN_COLS - 2)
    j = np.arange(NA_QROWS)[:, None]
    i = np.arange(NA_KROWS)[None, :]
    row_ok, drow = [], []
    for blk in (0, 1, n_blk - 1):
        u0 = int(np.clip(NA_QROWS * blk - WIN_ROWS // 2, 0, rows - NA_KROWS))
        r = NA_QROWS * blk + j
        r0 = np.clip(r - WIN_ROWS // 2, 0, rows - WIN_ROWS)
        krow = u0 + i
        row_ok.append((krow >= r0) & (krow < r0 + WIN_ROWS))
        drow.append(np.clip(krow - r + WIN_ROWS - 1, 0, 2 * WIN_ROWS - 2))
    row_ok = np.stack(row_ok)
    drow = np.stack(drow)
    tab = rpb.astype(F32)[:, drow][..., dcol]
    ok = row_ok[None, :, :, :, None, None] & col_ok[None, None, None, None]
    tab = jnp.where(ok, tab, MASK_NEG)
    h = rpb.shape[0]
    return tab.transpose(0, 1, 2, 4, 3, 5).reshape(h, 3, NA_QROWS * GRID_W, NA_KROWS * GRID_W)


def _na_kernel(q_ref, k_ref, v_ref, bias_ref, o_ref, *, n_ctx, rows):
    i = pl.program_id(1)
    tq = q_ref.shape[1]
    lane = lax.broadcasted_iota(jnp.int32, (tq, LANES), 1)
    lo_half = lane < NA_HEAD_DIM

    def attend(q2, parts):
        outs = []
        for hh in range(2):
            keep = lo_half if hh == 0 else jnp.logical_not(lo_half)
            qm = jnp.where(keep, q2, jnp.zeros_like(q2))
            scores = []
            for kk, _, bias in parts:
                s = _dot_nt(qm, kk)
                if bias is not None:
                    s = s + bias[hh]
                scores.append(s)
            m = functools.reduce(jnp.maximum, [jnp.max(s, axis=-1, keepdims=True) for s in scores])
            den = 0.0
            acc = 0.0
            for s, (_, vv, _) in zip(scores, parts):
                p = jnp.exp(s - m)
                den = den + jnp.sum(p, axis=-1, keepdims=True)
                acc = acc + _dot(p.astype(BF16), vv)
            outs.append(acc / den)
        return jnp.where(lo_half, outs[0], outs[1])

    scale = NA_HEAD_DIM ** -0.5

    @pl.when(i == 0)
    def _():
        for hp in range(NA_HEADS // 2):
            cols = slice(hp * LANES, (hp + 1) * LANES)
            q2 = q_ref[0, :, cols] * scale
            parts = [(k_ref[0, 0:n_ctx, cols], v_ref[0, 0:n_ctx, cols], None)]
            o_ref[0, :, cols] = attend(q2, parts).astype(o_ref.dtype)

    @pl.when(i > 0)
    def _():
        u0 = _na_block_start(i - 1, rows)
        krows = pl.ds(pl.multiple_of(n_ctx + u0 * GRID_W, GRID_W), NA_KROWS * GRID_W)
        for hp in range(NA_HEADS // 2):
            cols = slice(hp * LANES, (hp + 1) * LANES)
            q2 = q_ref[0, :, cols] * scale
            bias = (bias_ref[2 * hp, 0], bias_ref[2 * hp + 1, 0])
            parts = [(k_ref[0, krows, cols], v_ref[0, krows, cols], bias),
                     (k_ref[0, 0:n_ctx, cols], v_ref[0, 0:n_ctx, cols], None)]
            o_ref[0, :, cols] = attend(q2, parts).astype(o_ref.dtype)


def _na_attention(z, bias_tab, n_ctx):
    b, t, _ = z.shape
    rows = (t - n_ctx) // GRID_W
    tq = NA_QROWS * GRID_W
    assert n_ctx == tq
    n_blk = rows // NA_QROWS
    wk = NA_KROWS * GRID_W

    def pattern(i):
        return jnp.where(i <= 1, 0, jnp.where(i == n_blk, 2, 1))

    return pl.pallas_call(
        functools.partial(_na_kernel, n_ctx=n_ctx, rows=rows),
        out_shape=jax.ShapeDtypeStruct((b, t, NA_WIDTH), BF16),
        grid=(b, 1 + n_blk),
        in_specs=[
            pl.BlockSpec((1, tq, NA_WIDTH), lambda bi, i: (bi, i, COL_NA_Q // NA_WIDTH)),
            pl.BlockSpec((1, t, NA_WIDTH), lambda bi, i: (bi, 0, COL_NA_K // NA_WIDTH)),
            pl.BlockSpec((1, t, NA_WIDTH), lambda bi, i: (bi, 0, COL_NA_V // NA_WIDTH)),
            pl.BlockSpec((NA_HEADS, 1, tq, wk), lambda bi, i: (0, pattern(i), 0, 0)),
        ],
        out_specs=pl.BlockSpec((1, tq, NA_WIDTH), lambda bi, i: (bi, i, 0)),
        compiler_params=_params(("arbitrary", "arbitrary")),
        name="na_attention",
    )(z, z, z, bias_tab)


def _rope_tables(n_tokens):
    t = jnp.arange(n_tokens)
    pos = jnp.stack([t // GRID_W, t % GRID_W], axis=-1).astype(F32)
    n_freq = GQA_HEAD_DIM // 4
    inv_freq = jnp.power(ROPE_THETA, -jnp.arange(n_freq, dtype=F32) / n_freq)
    ang = pos[:, :, None] * inv_freq
    ang = jnp.concatenate([ang, ang], axis=-1).reshape(n_tokens, GQA_HEAD_DIM)
    cos, sin = jnp.cos(ang), jnp.sin(ang)
    first = (np.arange(GQA_HEAD_DIM) % (2 * n_freq)) < n_freq
    sin_a = jnp.where(first, -sin, 0.0)
    sin_b = jnp.where(first, 0.0, sin)
    rep = LANES // GQA_HEAD_DIM
    return tuple(jnp.tile(a, (1, rep)) for a in (cos, sin_a, sin_b))


def _gqa_kernel(q_ref, k_ref, v_ref, cos_ref, sa_ref, sb_ref, gq_ref, gk_ref, o_ref,
                kk_scr, vv_scr, *, n_ctx):
    j = pl.program_id(1)
    t = k_ref.shape[1]
    tq = q_ref.shape[1]
    hd = GQA_HEAD_DIM
    quarter = hd // 4

    r_i = lax.broadcasted_iota(jnp.int32, (LANES, LANES), 0)
    c_i = lax.broadcasted_iota(jnp.int32, (LANES, LANES), 1)
    head_ones = jnp.where((r_i // hd) == (c_i // hd), 1.0, 0.0).astype(BF16)
    lo_half = lax.broadcasted_iota(jnp.int32, (tq, LANES), 1) < hd

    def head_rms(x, g):
        hi, lo = _split_bf16(x * x)
        ms = (_dot(hi, head_ones) + _dot(lo, head_ones)) * (1.0 / hd)
        return (x * lax.rsqrt(ms + RMS_EPS)) * g

    def rope(x, rows):
        return (x * cos_ref[rows, :]
                + pltpu.roll(x, LANES - quarter, 1) * sa_ref[rows, :]
                + pltpu.roll(x, quarter, 1) * sb_ref[rows, :])

    @pl.when(j == 0)
    def _():
        zero = jnp.zeros((tq, LANES), F32)
        for r in range(t // tq):
            rows = slice(r * tq, (r + 1) * tq)
            k = head_rms(k_ref[0, rows, :].astype(F32), gk_ref[...])
            if r * tq >= n_ctx:
                k = rope(k, slice(r * tq - n_ctx, (r + 1) * tq - n_ctx))
            v = v_ref[0, rows, :].astype(F32)
            k_sw = pltpu.roll(k, hd, 1)
            v_sw = pltpu.roll(v, hd, 1)
            for scr, a, a_sw in ((kk_scr, k, k_sw), (vv_scr, v, v_sw)):
                scr[0, rows, :] = jnp.where(lo_half, a, zero).astype(BF16)
                scr[1, rows, :] = jnp.where(lo_half, zero, a_sw).astype(BF16)
                scr[2, rows, :] = jnp.where(lo_half, a_sw, zero).astype(BF16)
                scr[3, rows, :] = jnp.where(lo_half, zero, a).astype(BF16)

    scale = hd ** -0.5

    def tile(nk, rope_rows):
        for c in range(GQA_Q_WIDTH // LANES):
            qc = head_rms(q_ref[0, :, c * LANES:(c + 1) * LANES].astype(F32), gq_ref[...])
            if rope_rows is not None:
                qc = rope(qc, rope_rows)
            qc = (qc * scale).astype(BF16)
            grp = (2 * c) // (GQA_Q_HEADS // GQA_KV_HEADS)
            acc = jnp.zeros((tq, LANES), F32)
            for hh in range(2):
                slot = 2 * grp + hh
                s = _dot_nt(qc, kk_scr[slot, 0:nk, :])
                m = jnp.max(s, axis=-1, keepdims=True)
                p = jnp.exp(s - m)
                den = jnp.sum(p, axis=-1, keepdims=True)
                acc = acc + _dot(p.astype(BF16), vv_scr[slot, 0:nk, :]) / den
            o_ref[0, :, c * LANES:(c + 1) * LANES] = acc.astype(o_ref.dtype)

    @pl.when(j == 0)
    def _():
        tile(n_ctx, None)

    @pl.when(j > 0)
    def _():
        tile(t, pl.ds(pl.multiple_of((j - 1) * tq, tq), tq))


def _gqa_attention(z, tables, gq, gk, n_ctx):
    b, t, _ = z.shape
    tq = ROW_TILE
    assert n_ctx == tq
    n_lat = t - n_ctx
    cos, sa, sb = tables
    tab_spec = pl.BlockSpec((n_lat, LANES), lambda bi, j: (0, 0))
    g_spec = pl.BlockSpec((1, LANES), lambda bi, j: (0, 0))
    return pl.pallas_call(
        functools.partial(_gqa_kernel, n_ctx=n_ctx),
        out_shape=jax.ShapeDtypeStruct((b, t, GQA_Q_WIDTH), BF16),
        grid=(b, t // tq),
        in_specs=[
            pl.BlockSpec((1, tq, GQA_Q_WIDTH), lambda bi, j: (bi, j, COL_GQA_Q // GQA_Q_WIDTH)),
            pl.BlockSpec((1, t, GQA_KV_WIDTH), lambda bi, j: (bi, 0, COL_GQA_K // GQA_KV_WIDTH)),
            pl.BlockSpec((1, t, GQA_KV_WIDTH), lambda bi, j: (bi, 0, COL_GQA_V // GQA_KV_WIDTH)),
            tab_spec, tab_spec, tab_spec, g_spec, g_spec,
        ],
        out_specs=pl.BlockSpec((1, tq, GQA_Q_WIDTH), lambda bi, j: (bi, j, 0)),
        scratch_shapes=[pltpu.VMEM((4, t, LANES), BF16), pltpu.VMEM((4, t, LANES), BF16)],
        compiler_params=_params(("arbitrary", "arbitrary")),
        name="gqa_attention",
    )(z, z, z, cos, sa, sb, gq, gk)


HG_LEVELS = 6


def _hgrn_sum_matrices():
    c = HG_CHUNK
    u = np.arange(c)[None, :]
    r = np.arange(c)[:, None]
    fw, bw = [], []
    for lvl in range(HG_LEVELS):
        hs = c >> (lvl + 1)
        blk = (r // (2 * hs)) * (2 * hs)
        upper = (r % (2 * hs)) >= hs
        last_lower = blk + hs - 1
        first_upper = blk + hs
        fw.append(np.where(upper, (u > last_lower) & (u <= r), (u > r) & (u <= last_lower)))
        bw.append(np.where(upper, (u >= first_upper) & (u < r), (u >= r) & (u < first_upper)))
    fw += [u <= r, u > r]
    bw += [u >= r, u < r]
    return (jnp.asarray(np.concatenate(fw), BF16), jnp.asarray(np.concatenate(bw), BF16))


def _hgrn_pair_masks():
    c = HG_CHUNK
    row = np.arange(c)[:, None]
    col = np.arange(c)[None, :]
    fw, bw = [], []
    for lvl in range(HG_LEVELS):
        hs = c >> (lvl + 1)
        same = (row // (2 * hs)) == (col // (2 * hs))
        row_up = (row % (2 * hs)) >= hs
        col_up = (col % (2 * hs)) >= hs
        fw.append(same & row_up & ~col_up)
        bw.append(same & ~row_up & col_up)
    fw.append(row == col)
    bw.append(row == col)
    return jnp.asarray(np.stack(fw), F32), jnp.asarray(np.stack(bw), F32)


def _hgrn_kernel(q_ref, ff_ref, fb_ref, i_ref, g_ref, lbl_ref, ng_ref, wf_ref, wb_ref, mf_ref, mb_ref,
                 o_ref, o_scr, st_scr, *, layer, n_ctx):
    t = q_ref.shape[1]
    c = HG_CHUNK
    dk = HG_DK
    n = t // c
    nc = n_ctx // c
    depth = lbl_ref.shape[0]

    def lower_bound(direction):
        logits = [lbl_ref[d, direction] for d in range(depth)]
        m = functools.reduce(jnp.maximum, logits)
        e = [jnp.exp(x - m) for x in logits]
        tot = functools.reduce(lambda a, b: a + b, e)
        p = [x / tot for x in e]
        cum = functools.reduce(lambda a, b: a + b, p[:layer + 1])
        return cum - p[0]

    def stream(off, f_ref, lb, w_ref, m_ref, exit_row, slot, cols):
        rows = pl.ds(off, c)
        f = lb + (1.0 - lb) * jax.nn.sigmoid(f_ref[0, rows, cols])
        kk = 1.0 - f
        g_hi, g_lo = _split_bf16(jnp.log(f))
        w = w_ref[...]
        decay = jnp.exp(_dot(w, g_hi) + _dot(w, g_lo))
        q = _silu(q_ref[0, rows, cols].astype(F32))
        v = i_ref[0, rows, cols]
        a = m_ref[HG_LEVELS] * _dot_nt(q.astype(BF16), kk.astype(BF16))
        for lvl in range(HG_LEVELS):
            y = decay[lvl * c:(lvl + 1) * c, :]
            a = a + m_ref[lvl] * _dot_nt((q * y).astype(BF16), (kk * y).astype(BF16))
        e_cum = decay[HG_LEVELS * c:(HG_LEVELS + 1) * c, :]
        e_rest = decay[(HG_LEVELS + 1) * c:(HG_LEVELS + 2) * c, :]
        st = st_scr[slot]
        o = _dot(a.astype(BF16), v) + _dot_nt((q * e_cum).astype(BF16), st.astype(BF16))
        o_scr[rows, cols] += o
        kd = (kk * e_rest).astype(BF16)
        st_scr[slot] = st * e_cum[exit_row:exit_row + 1, :] + _dot_tn(v, kd)

    lb_f = lower_bound(0)
    lb_b = lower_bound(1)
    st_scr[...] = jnp.zeros_like(st_scr)
    o_scr[...] = jnp.zeros_like(o_scr)

    def body(k, carry):
        off_f = pl.multiple_of(k * c, c)
        kb = jnp.where(k < nc, nc - 1 - k, n + nc - 1 - k)
        off_b = pl.multiple_of(kb * c, c)
        for h in range(HG_HEADS):
            cols = slice(h * dk, (h + 1) * dk)
            stream(off_f, ff_ref, lb_f[:, cols], wf_ref, mf_ref, c - 1, 2 * h, cols)
            stream(off_b, fb_ref, lb_b[:, cols], wb_ref, mb_ref, 0, 2 * h + 1, cols)
        return carry

    lax.fori_loop(0, n, body, 0)

    ng = ng_ref[...]

    def readout(r, carry):
        rows = pl.ds(pl.multiple_of(r * ROW_TILE, ROW_TILE), ROW_TILE)
        for h in range(HG_HEADS):
            cols = slice(h * dk, (h + 1) * dk)
            y = (_rms(o_scr[rows, cols]) * ng) * _silu(g_ref[0, rows, cols].astype(F32))
            o_ref[0, rows, cols] = y.astype(o_ref.dtype)
        return carry

    lax.fori_loop(0, t // ROW_TILE, readout, 0)


def _hgrn(z, zf, lb_logits, norm_g, layer, n_ctx):
    b, t, _ = z.shape
    depth = lb_logits.shape[0]
    w = HG_WIDTH
    lbl = lb_logits.astype(F32).reshape(depth, 2, 1, w)
    wf, wb = _hgrn_sum_matrices()
    mf, mb = _hgrn_pair_masks()

    def zcol(base):
        return pl.BlockSpec((1, t, w), lambda bi: (bi, 0, base // w))

    def whole(shape):
        return pl.BlockSpec(shape, lambda bi: (0,) * len(shape))

    return pl.pallas_call(
        functools.partial(_hgrn_kernel, layer=layer, n_ctx=n_ctx),
        out_shape=jax.ShapeDtypeStruct((b, t, w), BF16),
        grid=(b,),
        in_specs=[
            zcol(COL_HG_Q),
            pl.BlockSpec((1, t, w), lambda bi: (bi, 0, 0)),
            pl.BlockSpec((1, t, w), lambda bi: (bi, 0, 1)),
            zcol(COL_HG_I),
            zcol(COL_HG_G),
            whole(lbl.shape), whole((1, HG_DK)), whole(wf.shape), whole(wb.shape),
            whole(mf.shape), whole(mb.shape),
        ],
        out_specs=pl.BlockSpec((1, t, w), lambda bi: (bi, 0, 0)),
        scratch_shapes=[pltpu.VMEM((t, w), F32), pltpu.VMEM((2 * HG_HEADS, HG_DK, HG_DK), F32)],
        compiler_params=_params(("arbitrary",)),
        name="hgrn2_scan",
    )(z, zf, zf, z, z, lbl, norm_g.reshape(1, HG_DK).astype(F32), wf, wb, mf, mb)


def _merge_kernel(s_ref, a_ref, b_ref, c_ref, ga_ref, gb_ref, gc_ref, mod_ref, modc_ref,
                  wb_ref, wo_ref, gf_ref, wr_ref, br_ref, s_out, f_out, route_out):
    is_ctx = pl.program_id(1) == 0

    def mod_row(k):
        return jnp.where(is_ctx, modc_ref[k:k + 1, :], mod_ref[0, k:k + 1, :])

    y = 0.0
    for idx, (br, gr) in enumerate(((a_ref, ga_ref), (b_ref, gb_ref), (c_ref, gc_ref))):
        y = y + jax.nn.sigmoid(gr[0].astype(F32)) * _dot(br[0], wb_ref[idx])
    x = s_ref[0] + mod_row(2) * _dot(y.astype(BF16), wo_ref[...])
    s_out[0] = x
    f = (_rms(x) * gf_ref[...]) * (1.0 + mod_row(4)) + mod_row(3)
    f_out[0] = f

    f_hi, f_lo = _split_bf16(f)
    w_hi, w_lo = _split_bf16(wr_ref[...])
    logits = _dot(f_hi, w_hi) + _dot(f_lo, w_hi) + _dot(f_hi, w_lo) + br_ref[...]

    lane = lax.broadcasted_iota(jnp.int32, logits.shape, 1)
    is_group = jnp.logical_and(lane >= N_EXPERTS, lane < N_EXPERTS + N_GROUPS)

    def first_argmax(x, x_max):
        return jnp.min(jnp.where(x == x_max, lane, LANES), axis=-1, keepdims=True)

    gl = jnp.where(is_group, logits, MASK_NEG)
    g_max = jnp.max(gl, axis=-1, keepdims=True)
    g_idx = first_argmax(gl, g_max) - N_EXPERTS
    p_group = 1.0 / jnp.sum(jnp.exp(gl - g_max), axis=-1, keepdims=True)
    in_group = jnp.logical_and(lane < N_EXPERTS, (lane // EXPERTS_PER_GROUP) == g_idx)
    e1 = jnp.where(in_group, logits, MASK_NEG)
    v1 = jnp.max(e1, axis=-1, keepdims=True)
    i1 = first_argmax(e1, v1)
    e2 = jnp.where(lane == i1, MASK_NEG, e1)
    v2 = jnp.max(e2, axis=-1, keepdims=True)
    i2 = first_argmax(e2, v2)
    r21 = jnp.exp(v2 - v1)
    w1 = 1.0 / (1.0 + r21)
    w2 = r21 * w1
    route = jnp.where(lane == 0, i1.astype(F32), 0.0)
    route = jnp.where(lane == 1, i2.astype(F32), route)
    route = jnp.where(lane == 2, w1 * p_group, route)
    route = jnp.where(lane == 3, w2 * p_group, route)
    route_out[0] = route


def _merge(s, a, bb, cc, z, mod, modc, wb, wo, gf, wr, br):
    b, t, d = s.shape
    tm = ROW_TILE
    bw = a.shape[-1]
    gate0 = COL_GATES // d

    def rows(width, colblk=0):
        return pl.BlockSpec((1, tm, width), lambda bi, j: (bi, j, colblk))

    def whole(shape):
        return pl.BlockSpec(shape, lambda bi, j: (0,) * len(shape))

    return pl.pallas_call(
        _merge_kernel,
        out_shape=(jax.ShapeDtypeStruct((b, t, d), F32),
                   jax.ShapeDtypeStruct((b, t, d), F32),
                   jax.ShapeDtypeStruct((b, t, LANES), F32)),
        grid=(b, t // tm),
        in_specs=[
            rows(d), rows(bw), rows(bw), rows(bw),
            rows(d, gate0), rows(d, gate0 + 1), rows(d, gate0 + 2),
            pl.BlockSpec((1, N_MOD, d), lambda bi, j: (bi, 0, 0)),
            whole((N_MOD, d)),
            whole(wb.shape), whole(wo.shape), whole((1, d)), whole(wr.shape), whole((1, LANES)),
        ],
        out_specs=(rows(d), rows(d), rows(LANES)),
        compiler_params=_params(("arbitrary", "arbitrary")),
        name="merge_router",
    )(s, a, bb, cc, z, z, z, mod, modc, wb, wo, gf, wr, br)


MOE_CHUNK = 256


def _moe_kernel(tok_ref, wgt_ref, cnt_ref, off_ref, f_ref, wg_ref, wu_ref, wd_ref, y_ref,
                xs_scr, ys_scr):
    e = pl.program_id(1)
    sub = SUBLANES
    n_col = f_ref.shape[2]

    @pl.when(e == 0)
    def _():
        y_ref[...] = jnp.zeros_like(y_ref)
        xs_scr[...] = jnp.zeros_like(xs_scr)

    cnt = cnt_ref[0, 0, e]
    off = off_ref[0, 0, e]
    wg = wg_ref[0, 0].astype(BF16)
    wu = wu_ref[0, 0].astype(BF16)
    wd = wd_ref[0, 0].astype(BF16)

    def chunk(ci, carry):
        base = off + ci * MOE_CHUNK
        m = jnp.minimum(MOE_CHUNK, cnt - ci * MOE_CHUNK)
        n_grp = (m + sub - 1) // sub

        def gather(gi, c2):
            for u in range(sub):
                r = gi * sub + u
                tok = tok_ref[0, 0, base + r]
                xs_scr[pl.ds(pl.multiple_of(r * n_col, n_col), n_col), :] = f_ref[0, tok]
            return c2

        lax.fori_loop(0, n_grp, gather, 0)
        x = jnp.concatenate(
            [xs_scr[pl.ds(s, MOE_CHUNK, stride=n_col), :] for s in range(n_col)], axis=1).astype(BF16)
        he = (_silu(_dot(x, wg)) * _dot(x, wu)).astype(BF16)
        y = _dot(he, wd)
        for s in range(n_col):
            ys_scr[pl.ds(s, MOE_CHUNK, stride=n_col), :] = y[:, s * LANES:(s + 1) * LANES]

        def scatter(gi, c2):
            for u in range(sub):
                r = gi * sub + u
                tok = tok_ref[0, 0, base + r]
                wv = jnp.where(r < m, wgt_ref[0, 0, base + r], 0.0)
                y_ref[0, tok] = y_ref[0, tok] + wv * ys_scr[pl.ds(pl.multiple_of(r * n_col, n_col), n_col), :]
            return c2

        lax.fori_loop(0, n_grp, scatter, 0)
        return carry

    lax.fori_loop(0, (cnt + MOE_CHUNK - 1) // MOE_CHUNK, chunk, 0)


def _moe(f, route, w_gate, w_up, w_down, layer):
    b, t, d = f.shape
    hid = w_gate.shape[-1]
    n_col = d // LANES
    assert n_col == SUBLANES
    n_slot = TOP_K * t

    eid = route[:, :, 0:TOP_K].astype(jnp.int32).reshape(b, n_slot)
    wts = route[:, :, TOP_K:2 * TOP_K].reshape(b, n_slot)
    perm = jnp.argsort(eid, axis=1).astype(jnp.int32)
    pad = jnp.zeros((b, MOE_CHUNK), jnp.int32)
    tok_sorted = jnp.concatenate([perm // TOP_K, pad], axis=1).reshape(b, 1, n_slot + MOE_CHUNK)
    wgt_sorted = jnp.concatenate([jnp.take_along_axis(wts, perm, axis=1), pad.astype(F32)],
                                 axis=1).reshape(b, 1, n_slot + MOE_CHUNK)
    counts = jnp.sum(eid[:, :, None] == jnp.arange(N_EXPERTS)[None, None, :], axis=1).astype(jnp.int32)
    offs = (jnp.cumsum(counts, axis=1) - counts).astype(jnp.int32)
    counts = counts.reshape(b, 1, N_EXPERTS)
    offs = offs.reshape(b, 1, N_EXPERTS)

    def smem(n):
        return pl.BlockSpec((1, 1, n), lambda bi, e: (bi, 0, 0), memory_space=pltpu.SMEM)

    tile_spec = pl.BlockSpec((1, t, n_col, LANES), lambda bi, e: (bi, 0, 0, 0))
    y4 = pl.pallas_call(
        _moe_kernel,
        out_shape=jax.ShapeDtypeStruct((b, t, n_col, LANES), F32),
        grid=(b, N_EXPERTS),
        in_specs=[
            smem(n_slot + MOE_CHUNK), smem(n_slot + MOE_CHUNK), smem(N_EXPERTS), smem(N_EXPERTS),
            pl.BlockSpec((1, t, n_col, LANES), lambda bi, e: (bi, 0, 0, 0),
                         pipeline_mode=pl.Buffered(1)),
            pl.BlockSpec((1, 1, d, hid), lambda bi, e: (layer, e, 0, 0)),
            pl.BlockSpec((1, 1, d, hid), lambda bi, e: (layer, e, 0, 0)),
            pl.BlockSpec((1, 1, hid, d), lambda bi, e: (layer, e, 0, 0)),
        ],
        out_specs=tile_spec,
        scratch_shapes=[pltpu.VMEM((MOE_CHUNK * n_col, LANES), F32),
                        pltpu.VMEM((MOE_CHUNK * n_col, LANES), F32)],
        compiler_params=_params(("arbitrary", "arbitrary")),
        name="moe_experts",
    )(tok_sorted, wgt_sorted, counts, offs, f.reshape(b, t, n_col, LANES), w_gate, w_up, w_down)
    return y4.reshape(b, t, d)


def _final_kernel(s_ref, y_ref, mod_ref, g_ref, o_ref):
    x = s_ref[0] + mod_ref[0, 5:6, :] * y_ref[0]
    o_ref[0] = _rms(x) * g_ref[...]


def _final_norm(s, y, mod, g, n_ctx):
    b, t, d = s.shape
    tm = ROW_TILE
    skip = n_ctx // tm
    lat_rows = pl.BlockSpec((1, tm, d), lambda bi, j: (bi, j + skip, 0))
    return pl.pallas_call(
        _final_kernel,
        out_shape=jax.ShapeDtypeStruct((b, t - n_ctx, d), F32),
        grid=(b, (t - n_ctx) // tm),
        in_specs=[lat_rows, lat_rows,
                  pl.BlockSpec((1, N_MOD, d), lambda bi, j: (bi, 0, 0)),
                  pl.BlockSpec((1, d), lambda bi, j: (0, 0))],
        out_specs=pl.BlockSpec((1, tm, d), lambda bi, j: (bi, j, 0)),
        compiler_params=_params(("arbitrary", "arbitrary")),
        name="final_norm",
    )(s, y, mod, g)


def kernel(x, c, ctx, c_ctx, w_ada, b_ada, g_mix, g_ffn, w_in, na_rpb, hg_lb_logits, hg_norm_g,
           gqa_qnorm_g, gqa_knorm_g, w_branch, w_out, w_group_router, b_group_router,
           w_expert_router, b_expert_router, w_exp_gate, w_exp_up, w_exp_down, g_final):
    b, n_lat, d = x.shape
    n_ctx = ctx.shape[1]
    depth = w_in.shape[0]
    assert n_ctx == ROW_TILE and n_lat % ROW_TILE == 0 and (n_ctx + n_lat) % WIDE_ROWS == 0
    rows = n_lat // GRID_W

    s = jnp.concatenate([ctx, x], axis=1)

    c_rows = 16
    c_all = jnp.concatenate([c, c_ctx[None, :], jnp.zeros((c_rows - b - 1, d), c.dtype)], axis=0)
    mod_all = _ada(c_all, w_ada, b_ada).reshape(depth, c_rows, N_MOD, d)

    tables = _rope_tables(n_lat)
    rep = LANES // GQA_HEAD_DIM

    prev = None
    for l in range(depth):
        mod = mod_all[l, :b]
        modc = mod_all[l, b]
        w_pad = jnp.concatenate(
            [w_in[l, :, :COL_RAW_GATES],
             jnp.zeros((d, COL_GATES - COL_RAW_GATES), w_in.dtype),
             w_in[l, :, COL_RAW_GATES:]], axis=1).astype(BF16)
        z, zf, s = _inproj(s, g_mix[l].reshape(1, d), modc, mod, w_pad, n_ctx, prev)

        a = _na_attention(z, _na_bias_table(na_rpb[l], rows), n_ctx)
        cc = _gqa_attention(z, tables,
                            jnp.tile(gqa_qnorm_g[l].astype(F32), rep).reshape(1, LANES),
                            jnp.tile(gqa_knorm_g[l].astype(F32), rep).reshape(1, LANES), n_ctx)
        bb = _hgrn(z, zf, hg_lb_logits, hg_norm_g[l], l, n_ctx)

        wr = jnp.concatenate(
            [w_expert_router[l], w_group_router[l],
             jnp.zeros((d, LANES - N_EXPERTS - N_GROUPS), F32)], axis=1)
        br = jnp.concatenate(
            [b_expert_router[l], b_group_router[l],
             jnp.zeros((LANES - N_EXPERTS - N_GROUPS,), F32)]).reshape(1, LANES)
        s, f, route = _merge(s, a, bb, cc, z, mod, modc, w_branch[l].astype(BF16),
                             w_out[l].astype(BF16), g_ffn[l].reshape(1, d), wr, br)
        y = _moe(f, route, w_exp_gate, w_exp_up, w_exp_down, l)
        prev = (y, mod, modc)

    return _final_norm(s, prev[0], prev[1], g_final.reshape(1, d), n_ctx)
```

```python
import functools

import numpy as np
import jax
import jax.numpy as jnp
from jax import lax
from jax.experimental import pallas as pl
from jax.experimental.pallas import tpu as pltpu

F32 = jnp.float32
BF16 = jnp.bfloat16

RMS_EPS = 1e-6
N_MOD = 6
GRID_W = 64

NA_HEADS = 8
NA_HEAD_DIM = 64
NA_WIDTH = NA_HEADS * NA_HEAD_DIM
WIN_ROWS = 8
WIN_COLS = 16
NA_QROWS = 4
NA_KROWS = 12

HG_HEADS = 4
HG_DK = 128
HG_WIDTH = HG_HEADS * HG_DK
HG_CHUNK = 64

GQA_Q_HEADS = 8
GQA_KV_HEADS = 2
GQA_HEAD_DIM = 64
GQA_Q_WIDTH = GQA_Q_HEADS * GQA_HEAD_DIM
GQA_KV_WIDTH = GQA_KV_HEADS * GQA_HEAD_DIM
ROPE_THETA = 10000.0

N_GROUPS = 4
EXPERTS_PER_GROUP = 4
N_EXPERTS = N_GROUPS * EXPERTS_PER_GROUP
TOP_K = 2

LANES = 128
SUBLANES = 8
ROW_TILE = 256
WIDE_ROWS = 768
MASK_NEG = -1e30

COL_NA_Q = 0
COL_NA_K = 512
COL_NA_V = 1024
COL_HG_Q = 1536
COL_HG_FF = 2048
COL_HG_FB = 2560
COL_HG_I = 3072
COL_HG_G = 3584
COL_GQA_Q = 4096
COL_GQA_K = 4608
COL_GQA_V = 4736
COL_RAW_GATES = 4864
COL_GATES = 5120
IN_COLS_PAD = 8192
IN_TILE = 1024

VMEM_LIMIT = 56 * 1024 * 1024


def _dot(a, b):
    return jnp.dot(a, b, preferred_element_type=F32)


def _dot_nt(a, b):
    return lax.dot_general(a, b, (((1,), (1,)), ((), ())), preferred_element_type=F32)


def _dot_tn(a, b):
    return lax.dot_general(a, b, (((0,), (0,)), ((), ())), preferred_element_type=F32)


def _split_bf16(x):
    hi = x.astype(BF16)
    lo = (x - hi.astype(F32)).astype(BF16)
    return hi, lo


def _silu(x):
    return x * jax.nn.sigmoid(x)


def _rms(x):
    return x * lax.rsqrt(jnp.mean(x * x, axis=-1, keepdims=True) + RMS_EPS)


def _tiles_to_rows(ref, tok0, n):
    return jnp.concatenate(
        [ref[0, pl.ds(tok0 * SUBLANES + s, n, stride=SUBLANES), :] for s in range(SUBLANES)], axis=1)


def _rows_to_tiles(ref, tok0, x):
    n = x.shape[0]
    for s in range(SUBLANES):
        ref[0, pl.ds(tok0 * SUBLANES + s, n, stride=SUBLANES), :] = x[:, s * LANES:(s + 1) * LANES]


def _params(semantics, vmem=VMEM_LIMIT):
    return pltpu.CompilerParams(dimension_semantics=semantics, vmem_limit_bytes=vmem)


def _ada_kernel(c_ref, w_ref, b_ref, o_ref):
    sc = _silu(c_ref[...]).astype(BF16)
    o_ref[0] = _dot(sc, w_ref[0].astype(BF16)) + b_ref[0]


def _ada(c_all, w_ada, b_ada):
    depth, d, n = w_ada.shape
    rows = c_all.shape[0]
    tn = 1536
    return pl.pallas_call(
        _ada_kernel,
        out_shape=jax.ShapeDtypeStruct((depth, rows, n), F32),
        grid=(depth, n // tn),
        in_specs=[
            pl.BlockSpec((rows, d), lambda l, j: (0, 0)),
            pl.BlockSpec((1, d, tn), lambda l, j: (l, 0, j)),
            pl.BlockSpec((1, 1, tn), lambda l, j: (l, 0, j)),
        ],
        out_specs=pl.BlockSpec((1, rows, tn), lambda l, j: (l, 0, j)),
        compiler_params=_params(("arbitrary", "arbitrary")),
        name="ada_mod",
    )(c_all, w_ada, b_ada.reshape(depth, 1, n))


def _inproj_kernel(*refs, n_ctx, tiles_per_batch, residual):
    if residual:
        (s_ref, y_ref, modp_ref, modcp_ref, g_ref, modc_ref, mod_ref, w_ref,
         z_ref, zf_ref, s_out, h_scr) = refs
    else:
        s_ref, g_ref, modc_ref, mod_ref, w_ref, z_ref, zf_ref, h_scr = refs
    i = pl.program_id(0)
    j = pl.program_id(1)
    tm = s_ref.shape[1]

    @pl.when(j == 0)
    def _():
        g = g_ref[...]
        first = (i % tiles_per_batch) == 0
        for r in range(tm // ROW_TILE):
            rows = slice(r * ROW_TILE, (r + 1) * ROW_TILE)
            is_ctx = jnp.logical_and(first, r * ROW_TILE < n_ctx)

            def pick(ctx_ref, lat_ref, k):
                return jnp.where(is_ctx, ctx_ref[k:k + 1, :], lat_ref[0, k:k + 1, :])

            x = s_ref[0, rows, :]
            if residual:
                x = x + pick(modcp_ref, modp_ref, 5) * _tiles_to_rows(y_ref, r * ROW_TILE, ROW_TILE)
                s_out[0, rows, :] = x
            h = (_rms(x) * g) * (1.0 + pick(modc_ref, mod_ref, 1)) + pick(modc_ref, mod_ref, 0)
            h_scr[rows, :] = h.astype(BF16)

    acc = _dot(h_scr[...], w_ref[...])
    z_ref[0] = acc.astype(BF16)

    @pl.when(j == COL_HG_FF // IN_TILE)
    def _():
        zf_ref[0] = acc


def _inproj(s, g, modc, mod, w_pad, n_ctx, prev=None):
    b, t, d = s.shape
    n = w_pad.shape[1]
    tm = WIDE_ROWS
    tpb = t // tm
    residual = prev is not None
    assert d == IN_TILE

    row_spec = pl.BlockSpec((1, tm, d), lambda i, j: (i // tpb, i % tpb, 0))
    mod_spec = pl.BlockSpec((1, N_MOD, d), lambda i, j: (i // tpb, 0, 0))
    modc_spec = pl.BlockSpec((N_MOD, d), lambda i, j: (0, 0))
    in_specs = [row_spec]
    args = [s]
    if residual:
        y_spec = pl.BlockSpec((1, tm * SUBLANES, LANES), lambda i, j: (i // tpb, i % tpb, 0))
        in_specs += [y_spec, mod_spec, modc_spec]
        args += [prev[0], prev[1], prev[2]]
    in_specs += [pl.BlockSpec((1, d), lambda i, j: (0, 0)), modc_spec, mod_spec,
                 pl.BlockSpec((d, IN_TILE), lambda i, j: (0, j))]
    args += [g, modc, mod, w_pad]
    out_shape = [jax.ShapeDtypeStruct((b, t, n), BF16),
                 jax.ShapeDtypeStruct((b, t, IN_TILE), F32)]
    out_specs = [
        pl.BlockSpec((1, tm, IN_TILE), lambda i, j: (i // tpb, i % tpb, j)),
        row_spec,
    ]
    if residual:
        out_shape.append(jax.ShapeDtypeStruct((b, t, d), F32))
        out_specs.append(row_spec)
    outs = pl.pallas_call(
        functools.partial(_inproj_kernel, n_ctx=n_ctx, tiles_per_batch=tpb, residual=residual),
        out_shape=tuple(out_shape),
        grid=(b * tpb, n // IN_TILE),
        in_specs=in_specs,
        out_specs=tuple(out_specs),
        scratch_shapes=[pltpu.VMEM((tm, d), BF16)],
        compiler_params=_params(("arbitrary", "arbitrary")),
        name="in_proj",
    )(*args)
    return outs if residual else (outs[0], outs[1], s)


def _na_block_start(blk, rows):
    return jnp.clip(NA_QROWS * blk - WIN_ROWS // 2, 0, rows - NA_KROWS)


def _na_bias_table(rpb, rows):
    n_blk = rows // NA_QROWS
    assert rows % NA_QROWS == 0 and n_blk >= 3 and rows >= NA_KROWS
    qc = np.arange(GRID_W)[:, None]
    kc = np.arange(GRID_W)[None, :]
    c0 = np.clip(qc - WIN_COLS // 2, 0, GRID_W - WIN_COLS)
    col_ok = (kc >= c0) & (kc < c0 + WIN_COLS)
    dcol = np.clip(kc - qc + WIN_COLS - 1, 0, 2 * WIN_COLS - 2)
    j = np.arange(NA_QROWS)[:, None]
    i = np.arange(NA_KROWS)[None, :]
    row_ok, drow = [], []
    for blk in (0, 1, n_blk - 1):
        u0 = int(np.clip(NA_QROWS * blk - WIN_ROWS // 2, 0, rows - NA_KROWS))
        r = NA_QROWS * blk + j
        r0 = np.clip(r - WIN_ROWS // 2, 0, rows - WIN_ROWS)
        krow = u0 + i
        row_ok.append((krow >= r0) & (krow < r0 + WIN_ROWS))
        drow.append(np.clip(krow - r + WIN_ROWS - 1, 0, 2 * WIN_ROWS - 2))
    row_ok = np.stack(row_ok)
    drow = np.stack(drow)
    n_dr, n_dc = 2 * WIN_ROWS - 1, 2 * WIN_COLS - 1
    row_sel = (drow[..., None] == np.arange(n_dr)).astype(np.float32)
    col_sel = (np.arange(n_dc)[:, None, None] == dcol[None]).astype(np.float32)
    by_row = jnp.einsum('pjia,hab->hpjib', row_sel, rpb.astype(F32), precision=lax.Precision.HIGHEST)
    tab = jnp.einsum('hpjib,bqk->hpjqik', by_row, col_sel, precision=lax.Precision.HIGHEST)
    ok = row_ok[None, :, :, None, :, None] & col_ok[None, None, None, :, None, :]
    tab = jnp.where(ok, tab, MASK_NEG)
    h = rpb.shape[0]
    return tab.reshape(h, 3, NA_QROWS * GRID_W, NA_KROWS * GRID_W)


def _na_kernel(q_ref, k_ref, v_ref, bias_ref, o_ref, *, n_ctx, rows):
    i = pl.program_id(1)
    tq = q_ref.shape[1]
    lane = lax.broadcasted_iota(jnp.int32, (tq, LANES), 1)
    lo_half = lane < NA_HEAD_DIM

    def attend(q2, parts):
        outs = []
        for hh in range(2):
            keep = lo_half if hh == 0 else jnp.logical_not(lo_half)
            qm = jnp.where(keep, q2, jnp.zeros_like(q2))
            scores = []
            for kk, _, bias in parts:
                s = _dot_nt(qm, kk)
                if bias is not None:
                    s = s + bias[hh]
                scores.append(s)
            m = functools.reduce(jnp.maximum, [jnp.max(s, axis=-1, keepdims=True) for s in scores])
            den = 0.0
            acc = 0.0
            for s, (_, vv, _) in zip(scores, parts):
                p = jnp.exp(s - m)
                den = den + jnp.sum(p, axis=-1, keepdims=True)
                acc = acc + _dot(p.astype(BF16), vv)
            outs.append(acc / den)
        return jnp.where(lo_half, outs[0], outs[1])

    scale = NA_HEAD_DIM ** -0.5

    @pl.when(i == 0)
    def _():
        for hp in range(NA_HEADS // 2):
            cols = slice(hp * LANES, (hp + 1) * LANES)
            q2 = q_ref[0, :, cols] * scale
            parts = [(k_ref[0, 0:n_ctx, cols], v_ref[0, 0:n_ctx, cols], None)]
            o_ref[0, :, cols] = attend(q2, parts).astype(o_ref.dtype)

    @pl.when(i > 0)
    def _():
        u0 = _na_block_start(i - 1, rows)
        krows = pl.ds(pl.multiple_of(n_ctx + u0 * GRID_W, GRID_W), NA_KROWS * GRID_W)
        for hp in range(NA_HEADS // 2):
            cols = slice(hp * LANES, (hp + 1) * LANES)
            q2 = q_ref[0, :, cols] * scale
            bias = (bias_ref[2 * hp, 0], bias_ref[2 * hp + 1, 0])
            parts = [(k_ref[0, krows, cols], v_ref[0, krows, cols], bias),
                     (k_ref[0, 0:n_ctx, cols], v_ref[0, 0:n_ctx, cols], None)]
            o_ref[0, :, cols] = attend(q2, parts).astype(o_ref.dtype)


def _na_attention(z, bias_tab, n_ctx):
    b, t, _ = z.shape
    rows = (t - n_ctx) // GRID_W
    tq = NA_QROWS * GRID_W
    assert n_ctx == tq
    n_blk = rows // NA_QROWS
    wk = NA_KROWS * GRID_W

    def pattern(i):
        return jnp.where(i <= 1, 0, jnp.where(i == n_blk, 2, 1))

    return pl.pallas_call(
        functools.partial(_na_kernel, n_ctx=n_ctx, rows=rows),
        out_shape=jax.ShapeDtypeStruct((b, t, NA_WIDTH), BF16),
        grid=(b, 1 + n_blk),
        in_specs=[
            pl.BlockSpec((1, tq, NA_WIDTH), lambda bi, i: (bi, i, COL_NA_Q // NA_WIDTH)),
            pl.BlockSpec((1, t, NA_WIDTH), lambda bi, i: (bi, 0, COL_NA_K // NA_WIDTH)),
            pl.BlockSpec((1, t, NA_WIDTH), lambda bi, i: (bi, 0, COL_NA_V // NA_WIDTH)),
            pl.BlockSpec((NA_HEADS, 1, tq, wk), lambda bi, i: (0, pattern(i), 0, 0)),
        ],
        out_specs=pl.BlockSpec((1, tq, NA_WIDTH), lambda bi, i: (bi, i, 0)),
        compiler_params=_params(("arbitrary", "arbitrary")),
        name="na_attention",
    )(z, z, z, bias_tab)


def _rope_tables(n_tokens):
    t = jnp.arange(n_tokens)
    pos = jnp.stack([t // GRID_W, t % GRID_W], axis=-1).astype(F32)
    n_freq = GQA_HEAD_DIM // 4
    inv_freq = jnp.power(ROPE_THETA, -jnp.arange(n_freq, dtype=F32) / n_freq)
    ang = pos[:, :, None] * inv_freq
    ang = jnp.concatenate([ang, ang], axis=-1).reshape(n_tokens, GQA_HEAD_DIM)
    cos, sin = jnp.cos(ang), jnp.sin(ang)
    first = (np.arange(GQA_HEAD_DIM) % (2 * n_freq)) < n_freq
    sin_a = jnp.where(first, -sin, 0.0)
    sin_b = jnp.where(first, 0.0, sin)
    rep = LANES // GQA_HEAD_DIM
    return tuple(jnp.tile(a, (1, rep)) for a in (cos, sin_a, sin_b))


def _gqa_kernel(q_ref, k_ref, v_ref, cos_ref, sa_ref, sb_ref, gq_ref, gk_ref, o_ref,
                kk_scr, vv_scr, *, n_ctx):
    j = pl.program_id(1)
    t = k_ref.shape[1]
    tq = q_ref.shape[1]
    hd = GQA_HEAD_DIM
    quarter = hd // 4

    r_i = lax.broadcasted_iota(jnp.int32, (LANES, LANES), 0)
    c_i = lax.broadcasted_iota(jnp.int32, (LANES, LANES), 1)
    head_ones = jnp.where((r_i // hd) == (c_i // hd), 1.0, 0.0).astype(BF16)
    lo_half = lax.broadcasted_iota(jnp.int32, (tq, LANES), 1) < hd

    def head_rms(x, g):
        hi, lo = _split_bf16(x * x)
        ms = (_dot(hi, head_ones) + _dot(lo, head_ones)) * (1.0 / hd)
        return (x * lax.rsqrt(ms + RMS_EPS)) * g

    def rope(x, rows):
        return (x * cos_ref[rows, :]
                + pltpu.roll(x, LANES - quarter, 1) * sa_ref[rows, :]
                + pltpu.roll(x, quarter, 1) * sb_ref[rows, :])

    @pl.when(j == 0)
    def _():
        zero = jnp.zeros((tq, LANES), F32)
        for r in range(t // tq):
            rows = slice(r * tq, (r + 1) * tq)
            k = head_rms(k_ref[0, rows, :].astype(F32), gk_ref[...])
            if r * tq >= n_ctx:
                k = rope(k, slice(r * tq - n_ctx, (r + 1) * tq - n_ctx))
            v = v_ref[0, rows, :].astype(F32)
            k_sw = pltpu.roll(k, hd, 1)
            v_sw = pltpu.roll(v, hd, 1)
            for scr, a, a_sw in ((kk_scr, k, k_sw), (vv_scr, v, v_sw)):
                scr[0, rows, :] = jnp.where(lo_half, a, zero).astype(BF16)
                scr[1, rows, :] = jnp.where(lo_half, zero, a_sw).astype(BF16)
                scr[2, rows, :] = jnp.where(lo_half, a_sw, zero).astype(BF16)
                scr[3, rows, :] = jnp.where(lo_half, zero, a).astype(BF16)

    scale = hd ** -0.5

    def tile(nk, rope_rows):
        for c in range(GQA_Q_WIDTH // LANES):
            qc = head_rms(q_ref[0, :, c * LANES:(c + 1) * LANES].astype(F32), gq_ref[...])
            if rope_rows is not None:
                qc = rope(qc, rope_rows)
            qc = (qc * scale).astype(BF16)
            grp = (2 * c) // (GQA_Q_HEADS // GQA_KV_HEADS)
            acc = jnp.zeros((tq, LANES), F32)
            for hh in range(2):
                slot = 2 * grp + hh
                s = _dot_nt(qc, kk_scr[slot, 0:nk, :])
                m = jnp.max(s, axis=-1, keepdims=True)
                p = jnp.exp(s - m)
                den = jnp.sum(p, axis=-1, keepdims=True)
                acc = acc + _dot(p.astype(BF16), vv_scr[slot, 0:nk, :]) / den
            o_ref[0, :, c * LANES:(c + 1) * LANES] = acc.astype(o_ref.dtype)

    @pl.when(j == 0)
    def _():
        tile(n_ctx, None)

    @pl.when(j > 0)
    def _():
        tile(t, pl.ds(pl.multiple_of((j - 1) * tq, tq), tq))


def _gqa_attention(z, tables, gq, gk, n_ctx):
    b, t, _ = z.shape
    tq = ROW_TILE
    assert n_ctx == tq
    n_lat = t - n_ctx
    cos, sa, sb = tables
    tab_spec = pl.BlockSpec((n_lat, LANES), lambda bi, j: (0, 0))
    g_spec = pl.BlockSpec((1, LANES), lambda bi, j: (0, 0))
    return pl.pallas_call(
        functools.partial(_gqa_kernel, n_ctx=n_ctx),
        out_shape=jax.ShapeDtypeStruct((b, t, GQA_Q_WIDTH), BF16),
        grid=(b, t // tq),
        in_specs=[
            pl.BlockSpec((1, tq, GQA_Q_WIDTH), lambda bi, j: (bi, j, COL_GQA_Q // GQA_Q_WIDTH)),
            pl.BlockSpec((1, t, GQA_KV_WIDTH), lambda bi, j: (bi, 0, COL_GQA_K // GQA_KV_WIDTH)),
            pl.BlockSpec((1, t, GQA_KV_WIDTH), lambda bi, j: (bi, 0, COL_GQA_V // GQA_KV_WIDTH)),
            tab_spec, tab_spec, tab_spec, g_spec, g_spec,
        ],
        out_specs=pl.BlockSpec((1, tq, GQA_Q_WIDTH), lambda bi, j: (bi, j, 0)),
        scratch_shapes=[pltpu.VMEM((4, t, LANES), BF16), pltpu.VMEM((4, t, LANES), BF16)],
        compiler_params=_params(("arbitrary", "arbitrary")),
        name="gqa_attention",
    )(z, z, z, cos, sa, sb, gq, gk)


HG_LEVELS = 6


def _hgrn_sum_matrices():
    c = HG_CHUNK
    u = np.arange(c)[None, :]
    r = np.arange(c)[:, None]
    fw, bw = [], []
    for lvl in range(HG_LEVELS):
        hs = c >> (lvl + 1)
        blk = (r // (2 * hs)) * (2 * hs)
        upper = (r % (2 * hs)) >= hs
        last_lower = blk + hs - 1
        first_upper = blk + hs
        fw.append(np.where(upper, (u > last_lower) & (u <= r), (u > r) & (u <= last_lower)))
        bw.append(np.where(upper, (u >= first_upper) & (u < r), (u >= r) & (u < first_upper)))
    fw += [u <= r, u > r]
    bw += [u >= r, u < r]
    return (jnp.asarray(np.concatenate(fw), BF16), jnp.asarray(np.concatenate(bw), BF16))


def _hgrn_pair_masks():
    c = HG_CHUNK
    row = np.arange(c)[:, None]
    col = np.arange(c)[None, :]
    fw, bw = [], []
    for lvl in range(HG_LEVELS):
        hs = c >> (lvl + 1)
        same = (row // (2 * hs)) == (col // (2 * hs))
        row_up = (row % (2 * hs)) >= hs
        col_up = (col % (2 * hs)) >= hs
        fw.append(same & row_up & ~col_up)
        bw.append(same & ~row_up & col_up)
    fw.append(row == col)
    bw.append(row == col)

    def in_lane_halves(masks):
        m = np.stack(masks).astype(np.float32)
        z = np.zeros_like(m)
        return jnp.asarray(np.stack([np.concatenate([m, z], axis=-1),
                                     np.concatenate([z, m], axis=-1)], axis=1))

    assert 2 * c == LANES
    return in_lane_halves(fw), in_lane_halves(bw)


def _hgrn_kernel(q_ref, ff_ref, fb_ref, i_ref, g_ref, lbl_ref, ng_ref, wf_ref, wb_ref, mf_ref, mb_ref,
                 o_ref, o_scr, st_scr, *, layer, n_ctx):
    t = q_ref.shape[1]
    c = HG_CHUNK
    dk = HG_DK
    n = t // c
    nc = n_ctx // c
    depth = lbl_ref.shape[0]

    def lower_bound(direction):
        logits = [lbl_ref[d, direction] for d in range(depth)]
        m = functools.reduce(jnp.maximum, logits)
        e = [jnp.exp(x - m) for x in logits]
        tot = functools.reduce(lambda a, b: a + b, e)
        p = [x / tot for x in e]
        cum = functools.reduce(lambda a, b: a + b, p[:layer + 1])
        return cum - p[0]

    def heads_on_rows(x):
        return jnp.concatenate([x[:, h * dk:(h + 1) * dk] for h in range(HG_HEADS)], axis=0)

    def stream(off, f_ref, lb, w_ref, m_ref, exit_row, slot):
        rows = pl.ds(off, c)
        f = lb + (1.0 - lb) * jax.nn.sigmoid(f_ref[0, rows, :])
        kk = 1.0 - f
        g_hi, g_lo = _split_bf16(jnp.log(f))
        w = w_ref[...]
        decay = jnp.exp(_dot(w, g_hi) + _dot(w, g_lo))
        q = _silu(q_ref[0, rows, :].astype(F32))
        v = i_ref[0, rows, :]
        a2 = [0.0] * HG_HEADS
        for lvl in range(HG_LEVELS + 1):
            if lvl < HG_LEVELS:
                y = decay[lvl * c:(lvl + 1) * c, :]
                qs, ks = heads_on_rows((q * y).astype(BF16)), heads_on_rows((kk * y).astype(BF16))
            else:
                qs, ks = heads_on_rows(q.astype(BF16)), heads_on_rows(kk.astype(BF16))
            p = _dot_nt(qs, ks)
            for h in range(HG_HEADS):
                tile = (h * c) // LANES
                slab = p[h * c:(h + 1) * c, tile * LANES:(tile + 1) * LANES]
                a2[h] = a2[h] + m_ref[lvl, h % 2] * slab
        e_cum = decay[HG_LEVELS * c:(HG_LEVELS + 1) * c, :]
        e_rest = decay[(HG_LEVELS + 1) * c:(HG_LEVELS + 2) * c, :]
        st = st_scr[slot]
        inter = _dot_nt(heads_on_rows((q * e_cum).astype(BF16)), st.astype(BF16))
        upd = _dot_tn(v, (kk * e_rest).astype(BF16))
        outs = []
        for h in range(HG_HEADS):
            cols = slice(h * dk, (h + 1) * dk)
            v2 = jnp.concatenate([v[:, cols], v[:, cols]], axis=0)
            outs.append(_dot(a2[h].astype(BF16), v2) + inter[h * c:(h + 1) * c, cols])
            st_scr[slot, cols, :] = (st[h * dk:(h + 1) * dk, :] * e_cum[exit_row:exit_row + 1, cols]
                                     + upd[h * dk:(h + 1) * dk, cols])
        o_scr[rows, :] += jnp.concatenate(outs, axis=1)

    lb_f = lower_bound(0)
    lb_b = lower_bound(1)
    st_scr[...] = jnp.zeros_like(st_scr)
    o_scr[...] = jnp.zeros_like(o_scr)

    def body(k, carry):
        off_f = pl.multiple_of(k * c, c)
        kb = jnp.where(k < nc, nc - 1 - k, n + nc - 1 - k)
        off_b = pl.multiple_of(kb * c, c)
        stream(off_f, ff_ref, lb_f, wf_ref, mf_ref, c - 1, 0)
        stream(off_b, fb_ref, lb_b, wb_ref, mb_ref, 0, 1)
        return carry

    lax.fori_loop(0, n, body, 0)

    ng = ng_ref[...]

    def readout(r, carry):
        rows = pl.ds(pl.multiple_of(r * ROW_TILE, ROW_TILE), ROW_TILE)
        for h in range(HG_HEADS):
            cols = slice(h * dk, (h + 1) * dk)
            y = (_rms(o_scr[rows, cols]) * ng) * _silu(g_ref[0, rows, cols].astype(F32))
            o_ref[0, rows, cols] = y.astype(o_ref.dtype)
        return carry

    lax.fori_loop(0, t // ROW_TILE, readout, 0)


def _hgrn(z, zf, lb_logits, norm_g, layer, n_ctx):
    b, t, _ = z.shape
    depth = lb_logits.shape[0]
    w = HG_WIDTH
    lbl = lb_logits.astype(F32).reshape(depth, 2, 1, w)
    wf, wb = _hgrn_sum_matrices()
    mf, mb = _hgrn_pair_masks()

    def zcol(base):
        return pl.BlockSpec((1, t, w), lambda bi: (bi, 0, base // w))

    def whole(shape):
        return pl.BlockSpec(shape, lambda bi: (0,) * len(shape))

    return pl.pallas_call(
        functools.partial(_hgrn_kernel, layer=layer, n_ctx=n_ctx),
        out_shape=jax.ShapeDtypeStruct((b, t, w), BF16),
        grid=(b,),
        in_specs=[
            zcol(COL_HG_Q),
            pl.BlockSpec((1, t, w), lambda bi: (bi, 0, 0)),
            pl.BlockSpec((1, t, w), lambda bi: (bi, 0, 1)),
            zcol(COL_HG_I),
            zcol(COL_HG_G),
            whole(lbl.shape), whole((1, HG_DK)), whole(wf.shape), whole(wb.shape),
            whole(mf.shape), whole(mb.shape),
        ],
        out_specs=pl.BlockSpec((1, t, w), lambda bi: (bi, 0, 0)),
        scratch_shapes=[pltpu.VMEM((t, w), F32), pltpu.VMEM((2, w, HG_DK), F32)],
        compiler_params=_params(("arbitrary",)),
        name="hgrn2_scan",
    )(z, zf, zf, z, z, lbl, norm_g.reshape(1, HG_DK).astype(F32), wf, wb, mf, mb)


def _merge_kernel(s_ref, a_ref, b_ref, c_ref, ga_ref, gb_ref, gc_ref, mod_ref, modc_ref,
                  wb_ref, wo_ref, gf_ref, wr_ref, br_ref, s_out, f_out, route_out):
    is_ctx = pl.program_id(1) == 0

    def mod_row(k):
        return jnp.where(is_ctx, modc_ref[k:k + 1, :], mod_ref[0, k:k + 1, :])

    y = 0.0
    for idx, (br, gr) in enumerate(((a_ref, ga_ref), (b_ref, gb_ref), (c_ref, gc_ref))):
        y = y + jax.nn.sigmoid(gr[0].astype(F32)) * _dot(br[0], wb_ref[idx])
    x = s_ref[0] + mod_row(2) * _dot(y.astype(BF16), wo_ref[...])
    s_out[0] = x
    f = (_rms(x) * gf_ref[...]) * (1.0 + mod_row(4)) + mod_row(3)
    _rows_to_tiles(f_out, 0, f)

    f_hi, f_lo = _split_bf16(f)
    w_hi, w_lo = _split_bf16(wr_ref[...])
    logits = _dot(f_hi, w_hi) + _dot(f_lo, w_hi) + _dot(f_hi, w_lo) + br_ref[...]

    lane = lax.broadcasted_iota(jnp.int32, logits.shape, 1)
    is_group = jnp.logical_and(lane >= N_EXPERTS, lane < N_EXPERTS + N_GROUPS)

    def first_argmax(x, x_max):
        return jnp.min(jnp.where(x == x_max, lane, LANES), axis=-1, keepdims=True)

    gl = jnp.where(is_group, logits, MASK_NEG)
    g_max = jnp.max(gl, axis=-1, keepdims=True)
    g_idx = first_argmax(gl, g_max) - N_EXPERTS
    p_group = 1.0 / jnp.sum(jnp.exp(gl - g_max), axis=-1, keepdims=True)
    in_group = jnp.logical_and(lane < N_EXPERTS, (lane // EXPERTS_PER_GROUP) == g_idx)
    e1 = jnp.where(in_group, logits, MASK_NEG)
    v1 = jnp.max(e1, axis=-1, keepdims=True)
    i1 = first_argmax(e1, v1)
    e2 = jnp.where(lane == i1, MASK_NEG, e1)
    v2 = jnp.max(e2, axis=-1, keepdims=True)
    i2 = first_argmax(e2, v2)
    r21 = jnp.exp(v2 - v1)
    w1 = 1.0 / (1.0 + r21)
    w2 = r21 * w1
    route = jnp.where(lane == 0, i1.astype(F32), 0.0)
    route = jnp.where(lane == 1, i2.astype(F32), route)
    route = jnp.where(lane == 2, w1 * p_group, route)
    route = jnp.where(lane == 3, w2 * p_group, route)
    route_out[0] = route


def _merge(s, a, bb, cc, z, mod, modc, wb, wo, gf, wr, br):
    b, t, d = s.shape
    tm = ROW_TILE
    bw = a.shape[-1]
    gate0 = COL_GATES // d

    def rows(width, colblk=0):
        return pl.BlockSpec((1, tm, width), lambda bi, j: (bi, j, colblk))

    def whole(shape):
        return pl.BlockSpec(shape, lambda bi, j: (0,) * len(shape))

    return pl.pallas_call(
        _merge_kernel,
        out_shape=(jax.ShapeDtypeStruct((b, t, d), F32),
                   jax.ShapeDtypeStruct((b, t * SUBLANES, LANES), F32),
                   jax.ShapeDtypeStruct((b, t, LANES), F32)),
        grid=(b, t // tm),
        in_specs=[
            rows(d), rows(bw), rows(bw), rows(bw),
            rows(d, gate0), rows(d, gate0 + 1), rows(d, gate0 + 2),
            pl.BlockSpec((1, N_MOD, d), lambda bi, j: (bi, 0, 0)),
            whole((N_MOD, d)),
            whole(wb.shape), whole(wo.shape), whole((1, d)), whole(wr.shape), whole((1, LANES)),
        ],
        out_specs=(rows(d), pl.BlockSpec((1, tm * SUBLANES, LANES), lambda bi, j: (bi, j, 0)),
                   rows(LANES)),
        compiler_params=_params(("arbitrary", "arbitrary")),
        name="merge_router",
    )(s, a, bb, cc, z, z, z, mod, modc, wb, wo, gf, wr, br)


MOE_CHUNK = 384
MOE_PAD_TOKENS = SUBLANES


def _moe_kernel(tok_ref, wgt_ref, cnt_ref, off_ref, f_ref, wg_ref, wu_ref, wd_ref, y_ref,
                xs_scr, ys_scr):
    e = pl.program_id(1)
    sub = SUBLANES
    n_col = f_ref.shape[2]
    t_dummy = f_ref.shape[1]

    @pl.when(e == 0)
    def _():
        y_ref[...] = jnp.zeros_like(y_ref)
        xs_scr[...] = jnp.zeros_like(xs_scr)

    cnt = cnt_ref[0, 0, e]
    off = off_ref[0, 0, e]
    wg = wg_ref[0, 0].astype(BF16)
    wu = wu_ref[0, 0].astype(BF16)
    wd = wd_ref[0, 0].astype(BF16)

    def chunk(ci, carry):
        base = off + ci * MOE_CHUNK
        m = jnp.minimum(MOE_CHUNK, cnt - ci * MOE_CHUNK)
        n_grp = (m + sub - 1) // sub

        def gather(gi, c2):
            for u in range(sub):
                r = gi * sub + u
                tok = tok_ref[0, 0, base + r]
                xs_scr[pl.ds(pl.multiple_of(r * n_col, n_col), n_col), :] = f_ref[0, tok]
            return c2

        lax.fori_loop(0, n_grp, gather, 0)
        x = jnp.concatenate(
            [xs_scr[pl.ds(s, MOE_CHUNK, stride=n_col), :] for s in range(n_col)], axis=1).astype(BF16)
        he = (_silu(_dot(x, wg)) * _dot(x, wu)).astype(BF16)
        y = _dot(he, wd)
        for s in range(n_col):
            ys_scr[pl.ds(s, MOE_CHUNK, stride=n_col), :] = y[:, s * LANES:(s + 1) * LANES]

        def scatter(gi, c2):
            toks, vals = [], []
            for u in range(sub):
                r = gi * sub + u
                tok = jnp.where(r < m, tok_ref[0, 0, base + r], t_dummy)
                contrib = wgt_ref[0, 0, base + r] * ys_scr[pl.ds(pl.multiple_of(r * n_col, n_col), n_col), :]
                toks.append(tok)
                vals.append(y_ref[0, tok] + contrib)
            for tok, val in zip(toks, vals):
                y_ref[0, tok] = val
            return c2

        lax.fori_loop(0, n_grp, scatter, 0)
        return carry

    lax.fori_loop(0, (cnt + MOE_CHUNK - 1) // MOE_CHUNK, chunk, 0)


def _moe(f_tiles, route, w_gate, w_up, w_down, layer):
    b, t, _ = route.shape
    d, hid = w_gate.shape[-2:]
    n_col = d // LANES
    assert n_col == SUBLANES and f_tiles.shape == (b, t * n_col, LANES)
    n_slot = TOP_K * t

    eid = route[:, :, 0:TOP_K].astype(jnp.int32).reshape(b, n_slot)
    wts = route[:, :, TOP_K:2 * TOP_K].reshape(b, n_slot)
    perm = jnp.argsort(eid, axis=1).astype(jnp.int32)
    pad = jnp.zeros((b, MOE_CHUNK), jnp.int32)
    tok_sorted = jnp.concatenate([perm // TOP_K, pad], axis=1).reshape(b, 1, n_slot + MOE_CHUNK)
    wgt_sorted = jnp.concatenate([jnp.take_along_axis(wts, perm, axis=1), pad.astype(F32)],
                                 axis=1).reshape(b, 1, n_slot + MOE_CHUNK)
    counts = jnp.sum(eid[:, :, None] == jnp.arange(N_EXPERTS)[None, None, :], axis=1).astype(jnp.int32)
    offs = (jnp.cumsum(counts, axis=1) - counts).astype(jnp.int32)
    counts = counts.reshape(b, 1, N_EXPERTS)
    offs = offs.reshape(b, 1, N_EXPERTS)

    def smem(n):
        return pl.BlockSpec((1, 1, n), lambda bi, e: (bi, 0, 0), memory_space=pltpu.SMEM)

    t_out = t + MOE_PAD_TOKENS
    y4 = pl.pallas_call(
        _moe_kernel,
        out_shape=jax.ShapeDtypeStruct((b, t_out, n_col, LANES), F32),
        grid=(b, N_EXPERTS),
        in_specs=[
            smem(n_slot + MOE_CHUNK), smem(n_slot + MOE_CHUNK), smem(N_EXPERTS), smem(N_EXPERTS),
            pl.BlockSpec((1, t, n_col, LANES), lambda bi, e: (bi, 0, 0, 0),
                         pipeline_mode=pl.Buffered(1)),
            pl.BlockSpec((1, 1, d, hid), lambda bi, e: (layer, e, 0, 0)),
            pl.BlockSpec((1, 1, d, hid), lambda bi, e: (layer, e, 0, 0)),
            pl.BlockSpec((1, 1, hid, d), lambda bi, e: (layer, e, 0, 0)),
        ],
        out_specs=pl.BlockSpec((1, t_out, n_col, LANES), lambda bi, e: (bi, 0, 0, 0)),
        scratch_shapes=[pltpu.VMEM((MOE_CHUNK * n_col, LANES), F32),
                        pltpu.VMEM((MOE_CHUNK * n_col, LANES), F32)],
        compiler_params=_params(("arbitrary", "arbitrary")),
        name="moe_experts",
    )(tok_sorted, wgt_sorted, counts, offs, f_tiles.reshape(b, t, n_col, LANES), w_gate, w_up, w_down)
    return y4.reshape(b, t_out * n_col, LANES)


def _final_kernel(s_ref, y_ref, mod_ref, g_ref, o_ref):
    x = s_ref[0] + mod_ref[0, 5:6, :] * _tiles_to_rows(y_ref, 0, s_ref.shape[1])
    o_ref[0] = _rms(x) * g_ref[...]


def _final_norm(s, y, mod, g, n_ctx):
    b, t, d = s.shape
    tm = ROW_TILE
    skip = n_ctx // tm
    lat_rows = pl.BlockSpec((1, tm, d), lambda bi, j: (bi, j + skip, 0))
    return pl.pallas_call(
        _final_kernel,
        out_shape=jax.ShapeDtypeStruct((b, t - n_ctx, d), F32),
        grid=(b, (t - n_ctx) // tm),
        in_specs=[lat_rows,
                  pl.BlockSpec((1, tm * SUBLANES, LANES), lambda bi, j: (bi, j + skip, 0)),
                  pl.BlockSpec((1, N_MOD, d), lambda bi, j: (bi, 0, 0)),
                  pl.BlockSpec((1, d), lambda bi, j: (0, 0))],
        out_specs=pl.BlockSpec((1, tm, d), lambda bi, j: (bi, j, 0)),
        compiler_params=_params(("arbitrary", "arbitrary")),
        name="final_norm",
    )(s, y, mod, g)


def kernel(x, c, ctx, c_ctx, w_ada, b_ada, g_mix, g_ffn, w_in, na_rpb, hg_lb_logits, hg_norm_g,
           gqa_qnorm_g, gqa_knorm_g, w_branch, w_out, w_group_router, b_group_router,
           w_expert_router, b_expert_router, w_exp_gate, w_exp_up, w_exp_down, g_final):
    b, n_lat, d = x.shape
    n_ctx = ctx.shape[1]
    depth = w_in.shape[0]
    assert n_ctx == ROW_TILE and n_lat % ROW_TILE == 0 and (n_ctx + n_lat) % WIDE_ROWS == 0
    rows = n_lat // GRID_W

    s = jnp.concatenate([ctx, x], axis=1)

    c_rows = 16
    c_all = jnp.concatenate([c, c_ctx[None, :], jnp.zeros((c_rows - b - 1, d), c.dtype)], axis=0)
    mod_all = _ada(c_all, w_ada, b_ada).reshape(depth, c_rows, N_MOD, d)

    tables = _rope_tables(n_lat)
    rep = LANES // GQA_HEAD_DIM

    prev = None
    for l in range(depth):
        mod = mod_all[l, :b]
        modc = mod_all[l, b]
        w_pad = jnp.concatenate(
            [w_in[l, :, :COL_RAW_GATES],
             jnp.zeros((d, COL_GATES - COL_RAW_GATES), w_in.dtype),
             w_in[l, :, COL_RAW_GATES:]], axis=1).astype(BF16)
        z, zf, s = _inproj(s, g_mix[l].reshape(1, d), modc, mod, w_pad, n_ctx, prev)

        a = _na_attention(z, _na_bias_table(na_rpb[l], rows), n_ctx)
        cc = _gqa_attention(z, tables,
                            jnp.tile(gqa_qnorm_g[l].astype(F32), rep).reshape(1, LANES),
                            jnp.tile(gqa_knorm_g[l].astype(F32), rep).reshape(1, LANES), n_ctx)
        bb = _hgrn(z, zf, hg_lb_logits, hg_norm_g[l], l, n_ctx)

        wr = jnp.concatenate(
            [w_expert_router[l], w_group_router[l],
             jnp.zeros((d, LANES - N_EXPERTS - N_GROUPS), F32)], axis=1)
        br = jnp.concatenate(
            [b_expert_router[l], b_group_router[l],
             jnp.zeros((LANES - N_EXPERTS - N_GROUPS,), F32)]).reshape(1, LANES)
        s, f, route = _merge(s, a, bb, cc, z, mod, modc, w_branch[l].astype(BF16),
                             w_out[l].astype(BF16), g_ffn[l].reshape(1, d), wr, br)
        y = _moe(f, route, w_exp_gate, w_exp_up, w_exp_down, l)
        prev = (y, mod, modc)

    return _final_norm(s, prev[0], prev[1], g_final.reshape(1, d), n_ctx)
```

```python
import functools

import numpy as np
import jax
import jax.numpy as jnp
from jax import lax
from jax.experimental import pallas as pl
from jax.experimental.pallas import tpu as pltpu

F32 = jnp.float32
BF16 = jnp.bfloat16

RMS_EPS = 1e-6
N_MOD = 6
GRID_W = 64

NA_HEADS = 8
NA_HEAD_DIM = 64
NA_WIDTH = NA_HEADS * NA_HEAD_DIM
WIN_ROWS = 8
WIN_COLS = 16
NA_QROWS = 4
NA_KROWS = 12

HG_HEADS = 4
HG_DK = 128
HG_WIDTH = HG_HEADS * HG_DK
HG_CHUNK = 64

GQA_Q_HEADS = 8
GQA_KV_HEADS = 2
GQA_HEAD_DIM = 64
GQA_Q_WIDTH = GQA_Q_HEADS * GQA_HEAD_DIM
GQA_KV_WIDTH = GQA_KV_HEADS * GQA_HEAD_DIM
ROPE_THETA = 10000.0

N_GROUPS = 4
EXPERTS_PER_GROUP = 4
N_EXPERTS = N_GROUPS * EXPERTS_PER_GROUP
TOP_K = 2

LANES = 128
SUBLANES = 8
ROW_TILE = 256
WIDE_ROWS = 768
MASK_NEG = -1e30

COL_NA_Q = 0
COL_NA_K = 512
COL_NA_V = 1024
COL_HG_Q = 1536
COL_HG_FF = 2048
COL_HG_FB = 2560
COL_HG_I = 3072
COL_HG_G = 3584
COL_GQA_Q = 4096
COL_GQA_K = 4608
COL_GQA_V = 4736
COL_RAW_GATES = 4864
COL_GATES = 5120
IN_COLS_PAD = 8192
IN_TILE = 1024

VMEM_LIMIT = 56 * 1024 * 1024


def _dot(a, b):
    return jnp.dot(a, b, preferred_element_type=F32)


def _dot_nt(a, b):
    return lax.dot_general(a, b, (((1,), (1,)), ((), ())), preferred_element_type=F32)


def _dot_tn(a, b):
    return lax.dot_general(a, b, (((0,), (0,)), ((), ())), preferred_element_type=F32)


def _split_bf16(x):
    hi = x.astype(BF16)
    lo = (x - hi.astype(F32)).astype(BF16)
    return hi, lo


def _silu(x):
    return x * jax.nn.sigmoid(x)


def _rms(x):
    return x * lax.rsqrt(jnp.mean(x * x, axis=-1, keepdims=True) + RMS_EPS)


def _tiles_to_rows(ref, tok0, n):
    return jnp.concatenate(
        [ref[0, pl.ds(tok0 * SUBLANES + s, n, stride=SUBLANES), :] for s in range(SUBLANES)], axis=1)


def _rows_to_tiles(ref, tok0, x):
    n = x.shape[0]
    for s in range(SUBLANES):
        ref[0, pl.ds(tok0 * SUBLANES + s, n, stride=SUBLANES), :] = x[:, s * LANES:(s + 1) * LANES]


def _params(semantics, vmem=VMEM_LIMIT):
    return pltpu.CompilerParams(dimension_semantics=semantics, vmem_limit_bytes=vmem)


def _ada_kernel(c_ref, w_ref, b_ref, o_ref):
    sc = _silu(c_ref[...]).astype(BF16)
    o_ref[0] = _dot(sc, w_ref[0].astype(BF16)) + b_ref[0]


def _ada(c_all, w_ada, b_ada):
    depth, d, n = w_ada.shape
    rows = c_all.shape[0]
    tn = 1536
    return pl.pallas_call(
        _ada_kernel,
        out_shape=jax.ShapeDtypeStruct((depth, rows, n), F32),
        grid=(depth, n // tn),
        in_specs=[
            pl.BlockSpec((rows, d), lambda l, j: (0, 0)),
            pl.BlockSpec((1, d, tn), lambda l, j: (l, 0, j)),
            pl.BlockSpec((1, 1, tn), lambda l, j: (l, 0, j)),
        ],
        out_specs=pl.BlockSpec((1, rows, tn), lambda l, j: (l, 0, j)),
        compiler_params=_params(("arbitrary", "arbitrary")),
        name="ada_mod",
    )(c_all, w_ada, b_ada.reshape(depth, 1, n))


def _inproj_kernel(*refs, n_ctx, tiles_per_batch, residual):
    if residual:
        (s_ref, y_ref, modp_ref, modcp_ref, g_ref, modc_ref, mod_ref, w_ref,
         z_ref, zf_ref, s_out, h_scr) = refs
    else:
        s_ref, g_ref, modc_ref, mod_ref, w_ref, z_ref, zf_ref, h_scr = refs
    i = pl.program_id(0)
    j = pl.program_id(1)
    tm = s_ref.shape[1]

    @pl.when(j == 0)
    def _():
        g = g_ref[...]
        first = (i % tiles_per_batch) == 0
        for r in range(tm // ROW_TILE):
            rows = slice(r * ROW_TILE, (r + 1) * ROW_TILE)
            is_ctx = jnp.logical_and(first, r * ROW_TILE < n_ctx)

            def pick(ctx_ref, lat_ref, k):
                return jnp.where(is_ctx, ctx_ref[k:k + 1, :], lat_ref[0, k:k + 1, :])

            x = s_ref[0, rows, :]
            if residual:
                x = x + pick(modcp_ref, modp_ref, 5) * _tiles_to_rows(y_ref, r * ROW_TILE, ROW_TILE)
                s_out[0, rows, :] = x
            h = (_rms(x) * g) * (1.0 + pick(modc_ref, mod_ref, 1)) + pick(modc_ref, mod_ref, 0)
            h_scr[rows, :] = h.astype(BF16)

    acc = _dot(h_scr[...], w_ref[...])
    z_ref[0] = acc.astype(BF16)

    @pl.when(j == COL_HG_FF // IN_TILE)
    def _():
        zf_ref[0] = acc


def _inproj(s, g, modc, mod, w_pad, n_ctx, prev=None):
    b, t, d = s.shape
    n = w_pad.shape[1]
    tm = WIDE_ROWS
    tpb = t // tm
    residual = prev is not None
    assert d == IN_TILE

    row_spec = pl.BlockSpec((1, tm, d), lambda i, j: (i // tpb, i % tpb, 0))
    mod_spec = pl.BlockSpec((1, N_MOD, d), lambda i, j: (i // tpb, 0, 0))
    modc_spec = pl.BlockSpec((N_MOD, d), lambda i, j: (0, 0))
    in_specs = [row_spec]
    args = [s]
    if residual:
        y_spec = pl.BlockSpec((1, tm * SUBLANES, LANES), lambda i, j: (i // tpb, i % tpb, 0))
        in_specs += [y_spec, mod_spec, modc_spec]
        args += [prev[0], prev[1], prev[2]]
    in_specs += [pl.BlockSpec((1, d), lambda i, j: (0, 0)), modc_spec, mod_spec,
                 pl.BlockSpec((d, IN_TILE), lambda i, j: (0, j))]
    args += [g, modc, mod, w_pad]
    out_shape = [jax.ShapeDtypeStruct((b, t, n), BF16),
                 jax.ShapeDtypeStruct((b, t, IN_TILE), F32)]
    out_specs = [
        pl.BlockSpec((1, tm, IN_TILE), lambda i, j: (i // tpb, i % tpb, j)),
        row_spec,
    ]
    if residual:
        out_shape.append(jax.ShapeDtypeStruct((b, t, d), F32))
        out_specs.append(row_spec)
    outs = pl.pallas_call(
        functools.partial(_inproj_kernel, n_ctx=n_ctx, tiles_per_batch=tpb, residual=residual),
        out_shape=tuple(out_shape),
        grid=(b * tpb, n // IN_TILE),
        in_specs=in_specs,
        out_specs=tuple(out_specs),
        scratch_shapes=[pltpu.VMEM((tm, d), BF16)],
        compiler_params=_params(("arbitrary", "arbitrary")),
        name="in_proj",
    )(*args)
    return outs if residual else (outs[0], outs[1], s)


def _na_block_start(blk, rows):
    return jnp.clip(NA_QROWS * blk - WIN_ROWS // 2, 0, rows - NA_KROWS)


def _na_bias_table(rpb, rows):
    n_blk = rows // NA_QROWS
    assert rows % NA_QROWS == 0 and n_blk >= 3 and rows >= NA_KROWS
    qc = np.arange(GRID_W)[:, None]
    kc = np.arange(GRID_W)[None, :]
    c0 = np.clip(qc - WIN_COLS // 2, 0, GRID_W - WIN_COLS)
    col_ok = (kc >= c0) & (kc < c0 + WIN_COLS)
    dcol = np.clip(kc - qc + WIN_COLS - 1, 0, 2 * WIN_COLS - 2)
    j = np.arange(NA_QROWS)[:, None]
    i = np.arange(NA_KROWS)[None, :]
    row_ok, drow = [], []
    for blk in (0, 1, n_blk - 1):
        u0 = int(np.clip(NA_QROWS * blk - WIN_ROWS // 2, 0, rows - NA_KROWS))
        r = NA_QROWS * blk + j
        r0 = np.clip(r - WIN_ROWS // 2, 0, rows - WIN_ROWS)
        krow = u0 + i
        row_ok.append((krow >= r0) & (krow < r0 + WIN_ROWS))
        drow.append(np.clip(krow - r + WIN_ROWS - 1, 0, 2 * WIN_ROWS - 2))
    row_ok = np.stack(row_ok)
    drow = np.stack(drow)
    n_dr, n_dc = 2 * WIN_ROWS - 1, 2 * WIN_COLS - 1
    row_sel = (drow[..., None] == np.arange(n_dr)).astype(np.float32)
    col_sel = (np.arange(n_dc)[:, None, None] == dcol[None]).astype(np.float32)
    by_row = jnp.einsum('pjia,hab->hpjib', row_sel, rpb.astype(F32), precision=lax.Precision.HIGHEST)
    tab = jnp.einsum('hpjib,bqk->hpjqik', by_row, col_sel, precision=lax.Precision.HIGHEST)
    ok = row_ok[None, :, :, None, :, None] & col_ok[None, None, None, :, None, :]
    tab = jnp.where(ok, tab, MASK_NEG)
    h = rpb.shape[0]
    return tab.reshape(h, 3, NA_QROWS * GRID_W, NA_KROWS * GRID_W)


def _na_kernel(q_ref, k_ref, v_ref, bias_ref, o_ref, *, n_ctx, rows):
    i = pl.program_id(1)
    tq = q_ref.shape[1]
    lane = lax.broadcasted_iota(jnp.int32, (tq, LANES), 1)
    lo_half = lane < NA_HEAD_DIM

    def attend(q2, parts):
        outs = []
        for hh in range(2):
            keep = lo_half if hh == 0 else jnp.logical_not(lo_half)
            qm = jnp.where(keep, q2, jnp.zeros_like(q2))
            scores = []
            for kk, _, bias in parts:
                s = _dot_nt(qm, kk)
                if bias is not None:
                    s = s + bias[hh]
                scores.append(s)
            m = functools.reduce(jnp.maximum, [jnp.max(s, axis=-1, keepdims=True) for s in scores])
            den = 0.0
            acc = 0.0
            for s, (_, vv, _) in zip(scores, parts):
                p = jnp.exp(s - m)
                den = den + jnp.sum(p, axis=-1, keepdims=True)
                acc = acc + _dot(p.astype(BF16), vv)
            outs.append(acc / den)
        return jnp.where(lo_half, outs[0], outs[1])

    scale = NA_HEAD_DIM ** -0.5

    @pl.when(i == 0)
    def _():
        for hp in range(NA_HEADS // 2):
            cols = slice(hp * LANES, (hp + 1) * LANES)
            q2 = q_ref[0, :, cols] * scale
            parts = [(k_ref[0, 0:n_ctx, cols], v_ref[0, 0:n_ctx, cols], None)]
            o_ref[0, :, cols] = attend(q2, parts).astype(o_ref.dtype)

    @pl.when(i > 0)
    def _():
        u0 = _na_block_start(i - 1, rows)
        krows = pl.ds(pl.multiple_of(n_ctx + u0 * GRID_W, GRID_W), NA_KROWS * GRID_W)
        for hp in range(NA_HEADS // 2):
            cols = slice(hp * LANES, (hp + 1) * LANES)
            q2 = q_ref[0, :, cols] * scale
            bias = (bias_ref[2 * hp, 0], bias_ref[2 * hp + 1, 0])
            parts = [(k_ref[0, krows, cols], v_ref[0, krows, cols], bias),
                     (k_ref[0, 0:n_ctx, cols], v_ref[0, 0:n_ctx, cols], None)]
            o_ref[0, :, cols] = attend(q2, parts).astype(o_ref.dtype)


def _na_attention(z, bias_tab, n_ctx):
    b, t, _ = z.shape
    rows = (t - n_ctx) // GRID_W
    tq = NA_QROWS * GRID_W
    assert n_ctx == tq
    n_blk = rows // NA_QROWS
    wk = NA_KROWS * GRID_W

    def pattern(i):
        return jnp.where(i <= 1, 0, jnp.where(i == n_blk, 2, 1))

    return pl.pallas_call(
        functools.partial(_na_kernel, n_ctx=n_ctx, rows=rows),
        out_shape=jax.ShapeDtypeStruct((b, t, NA_WIDTH), BF16),
        grid=(b, 1 + n_blk),
        in_specs=[
            pl.BlockSpec((1, tq, NA_WIDTH), lambda bi, i: (bi, i, COL_NA_Q // NA_WIDTH)),
            pl.BlockSpec((1, t, NA_WIDTH), lambda bi, i: (bi, 0, COL_NA_K // NA_WIDTH)),
            pl.BlockSpec((1, t, NA_WIDTH), lambda bi, i: (bi, 0, COL_NA_V // NA_WIDTH)),
            pl.BlockSpec((NA_HEADS, 1, tq, wk), lambda bi, i: (0, pattern(i), 0, 0)),
        ],
        out_specs=pl.BlockSpec((1, tq, NA_WIDTH), lambda bi, i: (bi, i, 0)),
        compiler_params=_params(("arbitrary", "arbitrary")),
        name="na_attention",
    )(z, z, z, bias_tab)


def _rope_tables(n_tokens):
    t = jnp.arange(n_tokens)
    pos = jnp.stack([t // GRID_W, t % GRID_W], axis=-1).astype(F32)
    n_freq = GQA_HEAD_DIM // 4
    inv_freq = jnp.power(ROPE_THETA, -jnp.arange(n_freq, dtype=F32) / n_freq)
    ang = pos[:, :, None] * inv_freq
    ang = jnp.concatenate([ang, ang], axis=-1).reshape(n_tokens, GQA_HEAD_DIM)
    cos, sin = jnp.cos(ang), jnp.sin(ang)
    first = (np.arange(GQA_HEAD_DIM) % (2 * n_freq)) < n_freq
    sin_a = jnp.where(first, -sin, 0.0)
    sin_b = jnp.where(first, 0.0, sin)
    rep = LANES // GQA_HEAD_DIM
    return tuple(jnp.tile(a, (1, rep)) for a in (cos, sin_a, sin_b))


def _gqa_kernel(q_ref, k_ref, v_ref, cos_ref, sa_ref, sb_ref, gq_ref, gk_ref, o_ref,
                kk_scr, vv_scr, *, n_ctx):
    j = pl.program_id(1)
    t = k_ref.shape[1]
    tq = q_ref.shape[1]
    hd = GQA_HEAD_DIM
    quarter = hd // 4

    r_i = lax.broadcasted_iota(jnp.int32, (LANES, LANES), 0)
    c_i = lax.broadcasted_iota(jnp.int32, (LANES, LANES), 1)
    head_ones = jnp.where((r_i // hd) == (c_i // hd), 1.0, 0.0).astype(BF16)
    lane = lax.broadcasted_iota(jnp.int32, (tq, LANES), 1)
    lo_half = lane < hd

    def head_rms(x, g):
        hi, lo = _split_bf16(x * x)
        ms = (_dot(hi, head_ones) + _dot(lo, head_ones)) * (1.0 / hd)
        return (x * lax.rsqrt(ms + RMS_EPS)) * g

    def rope(x, rows):
        return (x * cos_ref[rows, :]
                + pltpu.roll(x, LANES - quarter, 1) * sa_ref[rows, :]
                + pltpu.roll(x, quarter, 1) * sb_ref[rows, :])

    @pl.when(j == 0)
    def _():
        zero = jnp.zeros((tq, LANES), F32)
        for r in range(t // tq):
            rows = slice(r * tq, (r + 1) * tq)
            k = head_rms(k_ref[0, rows, :].astype(F32), gk_ref[...])
            if r * tq >= n_ctx:
                k = rope(k, slice(r * tq - n_ctx, (r + 1) * tq - n_ctx))
            v = v_ref[0, rows, :].astype(F32)
            k_sw = pltpu.roll(k, hd, 1)
            v_sw = pltpu.roll(v, hd, 1)
            v_lo_rest = jnp.where(lane == hd, 1.0, zero)
            v_hi_rest = jnp.where(lane == 0, 1.0, zero)
            for scr, a, a_sw, lo_rest, hi_rest in ((kk_scr, k, k_sw, zero, zero),
                                                   (vv_scr, v, v_sw, v_lo_rest, v_hi_rest)):
                scr[0, rows, :] = jnp.where(lo_half, a, lo_rest).astype(BF16)
                scr[1, rows, :] = jnp.where(lo_half, hi_rest, a_sw).astype(BF16)
                scr[2, rows, :] = jnp.where(lo_half, a_sw, lo_rest).astype(BF16)
                scr[3, rows, :] = jnp.where(lo_half, hi_rest, a).astype(BF16)

    scale = hd ** -0.5

    def tile(nk, rope_rows):
        for c in range(GQA_Q_WIDTH // LANES):
            qc = head_rms(q_ref[0, :, c * LANES:(c + 1) * LANES].astype(F32), gq_ref[...])
            if rope_rows is not None:
                qc = rope(qc, rope_rows)
            qc = (qc * scale).astype(BF16)
            grp = (2 * c) // (GQA_Q_HEADS // GQA_KV_HEADS)
            outs = []
            for hh in range(2):
                slot = 2 * grp + hh
                s = _dot_nt(qc, kk_scr[slot, 0:nk, :])
                m = jnp.max(s, axis=-1, keepdims=True)
                p = jnp.exp((s - m).astype(BF16))
                o = _dot(p, vv_scr[slot, 0:nk, :])
                sum_lane = hd if hh == 0 else 0
                den = jnp.sum(jnp.where(lane == sum_lane, o, 0.0), axis=-1, keepdims=True)
                outs.append(o / den)
            o_ref[0, :, c * LANES:(c + 1) * LANES] = jnp.where(lo_half, outs[0], outs[1]).astype(o_ref.dtype)

    @pl.when(j == 0)
    def _():
        tile(n_ctx, None)

    @pl.when(j > 0)
    def _():
        tile(t, pl.ds(pl.multiple_of((j - 1) * tq, tq), tq))


def _gqa_attention(z, tables, gq, gk, n_ctx):
    b, t, _ = z.shape
    tq = ROW_TILE
    assert n_ctx == tq
    n_lat = t - n_ctx
    cos, sa, sb = tables
    tab_spec = pl.BlockSpec((n_lat, LANES), lambda bi, j: (0, 0))
    g_spec = pl.BlockSpec((1, LANES), lambda bi, j: (0, 0))
    return pl.pallas_call(
        functools.partial(_gqa_kernel, n_ctx=n_ctx),
        out_shape=jax.ShapeDtypeStruct((b, t, GQA_Q_WIDTH), BF16),
        grid=(b, t // tq),
        in_specs=[
            pl.BlockSpec((1, tq, GQA_Q_WIDTH), lambda bi, j: (bi, j, COL_GQA_Q // GQA_Q_WIDTH)),
            pl.BlockSpec((1, t, GQA_KV_WIDTH), lambda bi, j: (bi, 0, COL_GQA_K // GQA_KV_WIDTH)),
            pl.BlockSpec((1, t, GQA_KV_WIDTH), lambda bi, j: (bi, 0, COL_GQA_V // GQA_KV_WIDTH)),
            tab_spec, tab_spec, tab_spec, g_spec, g_spec,
        ],
        out_specs=pl.BlockSpec((1, tq, GQA_Q_WIDTH), lambda bi, j: (bi, j, 0)),
        scratch_shapes=[pltpu.VMEM((4, t, LANES), BF16), pltpu.VMEM((4, t, LANES), BF16)],
        compiler_params=_params(("arbitrary", "arbitrary")),
        name="gqa_attention",
    )(z, z, z, cos, sa, sb, gq, gk)


HG_LEVELS = 6


def _hgrn_sum_matrices():
    c = HG_CHUNK
    u = np.arange(c)[None, :]
    r = np.arange(c)[:, None]
    fw, bw = [], []
    for lvl in range(HG_LEVELS):
        hs = c >> (lvl + 1)
        blk = (r // (2 * hs)) * (2 * hs)
        upper = (r % (2 * hs)) >= hs
        last_lower = blk + hs - 1
        first_upper = blk + hs
        fw.append(np.where(upper, (u > last_lower) & (u <= r), (u > r) & (u <= last_lower)))
        bw.append(np.where(upper, (u >= first_upper) & (u < r), (u >= r) & (u < first_upper)))
    fw += [u <= r, u > r]
    bw += [u >= r, u < r]

    def twice(parts):
        w = np.concatenate(parts).astype(np.float32)
        return jnp.asarray(np.concatenate([w, w], axis=1), BF16)

    return twice(fw), twice(bw)


def _hgrn_pair_masks():
    c = HG_CHUNK
    row = np.arange(c)[:, None]
    col = np.arange(c)[None, :]
    fw, bw = [], []
    for lvl in range(HG_LEVELS):
        hs = c >> (lvl + 1)
        same = (row // (2 * hs)) == (col // (2 * hs))
        row_up = (row % (2 * hs)) >= hs
        col_up = (col % (2 * hs)) >= hs
        fw.append(same & row_up & ~col_up)
        bw.append(same & ~row_up & col_up)
    fw.append(row == col)
    bw.append(row == col)

    def in_lane_halves(masks):
        m = np.stack(masks).astype(np.float32)
        z = np.zeros_like(m)
        return jnp.asarray(np.stack([np.concatenate([m, z], axis=-1),
                                     np.concatenate([z, m], axis=-1)], axis=1))

    assert 2 * c == LANES
    return in_lane_halves(fw), in_lane_halves(bw)


def _hgrn_kernel(q_ref, ff_ref, fb_ref, i_ref, g_ref, lbl_ref, ng_ref, wf_ref, wb_ref, mf_ref, mb_ref,
                 o_ref, o_scr, st_scr, *, layer, n_ctx):
    t = q_ref.shape[1]
    c = HG_CHUNK
    dk = HG_DK
    n = t // c
    nc = n_ctx // c
    depth = lbl_ref.shape[0]

    def lower_bound(direction):
        logits = [lbl_ref[d, direction] for d in range(depth)]
        m = functools.reduce(jnp.maximum, logits)
        e = [jnp.exp(x - m) for x in logits]
        tot = functools.reduce(lambda a, b: a + b, e)
        p = [x / tot for x in e]
        cum = functools.reduce(lambda a, b: a + b, p[:layer + 1])
        return cum - p[0]

    def heads_on_rows(x):
        return jnp.concatenate([x[:, h * dk:(h + 1) * dk] for h in range(HG_HEADS)], axis=0)

    def stream(off, f_ref, lb, w_ref, m_ref, exit_row, slot):
        rows = pl.ds(off, c)
        f = lb + (1.0 - lb) * jax.nn.sigmoid(f_ref[0, rows, :])
        kk = 1.0 - f
        decay = jnp.exp(_dot(w_ref[...], jnp.concatenate(_split_bf16(jnp.log(f)), axis=0)))
        q = _silu(q_ref[0, rows, :].astype(F32))
        v = i_ref[0, rows, :]
        a2 = [0.0] * HG_HEADS
        for lvl in range(HG_LEVELS + 1):
            if lvl < HG_LEVELS:
                y = decay[lvl * c:(lvl + 1) * c, :]
                qs, ks = heads_on_rows((q * y).astype(BF16)), heads_on_rows((kk * y).astype(BF16))
            else:
                qs, ks = heads_on_rows(q.astype(BF16)), heads_on_rows(kk.astype(BF16))
            p = _dot_nt(qs, ks)
            for h in range(HG_HEADS):
                tile = (h * c) // LANES
                slab = p[h * c:(h + 1) * c, tile * LANES:(tile + 1) * LANES]
                a2[h] = a2[h] + m_ref[lvl, h % 2] * slab
        e_cum = decay[HG_LEVELS * c:(HG_LEVELS + 1) * c, :]
        e_rest = decay[(HG_LEVELS + 1) * c:(HG_LEVELS + 2) * c, :]
        st = st_scr[slot]
        inter = _dot_nt(heads_on_rows((q * e_cum).astype(BF16)), st.astype(BF16))
        upd = _dot_tn(v, (kk * e_rest).astype(BF16))
        outs = []
        for h in range(HG_HEADS):
            cols = slice(h * dk, (h + 1) * dk)
            v2 = jnp.concatenate([v[:, cols], v[:, cols]], axis=0)
            outs.append(_dot(a2[h].astype(BF16), v2) + inter[h * c:(h + 1) * c, cols])
            st_scr[slot, cols, :] = (st[h * dk:(h + 1) * dk, :] * e_cum[exit_row:exit_row + 1, cols]
                                     + upd[h * dk:(h + 1) * dk, cols])
        o_scr[rows, :] += jnp.concatenate(outs, axis=1)

    lb_f = lower_bound(0)
    lb_b = lower_bound(1)
    st_scr[...] = jnp.zeros_like(st_scr)
    o_scr[...] = jnp.zeros_like(o_scr)

    def body(k, carry):
        off_f = pl.multiple_of(k * c, c)
        kb = jnp.where(k < nc, nc - 1 - k, n + nc - 1 - k)
        off_b = pl.multiple_of(kb * c, c)
        stream(off_f, ff_ref, lb_f, wf_ref, mf_ref, c - 1, 0)
        stream(off_b, fb_ref, lb_b, wb_ref, mb_ref, 0, 1)
        return carry

    lax.fori_loop(0, n, body, 0)

    ng = ng_ref[...]

    def readout(r, carry):
        rows = pl.ds(pl.multiple_of(r * ROW_TILE, ROW_TILE), ROW_TILE)
        for h in range(HG_HEADS):
            cols = slice(h * dk, (h + 1) * dk)
            y = (_rms(o_scr[rows, cols]) * ng) * _silu(g_ref[0, rows, cols].astype(F32))
            o_ref[0, rows, cols] = y.astype(o_ref.dtype)
        return carry

    lax.fori_loop(0, t // ROW_TILE, readout, 0)


def _hgrn(z, zf, lb_logits, norm_g, layer, n_ctx):
    b, t, _ = z.shape
    depth = lb_logits.shape[0]
    w = HG_WIDTH
    lbl = lb_logits.astype(F32).reshape(depth, 2, 1, w)
    wf, wb = _hgrn_sum_matrices()
    mf, mb = _hgrn_pair_masks()

    def zcol(base):
        return pl.BlockSpec((1, t, w), lambda bi: (bi, 0, base // w))

    def whole(shape):
        return pl.BlockSpec(shape, lambda bi: (0,) * len(shape))

    return pl.pallas_call(
        functools.partial(_hgrn_kernel, layer=layer, n_ctx=n_ctx),
        out_shape=jax.ShapeDtypeStruct((b, t, w), BF16),
        grid=(b,),
        in_specs=[
            zcol(COL_HG_Q),
            pl.BlockSpec((1, t, w), lambda bi: (bi, 0, 0)),
            pl.BlockSpec((1, t, w), lambda bi: (bi, 0, 1)),
            zcol(COL_HG_I),
            zcol(COL_HG_G),
            whole(lbl.shape), whole((1, HG_DK)), whole(wf.shape), whole(wb.shape),
            whole(mf.shape), whole(mb.shape),
        ],
        out_specs=pl.BlockSpec((1, t, w), lambda bi: (bi, 0, 0)),
        scratch_shapes=[pltpu.VMEM((t, w), F32), pltpu.VMEM((2, w, HG_DK), F32)],
        compiler_params=_params(("arbitrary",)),
        name="hgrn2_scan",
    )(z, zf, zf, z, z, lbl, norm_g.reshape(1, HG_DK).astype(F32), wf, wb, mf, mb)


def _merge_kernel(s_ref, a_ref, b_ref, c_ref, ga_ref, gb_ref, gc_ref, mod_ref, modc_ref,
                  wb_ref, wo_ref, gf_ref, wr_ref, br_ref, s_out, f_out, route_out):
    first = pl.program_id(1) == 0
    tm = s_ref.shape[1]

    y = 0.0
    for idx, (br, gr) in enumerate(((a_ref, ga_ref), (b_ref, gb_ref), (c_ref, gc_ref))):
        y = y + jax.nn.sigmoid(gr[0].astype(F32)) * _dot(br[0], wb_ref[idx])
    proj = _dot(y.astype(BF16), wo_ref[...])
    w_hi, w_lo = _split_bf16(wr_ref[...])
    for r in range(tm // ROW_TILE):
        rows = slice(r * ROW_TILE, (r + 1) * ROW_TILE)
        is_ctx = jnp.logical_and(first, r == 0)
        _merge_rows(s_ref, proj[rows, :], rows, r, is_ctx, mod_ref, modc_ref, gf_ref, w_hi, w_lo, br_ref,
                    s_out, f_out, route_out)


def _merge_rows(s_ref, proj, rows, r, is_ctx, mod_ref, modc_ref, gf_ref, w_hi, w_lo, br_ref,
                s_out, f_out, route_out):
    def mod_row(k):
        return jnp.where(is_ctx, modc_ref[k:k + 1, :], mod_ref[0, k:k + 1, :])

    x = s_ref[0, rows, :] + mod_row(2) * proj
    s_out[0, rows, :] = x
    f = (_rms(x) * gf_ref[...]) * (1.0 + mod_row(4)) + mod_row(3)
    _rows_to_tiles(f_out, r * ROW_TILE, f)

    f_hi, f_lo = _split_bf16(f)
    logits = _dot(f_hi, w_hi) + _dot(f_lo, w_hi) + _dot(f_hi, w_lo) + br_ref[...]

    lane = lax.broadcasted_iota(jnp.int32, logits.shape, 1)
    is_group = jnp.logical_and(lane >= N_EXPERTS, lane < N_EXPERTS + N_GROUPS)

    def first_argmax(x, x_max):
        return jnp.min(jnp.where(x == x_max, lane, LANES), axis=-1, keepdims=True)

    gl = jnp.where(is_group, logits, MASK_NEG)
    g_max = jnp.max(gl, axis=-1, keepdims=True)
    g_idx = first_argmax(gl, g_max) - N_EXPERTS
    p_group = 1.0 / jnp.sum(jnp.exp(gl - g_max), axis=-1, keepdims=True)
    in_group = jnp.logical_and(lane < N_EXPERTS, (lane // EXPERTS_PER_GROUP) == g_idx)
    e1 = jnp.where(in_group, logits, MASK_NEG)
    v1 = jnp.max(e1, axis=-1, keepdims=True)
    i1 = first_argmax(e1, v1)
    e2 = jnp.where(lane == i1, MASK_NEG, e1)
    v2 = jnp.max(e2, axis=-1, keepdims=True)
    i2 = first_argmax(e2, v2)
    r21 = jnp.exp(v2 - v1)
    w1 = 1.0 / (1.0 + r21)
    w2 = r21 * w1
    route = jnp.where(lane == 0, i1.astype(F32), 0.0)
    route = jnp.where(lane == 1, i2.astype(F32), route)
    route = jnp.where(lane == 2, w1 * p_group, route)
    route = jnp.where(lane == 3, w2 * p_group, route)
    route_out[0, rows, :] = route


def _merge(s, a, bb, cc, z, mod, modc, wb, wo, gf, wr, br):
    b, t, d = s.shape
    tm = WIDE_ROWS
    bw = a.shape[-1]
    gate0 = COL_GATES // d

    def rows(width, colblk=0):
        return pl.BlockSpec((1, tm, width), lambda bi, j: (bi, j, colblk))

    def whole(shape):
        return pl.BlockSpec(shape, lambda bi, j: (0,) * len(shape))

    return pl.pallas_call(
        _merge_kernel,
        out_shape=(jax.ShapeDtypeStruct((b, t, d), F32),
                   jax.ShapeDtypeStruct((b, t * SUBLANES, LANES), F32),
                   jax.ShapeDtypeStruct((b, t, LANES), F32)),
        grid=(b, t // tm),
        in_specs=[
            rows(d), rows(bw), rows(bw), rows(bw),
            rows(d, gate0), rows(d, gate0 + 1), rows(d, gate0 + 2),
            pl.BlockSpec((1, N_MOD, d), lambda bi, j: (bi, 0, 0)),
            whole((N_MOD, d)),
            whole(wb.shape), whole(wo.shape), whole((1, d)), whole(wr.shape), whole((1, LANES)),
        ],
        out_specs=(rows(d), pl.BlockSpec((1, tm * SUBLANES, LANES), lambda bi, j: (bi, j, 0)),
                   rows(LANES)),
        compiler_params=_params(("arbitrary", "arbitrary")),
        name="merge_router",
    )(s, a, bb, cc, z, z, z, mod, modc, wb, wo, gf, wr, br)


MOE_CHUNK = 384
MOE_PAD_TOKENS = SUBLANES


def _moe_kernel(tok_ref, wgt_ref, cnt_ref, off_ref, f_ref, wg_ref, wu_ref, wd_ref, y_ref,
                xs_scr, ys_scr):
    e = pl.program_id(1)
    sub = SUBLANES
    n_col = f_ref.shape[2]
    t_dummy = f_ref.shape[1]

    @pl.when(e == 0)
    def _():
        y_ref[...] = jnp.zeros_like(y_ref)
        xs_scr[...] = jnp.zeros_like(xs_scr)

    cnt = cnt_ref[0, 0, e]
    off = off_ref[0, 0, e]
    wg = wg_ref[0, 0].astype(BF16)
    wu = wu_ref[0, 0].astype(BF16)
    wd = wd_ref[0, 0].astype(BF16)

    def chunk(ci, carry):
        base = off + ci * MOE_CHUNK
        m = jnp.minimum(MOE_CHUNK, cnt - ci * MOE_CHUNK)
        n_grp = (m + sub - 1) // sub

        def gather(gi, c2):
            for u in range(sub):
                r = gi * sub + u
                tok = tok_ref[0, 0, base + r]
                xs_scr[pl.ds(pl.multiple_of(r * n_col, n_col), n_col), :] = f_ref[0, tok]
            return c2

        lax.fori_loop(0, n_grp, gather, 0)
        x = jnp.concatenate(
            [xs_scr[pl.ds(s, MOE_CHUNK, stride=n_col), :] for s in range(n_col)], axis=1).astype(BF16)
        he = (_silu(_dot(x, wg)) * _dot(x, wu)).astype(BF16)
        y = _dot(he, wd)
        for s in range(n_col):
            ys_scr[pl.ds(s, MOE_CHUNK, stride=n_col), :] = y[:, s * LANES:(s + 1) * LANES]

        def scatter(gi, c2):
            toks, vals = [], []
            for u in range(sub):
                r = gi * sub + u
                tok = jnp.where(r < m, tok_ref[0, 0, base + r], t_dummy)
                contrib = wgt_ref[0, 0, base + r] * ys_scr[pl.ds(pl.multiple_of(r * n_col, n_col), n_col), :]
                toks.append(tok)
                vals.append(y_ref[0, tok] + contrib)
            for tok, val in zip(toks, vals):
                y_ref[0, tok] = val
            return c2

        lax.fori_loop(0, n_grp, scatter, 0)
        return carry

    lax.fori_loop(0, (cnt + MOE_CHUNK - 1) // MOE_CHUNK, chunk, 0)


def _moe(f_tiles, route, w_gate, w_up, w_down, layer):
    b, t, _ = route.shape
    d, hid = w_gate.shape[-2:]
    n_col = d // LANES
    assert n_col == SUBLANES and f_tiles.shape == (b, t * n_col, LANES)
    n_slot = TOP_K * t

    eid = route[:, :, 0:TOP_K].astype(jnp.int32).reshape(b, n_slot)
    wts = route[:, :, TOP_K:2 * TOP_K].reshape(b, n_slot)
    perm = jnp.argsort(eid, axis=1).astype(jnp.int32)
    pad = jnp.zeros((b, MOE_CHUNK), jnp.int32)
    tok_sorted = jnp.concatenate([perm // TOP_K, pad], axis=1).reshape(b, 1, n_slot + MOE_CHUNK)
    wgt_sorted = jnp.concatenate([jnp.take_along_axis(wts, perm, axis=1), pad.astype(F32)],
                                 axis=1).reshape(b, 1, n_slot + MOE_CHUNK)
    counts = jnp.sum(eid[:, :, None] == jnp.arange(N_EXPERTS)[None, None, :], axis=1).astype(jnp.int32)
    offs = (jnp.cumsum(counts, axis=1) - counts).astype(jnp.int32)
    counts = counts.reshape(b, 1, N_EXPERTS)
    offs = offs.reshape(b, 1, N_EXPERTS)

    def smem(n):
        return pl.BlockSpec((1, 1, n), lambda bi, e: (bi, 0, 0), memory_space=pltpu.SMEM)

    t_out = t + MOE_PAD_TOKENS
    y4 = pl.pallas_call(
        _moe_kernel,
        out_shape=jax.ShapeDtypeStruct((b, t_out, n_col, LANES), F32),
        grid=(b, N_EXPERTS),
        in_specs=[
            smem(n_slot + MOE_CHUNK), smem(n_slot + MOE_CHUNK), smem(N_EXPERTS), smem(N_EXPERTS),
            pl.BlockSpec((1, t, n_col, LANES), lambda bi, e: (bi, 0, 0, 0),
                         pipeline_mode=pl.Buffered(1)),
            pl.BlockSpec((1, 1, d, hid), lambda bi, e: (layer, e, 0, 0)),
            pl.BlockSpec((1, 1, d, hid), lambda bi, e: (layer, e, 0, 0)),
            pl.BlockSpec((1, 1, hid, d), lambda bi, e: (layer, e, 0, 0)),
        ],
        out_specs=pl.BlockSpec((1, t_out, n_col, LANES), lambda bi, e: (bi, 0, 0, 0)),
        scratch_shapes=[pltpu.VMEM((MOE_CHUNK * n_col, LANES), F32),
                        pltpu.VMEM((MOE_CHUNK * n_col, LANES), F32)],
        compiler_params=_params(("arbitrary", "arbitrary")),
        name="moe_experts",
    )(tok_sorted, wgt_sorted, counts, offs, f_tiles.reshape(b, t, n_col, LANES), w_gate, w_up, w_down)
    return y4.reshape(b, t_out * n_col, LANES)


def _final_kernel(s_ref, y_ref, mod_ref, g_ref, o_ref):
    x = s_ref[0] + mod_ref[0, 5:6, :] * _tiles_to_rows(y_ref, 0, s_ref.shape[1])
    o_ref[0] = _rms(x) * g_ref[...]


def _final_norm(s, y, mod, g, n_ctx):
    b, t, d = s.shape
    tm = ROW_TILE
    skip = n_ctx // tm
    lat_rows = pl.BlockSpec((1, tm, d), lambda bi, j: (bi, j + skip, 0))
    return pl.pallas_call(
        _final_kernel,
        out_shape=jax.ShapeDtypeStruct((b, t - n_ctx, d), F32),
        grid=(b, (t - n_ctx) // tm),
        in_specs=[lat_rows,
                  pl.BlockSpec((1, tm * SUBLANES, LANES), lambda bi, j: (bi, j + skip, 0)),
                  pl.BlockSpec((1, N_MOD, d), lambda bi, j: (bi, 0, 0)),
                  pl.BlockSpec((1, d), lambda bi, j: (0, 0))],
        out_specs=pl.BlockSpec((1, tm, d), lambda bi, j: (bi, j, 0)),
        compiler_params=_params(("arbitrary", "arbitrary")),
        name="final_norm",
    )(s, y, mod, g)


def kernel(x, c, ctx, c_ctx, w_ada, b_ada, g_mix, g_ffn, w_in, na_rpb, hg_lb_logits, hg_norm_g,
           gqa_qnorm_g, gqa_knorm_g, w_branch, w_out, w_group_router, b_group_router,
           w_expert_router, b_expert_router, w_exp_gate, w_exp_up, w_exp_down, g_final):
    b, n_lat, d = x.shape
    n_ctx = ctx.shape[1]
    depth = w_in.shape[0]
    assert n_ctx == ROW_TILE and n_lat % ROW_TILE == 0 and (n_ctx + n_lat) % WIDE_ROWS == 0
    rows = n_lat // GRID_W

    s = jnp.concatenate([ctx, x], axis=1)

    c_rows = 16
    c_all = jnp.concatenate([c, c_ctx[None, :], jnp.zeros((c_rows - b - 1, d), c.dtype)], axis=0)
    mod_all = _ada(c_all, w_ada, b_ada).reshape(depth, c_rows, N_MOD, d)

    tables = _rope_tables(n_lat)
    rep = LANES // GQA_HEAD_DIM

    prev = None
    for l in range(depth):
        mod = mod_all[l, :b]
        modc = mod_all[l, b]
        w_pad = jnp.concatenate(
            [w_in[l, :, :COL_RAW_GATES],
             jnp.zeros((d, COL_GATES - COL_RAW_GATES), w_in.dtype),
             w_in[l, :, COL_RAW_GATES:]], axis=1).astype(BF16)
        z, zf, s = _inproj(s, g_mix[l].reshape(1, d), modc, mod, w_pad, n_ctx, prev)

        a = _na_attention(z, _na_bias_table(na_rpb[l], rows), n_ctx)
        cc = _gqa_attention(z, tables,
                            jnp.tile(gqa_qnorm_g[l].astype(F32), rep).reshape(1, LANES),
                            jnp.tile(gqa_knorm_g[l].astype(F32), rep).reshape(1, LANES), n_ctx)
        bb = _hgrn(z, zf, hg_lb_logits, hg_norm_g[l], l, n_ctx)

        wr = jnp.concatenate(
            [w_expert_router[l], w_group_router[l],
             jnp.zeros((d, LANES - N_EXPERTS - N_GROUPS), F32)], axis=1)
        br = jnp.concatenate(
            [b_expert_router[l], b_group_router[l],
             jnp.zeros((LANES - N_EXPERTS - N_GROUPS,), F32)]).reshape(1, LANES)
        s, f, route = _merge(s, a, bb, cc, z, mod, modc, w_branch[l].astype(BF16),
                             w_out[l].astype(BF16), g_ffn[l].reshape(1, d), wr, br)
        y = _moe(f, route, w_exp_gate, w_exp_up, w_exp_down, l)
        prev = (y, mod, modc)

    return _final_norm(s, prev[0], prev[1], g_final.reshape(1, d), n_ctx)
```

```python
import functools

import numpy as np
import jax
import jax.numpy as jnp
from jax import lax
from jax.experimental import pallas as pl
from jax.experimental.pallas import tpu as pltpu

F32 = jnp.float32
BF16 = jnp.bfloat16

RMS_EPS = 1e-6
N_MOD = 6
GRID_W = 64

NA_HEADS = 8
NA_HEAD_DIM = 64
NA_WIDTH = NA_HEADS * NA_HEAD_DIM
WIN_ROWS = 8
WIN_COLS = 16
NA_QROWS = 4
NA_KROWS = 12

HG_HEADS = 4
HG_DK = 128
HG_WIDTH = HG_HEADS * HG_DK
HG_CHUNK = 64

GQA_Q_HEADS = 8
GQA_KV_HEADS = 2
GQA_HEAD_DIM = 64
GQA_Q_WIDTH = GQA_Q_HEADS * GQA_HEAD_DIM
GQA_KV_WIDTH = GQA_KV_HEADS * GQA_HEAD_DIM
ROPE_THETA = 10000.0

N_GROUPS = 4
EXPERTS_PER_GROUP = 4
N_EXPERTS = N_GROUPS * EXPERTS_PER_GROUP
TOP_K = 2

LANES = 128
SUBLANES = 8
ROW_TILE = 256
WIDE_ROWS = 768
MASK_NEG = -1e30

COL_NA_Q = 0
COL_NA_K = 512
COL_NA_V = 1024
COL_HG_Q = 1536
COL_HG_FF = 2048
COL_HG_FB = 2560
COL_HG_I = 3072
COL_HG_G = 3584
COL_GQA_Q = 4096
COL_GQA_K = 4608
COL_GQA_V = 4736
COL_RAW_GATES = 4864
COL_GATES = 5120
IN_COLS_PAD = 8192
IN_TILE = 1024

VMEM_LIMIT = 56 * 1024 * 1024


def _dot(a, b):
    return jnp.dot(a, b, preferred_element_type=F32)


def _dot_nt(a, b):
    return lax.dot_general(a, b, (((1,), (1,)), ((), ())), preferred_element_type=F32)


def _dot_tn(a, b):
    return lax.dot_general(a, b, (((0,), (0,)), ((), ())), preferred_element_type=F32)


def _split_bf16(x):
    hi = x.astype(BF16)
    lo = (x - hi.astype(F32)).astype(BF16)
    return hi, lo


def _silu(x):
    return x * jax.nn.sigmoid(x)


def _rms(x):
    return x * lax.rsqrt(jnp.mean(x * x, axis=-1, keepdims=True) + RMS_EPS)


def _tiles_to_rows(ref, tok0, n):
    return jnp.concatenate(
        [ref[0, pl.ds(tok0 * SUBLANES + s, n, stride=SUBLANES), :] for s in range(SUBLANES)], axis=1)


def _rows_to_tiles(ref, tok0, x):
    n = x.shape[0]
    for s in range(SUBLANES):
        ref[0, pl.ds(tok0 * SUBLANES + s, n, stride=SUBLANES), :] = x[:, s * LANES:(s + 1) * LANES]


def _params(semantics, vmem=VMEM_LIMIT):
    return pltpu.CompilerParams(dimension_semantics=semantics, vmem_limit_bytes=vmem)


def _ada_kernel(c_ref, w_ref, b_ref, o_ref):
    sc = _silu(c_ref[...]).astype(BF16)
    o_ref[0] = _dot(sc, w_ref[0].astype(BF16)) + b_ref[0]


def _ada(c_all, w_ada, b_ada):
    depth, d, n = w_ada.shape
    rows = c_all.shape[0]
    tn = 1536
    return pl.pallas_call(
        _ada_kernel,
        out_shape=jax.ShapeDtypeStruct((depth, rows, n), F32),
        grid=(depth, n // tn),
        in_specs=[
            pl.BlockSpec((rows, d), lambda l, j: (0, 0)),
            pl.BlockSpec((1, d, tn), lambda l, j: (l, 0, j)),
            pl.BlockSpec((1, 1, tn), lambda l, j: (l, 0, j)),
        ],
        out_specs=pl.BlockSpec((1, rows, tn), lambda l, j: (l, 0, j)),
        compiler_params=_params(("arbitrary", "arbitrary")),
        name="ada_mod",
    )(c_all, w_ada, b_ada.reshape(depth, 1, n))


def _inproj_kernel(*refs, n_ctx, tiles_per_batch, residual):
    if residual:
        (s_ref, y_ref, modp_ref, modcp_ref, g_ref, modc_ref, mod_ref, w_ref,
         z_ref, zf_ref, s_out, h_scr) = refs
    else:
        s_ref, g_ref, modc_ref, mod_ref, w_ref, z_ref, zf_ref, h_scr = refs
    i = pl.program_id(0)
    j = pl.program_id(1)
    tm = s_ref.shape[1]

    @pl.when(j == 0)
    def _():
        g = g_ref[...]
        first = (i % tiles_per_batch) == 0
        for r in range(tm // ROW_TILE):
            rows = slice(r * ROW_TILE, (r + 1) * ROW_TILE)
            is_ctx = jnp.logical_and(first, r * ROW_TILE < n_ctx)

            def pick(ctx_ref, lat_ref, k):
                return jnp.where(is_ctx, ctx_ref[k:k + 1, :], lat_ref[0, k:k + 1, :])

            x = s_ref[0, rows, :]
            if residual:
                x = x + pick(modcp_ref, modp_ref, 5) * _tiles_to_rows(y_ref, r * ROW_TILE, ROW_TILE)
                s_out[0, rows, :] = x
            h = (_rms(x) * g) * (1.0 + pick(modc_ref, mod_ref, 1)) + pick(modc_ref, mod_ref, 0)
            h_scr[rows, :] = h.astype(BF16)

    acc = _dot(h_scr[...], w_ref[...])
    z_ref[0] = acc.astype(BF16)

    @pl.when(j == COL_HG_FF // IN_TILE)
    def _():
        zf_ref[0] = acc


def _inproj(s, g, modc, mod, w_pad, n_ctx, prev=None):
    b, t, d = s.shape
    n = w_pad.shape[1]
    tm = WIDE_ROWS
    tpb = t // tm
    residual = prev is not None
    assert d == IN_TILE

    row_spec = pl.BlockSpec((1, tm, d), lambda i, j: (i // tpb, i % tpb, 0))
    mod_spec = pl.BlockSpec((1, N_MOD, d), lambda i, j: (i // tpb, 0, 0))
    modc_spec = pl.BlockSpec((N_MOD, d), lambda i, j: (0, 0))
    in_specs = [row_spec]
    args = [s]
    if residual:
        y_spec = pl.BlockSpec((1, tm * SUBLANES, LANES), lambda i, j: (i // tpb, i % tpb, 0))
        in_specs += [y_spec, mod_spec, modc_spec]
        args += [prev[0], prev[1], prev[2]]
    in_specs += [pl.BlockSpec((1, d), lambda i, j: (0, 0)), modc_spec, mod_spec,
                 pl.BlockSpec((d, IN_TILE), lambda i, j: (0, j))]
    args += [g, modc, mod, w_pad]
    out_shape = [jax.ShapeDtypeStruct((b, t, n), BF16),
                 jax.ShapeDtypeStruct((b, t, IN_TILE), F32)]
    out_specs = [
        pl.BlockSpec((1, tm, IN_TILE), lambda i, j: (i // tpb, i % tpb, j)),
        row_spec,
    ]
    if residual:
        out_shape.append(jax.ShapeDtypeStruct((b, t, d), F32))
        out_specs.append(row_spec)
    outs = pl.pallas_call(
        functools.partial(_inproj_kernel, n_ctx=n_ctx, tiles_per_batch=tpb, residual=residual),
        out_shape=tuple(out_shape),
        grid=(b * tpb, n // IN_TILE),
        in_specs=in_specs,
        out_specs=tuple(out_specs),
        scratch_shapes=[pltpu.VMEM((tm, d), BF16)],
        compiler_params=_params(("arbitrary", "arbitrary")),
        name="in_proj",
    )(*args)
    return outs if residual else (outs[0], outs[1], s)


def _na_block_start(blk, rows):
    return jnp.clip(NA_QROWS * blk - WIN_ROWS // 2, 0, rows - NA_KROWS)


def _na_bias_table(rpb, rows):
    n_blk = rows // NA_QROWS
    assert rows % NA_QROWS == 0 and n_blk >= 3 and rows >= NA_KROWS
    qc = np.arange(GRID_W)[:, None]
    kc = np.arange(GRID_W)[None, :]
    c0 = np.clip(qc - WIN_COLS // 2, 0, GRID_W - WIN_COLS)
    col_ok = (kc >= c0) & (kc < c0 + WIN_COLS)
    dcol = np.clip(kc - qc + WIN_COLS - 1, 0, 2 * WIN_COLS - 2)
    j = np.arange(NA_QROWS)[:, None]
    i = np.arange(NA_KROWS)[None, :]
    row_ok, drow = [], []
    for blk in (0, 1, n_blk - 1):
        u0 = int(np.clip(NA_QROWS * blk - WIN_ROWS // 2, 0, rows - NA_KROWS))
        r = NA_QROWS * blk + j
        r0 = np.clip(r - WIN_ROWS // 2, 0, rows - WIN_ROWS)
        krow = u0 + i
        row_ok.append((krow >= r0) & (krow < r0 + WIN_ROWS))
        drow.append(np.clip(krow - r + WIN_ROWS - 1, 0, 2 * WIN_ROWS - 2))
    row_ok = np.stack(row_ok)
    drow = np.stack(drow)
    n_dr, n_dc = 2 * WIN_ROWS - 1, 2 * WIN_COLS - 1
    row_sel = (drow[..., None] == np.arange(n_dr)).astype(np.float32)
    col_sel = (np.arange(n_dc)[:, None, None] == dcol[None]).astype(np.float32)
    by_row = jnp.einsum('pjia,hab->hpjib', row_sel, rpb.astype(F32), precision=lax.Precision.HIGHEST)
    tab = jnp.einsum('hpjib,bqk->hpjqik', by_row, col_sel, precision=lax.Precision.HIGHEST)
    ok = row_ok[None, :, :, None, :, None] & col_ok[None, None, None, :, None, :]
    tab = jnp.where(ok, tab, MASK_NEG)
    h = rpb.shape[0]
    return tab.reshape(h, 3, NA_QROWS * GRID_W, NA_KROWS * GRID_W)


def _na_kernel(q_ref, k_ref, v_ref, bias_ref, o_ref, *, n_ctx, rows):
    i = pl.program_id(1)
    tq = q_ref.shape[1]
    lane = lax.broadcasted_iota(jnp.int32, (tq, LANES), 1)
    lo_half = lane < NA_HEAD_DIM

    scale = NA_HEAD_DIM ** -0.5

    def pair_scores(hp, key_rows, with_bias):
        cols = slice(hp * LANES, (hp + 1) * LANES)
        q2 = q_ref[0, :, cols] * scale
        out = []
        for hh in range(2):
            keep = lo_half if hh == 0 else jnp.logical_not(lo_half)
            qm = jnp.where(keep, q2, jnp.zeros_like(q2))
            blocks = []
            for n, kr in enumerate(key_rows):
                s = _dot_nt(qm, k_ref[0, kr, cols])
                if with_bias and n == 0:
                    s = s + bias_ref[2 * hp + hh, 0]
                blocks.append(s)
            out.append(blocks)
        return out

    def pair_finish(hp, scores, key_rows):
        cols = slice(hp * LANES, (hp + 1) * LANES)
        outs = []
        for blocks in scores:
            m = functools.reduce(jnp.maximum, [jnp.max(s, axis=-1, keepdims=True) for s in blocks])
            den = 0.0
            acc = 0.0
            for s, kr in zip(blocks, key_rows):
                p = jnp.exp(s - m)
                den = den + jnp.sum(p, axis=-1, keepdims=True)
                acc = acc + _dot(p.astype(BF16), v_ref[0, kr, cols])
            outs.append(acc / den)
        o_ref[0, :, cols] = jnp.where(lo_half, outs[0], outs[1]).astype(o_ref.dtype)

    def attend(key_rows, with_bias):
        n_pairs = NA_HEADS // 2
        nxt = pair_scores(0, key_rows, with_bias)
        for hp in range(n_pairs):
            cur = nxt
            if hp + 1 < n_pairs:
                nxt = pair_scores(hp + 1, key_rows, with_bias)
            pair_finish(hp, cur, key_rows)

    ctx_rows = slice(0, n_ctx)

    @pl.when(i == 0)
    def _():
        attend([ctx_rows], False)

    @pl.when(i > 0)
    def _():
        u0 = _na_block_start(i - 1, rows)
        local_rows = pl.ds(pl.multiple_of(n_ctx + u0 * GRID_W, GRID_W), NA_KROWS * GRID_W)
        attend([local_rows, ctx_rows], True)


def _na_attention(z, bias_tab, n_ctx):
    b, t, _ = z.shape
    rows = (t - n_ctx) // GRID_W
    tq = NA_QROWS * GRID_W
    assert n_ctx == tq
    n_blk = rows // NA_QROWS
    wk = NA_KROWS * GRID_W

    def pattern(i):
        return jnp.where(i <= 1, 0, jnp.where(i == n_blk, 2, 1))

    return pl.pallas_call(
        functools.partial(_na_kernel, n_ctx=n_ctx, rows=rows),
        out_shape=jax.ShapeDtypeStruct((b, t, NA_WIDTH), BF16),
        grid=(b, 1 + n_blk),
        in_specs=[
            pl.BlockSpec((1, tq, NA_WIDTH), lambda bi, i: (bi, i, COL_NA_Q // NA_WIDTH)),
            pl.BlockSpec((1, t, NA_WIDTH), lambda bi, i: (bi, 0, COL_NA_K // NA_WIDTH)),
            pl.BlockSpec((1, t, NA_WIDTH), lambda bi, i: (bi, 0, COL_NA_V // NA_WIDTH)),
            pl.BlockSpec((NA_HEADS, 1, tq, wk), lambda bi, i: (0, pattern(i), 0, 0)),
        ],
        out_specs=pl.BlockSpec((1, tq, NA_WIDTH), lambda bi, i: (bi, i, 0)),
        compiler_params=_params(("arbitrary", "arbitrary")),
        name="na_attention",
    )(z, z, z, bias_tab)


def _rope_tables(n_tokens):
    t = jnp.arange(n_tokens)
    pos = jnp.stack([t // GRID_W, t % GRID_W], axis=-1).astype(F32)
    n_freq = GQA_HEAD_DIM // 4
    inv_freq = jnp.power(ROPE_THETA, -jnp.arange(n_freq, dtype=F32) / n_freq)
    ang = pos[:, :, None] * inv_freq
    ang = jnp.concatenate([ang, ang], axis=-1).reshape(n_tokens, GQA_HEAD_DIM)
    cos, sin = jnp.cos(ang), jnp.sin(ang)
    first = (np.arange(GQA_HEAD_DIM) % (2 * n_freq)) < n_freq
    sin_a = jnp.where(first, -sin, 0.0)
    sin_b = jnp.where(first, 0.0, sin)
    rep = LANES // GQA_HEAD_DIM
    return tuple(jnp.tile(a, (1, rep)) for a in (cos, sin_a, sin_b))


def _gqa_kernel(q_ref, k_ref, v_ref, cos_ref, sa_ref, sb_ref, gq_ref, gk_ref, o_ref,
                kk_scr, vv_scr, *, n_ctx):
    j = pl.program_id(1)
    t = k_ref.shape[1]
    tq = q_ref.shape[1]
    hd = GQA_HEAD_DIM
    quarter = hd // 4

    r_i = lax.broadcasted_iota(jnp.int32, (LANES, LANES), 0)
    c_i = lax.broadcasted_iota(jnp.int32, (LANES, LANES), 1)
    head_ones = jnp.where((r_i // hd) == (c_i // hd), 1.0, 0.0).astype(BF16)
    lane = lax.broadcasted_iota(jnp.int32, (tq, LANES), 1)
    lo_half = lane < hd

    def head_rms(x, g):
        hi, lo = _split_bf16(x * x)
        ms = (_dot(hi, head_ones) + _dot(lo, head_ones)) * (1.0 / hd)
        return (x * lax.rsqrt(ms + RMS_EPS)) * g

    def rope(x, rows):
        return (x * cos_ref[rows, :]
                + pltpu.roll(x, LANES - quarter, 1) * sa_ref[rows, :]
                + pltpu.roll(x, quarter, 1) * sb_ref[rows, :])

    @pl.when(j == 0)
    def _():
        zero = jnp.zeros((tq, LANES), F32)
        for r in range(t // tq):
            rows = slice(r * tq, (r + 1) * tq)
            k = head_rms(k_ref[0, rows, :].astype(F32), gk_ref[...])
            if r * tq >= n_ctx:
                k = rope(k, slice(r * tq - n_ctx, (r + 1) * tq - n_ctx))
            v = v_ref[0, rows, :].astype(F32)
            k_sw = pltpu.roll(k, hd, 1)
            v_sw = pltpu.roll(v, hd, 1)
            v_lo_rest = jnp.where(lane == hd, 1.0, zero)
            v_hi_rest = jnp.where(lane == 0, 1.0, zero)
            for scr, a, a_sw, lo_rest, hi_rest in ((kk_scr, k, k_sw, zero, zero),
                                                   (vv_scr, v, v_sw, v_lo_rest, v_hi_rest)):
                scr[0, rows, :] = jnp.where(lo_half, a, lo_rest).astype(BF16)
                scr[1, rows, :] = jnp.where(lo_half, hi_rest, a_sw).astype(BF16)
                scr[2, rows, :] = jnp.where(lo_half, a_sw, lo_rest).astype(BF16)
                scr[3, rows, :] = jnp.where(lo_half, hi_rest, a).astype(BF16)

    scale = hd ** -0.5

    n_chunks = GQA_Q_WIDTH // LANES

    def tile(nk, rope_rows):
        def scores(c):
            qc = head_rms(q_ref[0, :, c * LANES:(c + 1) * LANES].astype(F32), gq_ref[...])
            if rope_rows is not None:
                qc = rope(qc, rope_rows)
            qc = (qc * scale).astype(BF16)
            grp = (2 * c) // (GQA_Q_HEADS // GQA_KV_HEADS)
            return [_dot_nt(qc, kk_scr[2 * grp + hh, 0:nk, :]) for hh in range(2)]

        s_next = scores(0)
        for c in range(n_chunks):
            s_pair = s_next
            if c + 1 < n_chunks:
                s_next = scores(c + 1)
            grp = (2 * c) // (GQA_Q_HEADS // GQA_KV_HEADS)
            outs = []
            for hh in range(2):
                s = s_pair[hh]
                m = jnp.max(s, axis=-1, keepdims=True)
                p = jnp.exp((s - m).astype(BF16))
                o = _dot(p, vv_scr[2 * grp + hh, 0:nk, :])
                sum_lane = hd if hh == 0 else 0
                den = jnp.sum(jnp.where(lane == sum_lane, o, 0.0), axis=-1, keepdims=True)
                outs.append(o / den)
            o_ref[0, :, c * LANES:(c + 1) * LANES] = jnp.where(lo_half, outs[0], outs[1]).astype(o_ref.dtype)

    @pl.when(j == 0)
    def _():
        tile(n_ctx, None)

    @pl.when(j > 0)
    def _():
        tile(t, pl.ds(pl.multiple_of((j - 1) * tq, tq), tq))


def _gqa_attention(z, tables, gq, gk, n_ctx):
    b, t, _ = z.shape
    tq = ROW_TILE
    assert n_ctx == tq
    n_lat = t - n_ctx
    cos, sa, sb = tables
    tab_spec = pl.BlockSpec((n_lat, LANES), lambda bi, j: (0, 0))
    g_spec = pl.BlockSpec((1, LANES), lambda bi, j: (0, 0))
    return pl.pallas_call(
        functools.partial(_gqa_kernel, n_ctx=n_ctx),
        out_shape=jax.ShapeDtypeStruct((b, t, GQA_Q_WIDTH), BF16),
        grid=(b, t // tq),
        in_specs=[
            pl.BlockSpec((1, tq, GQA_Q_WIDTH), lambda bi, j: (bi, j, COL_GQA_Q // GQA_Q_WIDTH)),
            pl.BlockSpec((1, t, GQA_KV_WIDTH), lambda bi, j: (bi, 0, COL_GQA_K // GQA_KV_WIDTH)),
            pl.BlockSpec((1, t, GQA_KV_WIDTH), lambda bi, j: (bi, 0, COL_GQA_V // GQA_KV_WIDTH)),
            tab_spec, tab_spec, tab_spec, g_spec, g_spec,
        ],
        out_specs=pl.BlockSpec((1, tq, GQA_Q_WIDTH), lambda bi, j: (bi, j, 0)),
        scratch_shapes=[pltpu.VMEM((4, t, LANES), BF16), pltpu.VMEM((4, t, LANES), BF16)],
        compiler_params=_params(("arbitrary", "arbitrary")),
        name="gqa_attention",
    )(z, z, z, cos, sa, sb, gq, gk)


HG_LEVELS = 6


def _hgrn_sum_matrices():
    c = HG_CHUNK
    u = np.arange(c)[None, :]
    r = np.arange(c)[:, None]
    fw, bw = [], []
    for lvl in range(HG_LEVELS):
        hs = c >> (lvl + 1)
        blk = (r // (2 * hs)) * (2 * hs)
        upper = (r % (2 * hs)) >= hs
        last_lower = blk + hs - 1
        first_upper = blk + hs
        fw.append(np.where(upper, (u > last_lower) & (u <= r), (u > r) & (u <= last_lower)))
        bw.append(np.where(upper, (u >= first_upper) & (u < r), (u >= r) & (u < first_upper)))
    fw += [u <= r, u > r]
    bw += [u >= r, u < r]

    def twice(parts):
        w = np.concatenate(parts).astype(np.float32)
        return jnp.asarray(np.concatenate([w, w], axis=1), BF16)

    return twice(fw), twice(bw)


def _hgrn_pair_masks():
    c = HG_CHUNK
    row = np.arange(c)[:, None]
    col = np.arange(c)[None, :]
    fw, bw = [], []
    for lvl in range(HG_LEVELS):
        hs = c >> (lvl + 1)
        same = (row // (2 * hs)) == (col // (2 * hs))
        row_up = (row % (2 * hs)) >= hs
        col_up = (col % (2 * hs)) >= hs
        fw.append(same & row_up & ~col_up)
        bw.append(same & ~row_up & col_up)
    fw.append(row == col)
    bw.append(row == col)

    def in_lane_halves(masks):
        m = np.stack(masks).astype(np.float32)
        z = np.zeros_like(m)
        return jnp.asarray(np.stack([np.concatenate([m, z], axis=-1),
                                     np.concatenate([z, m], axis=-1)], axis=1))

    assert 2 * c == LANES
    return in_lane_halves(fw), in_lane_halves(bw)


def _hgrn_kernel(q_ref, ff_ref, fb_ref, i_ref, g_ref, lbl_ref, ng_ref, wf_ref, wb_ref, mf_ref, mb_ref,
                 o_ref, o_scr, st_scr, *, layer, n_ctx):
    t = q_ref.shape[1]
    c = HG_CHUNK
    dk = HG_DK
    n = t // c
    nc = n_ctx // c
    depth = lbl_ref.shape[0]

    def lower_bound(direction):
        logits = [lbl_ref[d, direction] for d in range(depth)]
        m = functools.reduce(jnp.maximum, logits)
        e = [jnp.exp(x - m) for x in logits]
        tot = functools.reduce(lambda a, b: a + b, e)
        p = [x / tot for x in e]
        cum = functools.reduce(lambda a, b: a + b, p[:layer + 1])
        return cum - p[0]

    def heads_on_rows(x):
        return jnp.concatenate([x[:, h * dk:(h + 1) * dk] for h in range(HG_HEADS)], axis=0)

    def stream(off, f_ref, lb, w_ref, m_ref, exit_row, slot):
        rows = pl.ds(off, c)
        f = lb + (1.0 - lb) * jax.nn.sigmoid(f_ref[0, rows, :])
        kk = 1.0 - f
        decay = jnp.exp(_dot(w_ref[...], jnp.concatenate(_split_bf16(jnp.log(f)), axis=0)))
        q = _silu(q_ref[0, rows, :].astype(F32))
        v = i_ref[0, rows, :]
        a2 = [0.0] * HG_HEADS
        for lvl in range(HG_LEVELS + 1):
            if lvl < HG_LEVELS:
                y = decay[lvl * c:(lvl + 1) * c, :]
                qs, ks = heads_on_rows((q * y).astype(BF16)), heads_on_rows((kk * y).astype(BF16))
            else:
                qs, ks = heads_on_rows(q.astype(BF16)), heads_on_rows(kk.astype(BF16))
            p = _dot_nt(qs, ks)
            for h in range(HG_HEADS):
                tile = (h * c) // LANES
                slab = p[h * c:(h + 1) * c, tile * LANES:(tile + 1) * LANES]
                a2[h] = a2[h] + m_ref[lvl, h % 2] * slab
        e_cum = decay[HG_LEVELS * c:(HG_LEVELS + 1) * c, :]
        e_rest = decay[(HG_LEVELS + 1) * c:(HG_LEVELS + 2) * c, :]
        st = st_scr[slot]
        inter = _dot_nt(heads_on_rows((q * e_cum).astype(BF16)), st.astype(BF16))
        upd = _dot_tn(v, (kk * e_rest).astype(BF16))
        outs = []
        for h in range(HG_HEADS):
            cols = slice(h * dk, (h + 1) * dk)
            v2 = jnp.concatenate([v[:, cols], v[:, cols]], axis=0)
            outs.append(_dot(a2[h].astype(BF16), v2) + inter[h * c:(h + 1) * c, cols])
            st_scr[slot, cols, :] = (st[h * dk:(h + 1) * dk, :] * e_cum[exit_row:exit_row + 1, cols]
                                     + upd[h * dk:(h + 1) * dk, cols])
        o_scr[rows, :] += jnp.concatenate(outs, axis=1)

    lb_f = lower_bound(0)
    lb_b = lower_bound(1)
    st_scr[...] = jnp.zeros_like(st_scr)
    o_scr[...] = jnp.zeros_like(o_scr)

    def body(k, carry):
        off_f = pl.multiple_of(k * c, c)
        kb = jnp.where(k < nc, nc - 1 - k, n + nc - 1 - k)
        off_b = pl.multiple_of(kb * c, c)
        stream(off_f, ff_ref, lb_f, wf_ref, mf_ref, c - 1, 0)
        stream(off_b, fb_ref, lb_b, wb_ref, mb_ref, 0, 1)
        return carry

    lax.fori_loop(0, n, body, 0)

    ng = ng_ref[...]

    def readout(r, carry):
        rows = pl.ds(pl.multiple_of(r * ROW_TILE, ROW_TILE), ROW_TILE)
        for h in range(HG_HEADS):
            cols = slice(h * dk, (h + 1) * dk)
            y = (_rms(o_scr[rows, cols]) * ng) * _silu(g_ref[0, rows, cols].astype(F32))
            o_ref[0, rows, cols] = y.astype(o_ref.dtype)
        return carry

    lax.fori_loop(0, t // ROW_TILE, readout, 0)


def _hgrn(z, zf, lb_logits, norm_g, layer, n_ctx):
    b, t, _ = z.shape
    depth = lb_logits.shape[0]
    w = HG_WIDTH
    lbl = lb_logits.astype(F32).reshape(depth, 2, 1, w)
    wf, wb = _hgrn_sum_matrices()
    mf, mb = _hgrn_pair_masks()

    def zcol(base):
        return pl.BlockSpec((1, t, w), lambda bi: (bi, 0, base // w))

    def whole(shape):
        return pl.BlockSpec(shape, lambda bi: (0,) * len(shape))

    return pl.pallas_call(
        functools.partial(_hgrn_kernel, layer=layer, n_ctx=n_ctx),
        out_shape=jax.ShapeDtypeStruct((b, t, w), BF16),
        grid=(b,),
        in_specs=[
            zcol(COL_HG_Q),
            pl.BlockSpec((1, t, w), lambda bi: (bi, 0, 0)),
            pl.BlockSpec((1, t, w), lambda bi: (bi, 0, 1)),
            zcol(COL_HG_I),
            zcol(COL_HG_G),
            whole(lbl.shape), whole((1, HG_DK)), whole(wf.shape), whole(wb.shape),
            whole(mf.shape), whole(mb.shape),
        ],
        out_specs=pl.BlockSpec((1, t, w), lambda bi: (bi, 0, 0)),
        scratch_shapes=[pltpu.VMEM((t, w), F32), pltpu.VMEM((2, w, HG_DK), F32)],
        compiler_params=_params(("arbitrary",)),
        name="hgrn2_scan",
    )(z, zf, zf, z, z, lbl, norm_g.reshape(1, HG_DK).astype(F32), wf, wb, mf, mb)


def _merge_kernel(s_ref, a_ref, b_ref, c_ref, ga_ref, gb_ref, gc_ref, mod_ref, modc_ref,
                  wb_ref, wo_ref, gf_ref, wr_ref, br_ref, s_out, f_out, route_out):
    first = pl.program_id(1) == 0
    tm = s_ref.shape[1]

    y = 0.0
    for idx, (br, gr) in enumerate(((a_ref, ga_ref), (b_ref, gb_ref), (c_ref, gc_ref))):
        y = y + jax.nn.sigmoid(gr[0].astype(F32)) * _dot(br[0], wb_ref[idx])
    proj = _dot(y.astype(BF16), wo_ref[...])
    w_hi, w_lo = _split_bf16(wr_ref[...])
    for r in range(tm // ROW_TILE):
        rows = slice(r * ROW_TILE, (r + 1) * ROW_TILE)
        is_ctx = jnp.logical_and(first, r == 0)
        _merge_rows(s_ref, proj[rows, :], rows, r, is_ctx, mod_ref, modc_ref, gf_ref, w_hi, w_lo, br_ref,
                    s_out, f_out, route_out)


def _merge_rows(s_ref, proj, rows, r, is_ctx, mod_ref, modc_ref, gf_ref, w_hi, w_lo, br_ref,
                s_out, f_out, route_out):
    def mod_row(k):
        return jnp.where(is_ctx, modc_ref[k:k + 1, :], mod_ref[0, k:k + 1, :])

    x = s_ref[0, rows, :] + mod_row(2) * proj
    s_out[0, rows, :] = x
    f = (_rms(x) * gf_ref[...]) * (1.0 + mod_row(4)) + mod_row(3)
    _rows_to_tiles(f_out, r * ROW_TILE, f)

    f_hi, f_lo = _split_bf16(f)
    logits = _dot(f_hi, w_hi) + _dot(f_lo, w_hi) + _dot(f_hi, w_lo) + br_ref[...]

    lane = lax.broadcasted_iota(jnp.int32, logits.shape, 1)
    is_group = jnp.logical_and(lane >= N_EXPERTS, lane < N_EXPERTS + N_GROUPS)

    def first_argmax(x, x_max):
        return jnp.min(jnp.where(x == x_max, lane, LANES), axis=-1, keepdims=True)

    gl = jnp.where(is_group, logits, MASK_NEG)
    g_max = jnp.max(gl, axis=-1, keepdims=True)
    g_idx = first_argmax(gl, g_max) - N_EXPERTS
    p_group = 1.0 / jnp.sum(jnp.exp(gl - g_max), axis=-1, keepdims=True)
    in_group = jnp.logical_and(lane < N_EXPERTS, (lane // EXPERTS_PER_GROUP) == g_idx)
    e1 = jnp.where(in_group, logits, MASK_NEG)
    v1 = jnp.max(e1, axis=-1, keepdims=True)
    i1 = first_argmax(e1, v1)
    e2 = jnp.where(lane == i1, MASK_NEG, e1)
    v2 = jnp.max(e2, axis=-1, keepdims=True)
    i2 = first_argmax(e2, v2)
    r21 = jnp.exp(v2 - v1)
    w1 = 1.0 / (1.0 + r21)
    w2 = r21 * w1
    route = jnp.where(lane == 0, i1.astype(F32), 0.0)
    route = jnp.where(lane == 1, i2.astype(F32), route)
    route = jnp.where(lane == 2, w1 * p_group, route)
    route = jnp.where(lane == 3, w2 * p_group, route)
    route_out[0, rows, :] = route


def _merge(s, a, bb, cc, z, mod, modc, wb, wo, gf, wr, br):
    b, t, d = s.shape
    tm = WIDE_ROWS
    bw = a.shape[-1]
    gate0 = COL_GATES // d

    def rows(width, colblk=0):
        return pl.BlockSpec((1, tm, width), lambda bi, j: (bi, j, colblk))

    def whole(shape):
        return pl.BlockSpec(shape, lambda bi, j: (0,) * len(shape))

    return pl.pallas_call(
        _merge_kernel,
        out_shape=(jax.ShapeDtypeStruct((b, t, d), F32),
                   jax.ShapeDtypeStruct((b, t * SUBLANES, LANES), F32),
                   jax.ShapeDtypeStruct((b, t, LANES), F32)),
        grid=(b, t // tm),
        in_specs=[
            rows(d), rows(bw), rows(bw), rows(bw),
            rows(d, gate0), rows(d, gate0 + 1), rows(d, gate0 + 2),
            pl.BlockSpec((1, N_MOD, d), lambda bi, j: (bi, 0, 0)),
            whole((N_MOD, d)),
            whole(wb.shape), whole(wo.shape), whole((1, d)), whole(wr.shape), whole((1, LANES)),
        ],
        out_specs=(rows(d), pl.BlockSpec((1, tm * SUBLANES, LANES), lambda bi, j: (bi, j, 0)),
                   rows(LANES)),
        compiler_params=_params(("arbitrary", "arbitrary")),
        name="merge_router",
    )(s, a, bb, cc, z, z, z, mod, modc, wb, wo, gf, wr, br)


MOE_CHUNK = 384
MOE_PAD_TOKENS = SUBLANES


def _moe_kernel(tok_ref, wgt_ref, cnt_ref, off_ref, f_ref, wg_ref, wu_ref, wd_ref, y_ref,
                xs_scr, ys_scr):
    e = pl.program_id(1)
    sub = SUBLANES
    n_col = f_ref.shape[2]
    t_dummy = f_ref.shape[1]

    @pl.when(e == 0)
    def _():
        y_ref[...] = jnp.zeros_like(y_ref)
        xs_scr[...] = jnp.zeros_like(xs_scr)

    cnt = cnt_ref[0, 0, e]
    off = off_ref[0, 0, e]
    wg = wg_ref[0, 0].astype(BF16)
    wu = wu_ref[0, 0].astype(BF16)
    wd = wd_ref[0, 0].astype(BF16)

    def chunk(ci, carry):
        base = off + ci * MOE_CHUNK
        m = jnp.minimum(MOE_CHUNK, cnt - ci * MOE_CHUNK)
        n_grp = (m + sub - 1) // sub

        def gather(gi, c2):
            for u in range(sub):
                r = gi * sub + u
                tok = tok_ref[0, 0, base + r]
                xs_scr[pl.ds(pl.multiple_of(r * n_col, n_col), n_col), :] = f_ref[0, tok]
            return c2

        lax.fori_loop(0, n_grp, gather, 0)
        x = jnp.concatenate(
            [xs_scr[pl.ds(s, MOE_CHUNK, stride=n_col), :] for s in range(n_col)], axis=1).astype(BF16)
        he = (_silu(_dot(x, wg)) * _dot(x, wu)).astype(BF16)
        y = _dot(he, wd)
        for s in range(n_col):
            ys_scr[pl.ds(s, MOE_CHUNK, stride=n_col), :] = y[:, s * LANES:(s + 1) * LANES]

        def scatter(gi, c2):
            toks, vals = [], []
            for u in range(sub):
                r = gi * sub + u
                tok = jnp.where(r < m, tok_ref[0, 0, base + r], t_dummy)
                contrib = wgt_ref[0, 0, base + r] * ys_scr[pl.ds(pl.multiple_of(r * n_col, n_col), n_col), :]
                toks.append(tok)
                vals.append(y_ref[0, tok] + contrib)
            for tok, val in zip(toks, vals):
                y_ref[0, tok] = val
            return c2

        lax.fori_loop(0, n_grp, scatter, 0)
        return carry

    lax.fori_loop(0, (cnt + MOE_CHUNK - 1) // MOE_CHUNK, chunk, 0)


def _moe(f_tiles, route, w_gate, w_up, w_down, layer):
    b, t, _ = route.shape
    d, hid = w_gate.shape[-2:]
    n_col = d // LANES
    assert n_col == SUBLANES and f_tiles.shape == (b, t * n_col, LANES)
    n_slot = TOP_K * t

    eid = route[:, :, 0:TOP_K].astype(jnp.int32).reshape(b, n_slot)
    wts = route[:, :, TOP_K:2 * TOP_K].reshape(b, n_slot)
    perm = jnp.argsort(eid, axis=1).astype(jnp.int32)
    pad = jnp.zeros((b, MOE_CHUNK), jnp.int32)
    tok_sorted = jnp.concatenate([perm // TOP_K, pad], axis=1).reshape(b, 1, n_slot + MOE_CHUNK)
    wgt_sorted = jnp.concatenate([jnp.take_along_axis(wts, perm, axis=1), pad.astype(F32)],
                                 axis=1).reshape(b, 1, n_slot + MOE_CHUNK)
    counts = jnp.sum(eid[:, :, None] == jnp.arange(N_EXPERTS)[None, None, :], axis=1).astype(jnp.int32)
    offs = (jnp.cumsum(counts, axis=1) - counts).astype(jnp.int32)
    counts = counts.reshape(b, 1, N_EXPERTS)
    offs = offs.reshape(b, 1, N_EXPERTS)

    def smem(n):
        return pl.BlockSpec((1, 1, n), lambda bi, e: (bi, 0, 0), memory_space=pltpu.SMEM)

    t_out = t + MOE_PAD_TOKENS
    y4 = pl.pallas_call(
        _moe_kernel,
        out_shape=jax.ShapeDtypeStruct((b, t_out, n_col, LANES), F32),
        grid=(b, N_EXPERTS),
        in_specs=[
            smem(n_slot + MOE_CHUNK), smem(n_slot + MOE_CHUNK), smem(N_EXPERTS), smem(N_EXPERTS),
            pl.BlockSpec((1, t, n_col, LANES), lambda bi, e: (bi, 0, 0, 0),
                         pipeline_mode=pl.Buffered(1)),
            pl.BlockSpec((1, 1, d, hid), lambda bi, e: (layer, e, 0, 0)),
            pl.BlockSpec((1, 1, d, hid), lambda bi, e: (layer, e, 0, 0)),
            pl.BlockSpec((1, 1, hid, d), lambda bi, e: (layer, e, 0, 0)),
        ],
        out_specs=pl.BlockSpec((1, t_out, n_col, LANES), lambda bi, e: (bi, 0, 0, 0)),
        scratch_shapes=[pltpu.VMEM((MOE_CHUNK * n_col, LANES), F32),
                        pltpu.VMEM((MOE_CHUNK * n_col, LANES), F32)],
        compiler_params=_params(("arbitrary", "arbitrary")),
        name="moe_experts",
    )(tok_sorted, wgt_sorted, counts, offs, f_tiles.reshape(b, t, n_col, LANES), w_gate, w_up, w_down)
    return y4.reshape(b, t_out * n_col, LANES)


def _final_kernel(s_ref, y_ref, mod_ref, g_ref, o_ref):
    x = s_ref[0] + mod_ref[0, 5:6, :] * _tiles_to_rows(y_ref, 0, s_ref.shape[1])
    o_ref[0] = _rms(x) * g_ref[...]


def _final_norm(s, y, mod, g, n_ctx):
    b, t, d = s.shape
    tm = ROW_TILE
    skip = n_ctx // tm
    lat_rows = pl.BlockSpec((1, tm, d), lambda bi, j: (bi, j + skip, 0))
    return pl.pallas_call(
        _final_kernel,
        out_shape=jax.ShapeDtypeStruct((b, t - n_ctx, d), F32),
        grid=(b, (t - n_ctx) // tm),
        in_specs=[lat_rows,
                  pl.BlockSpec((1, tm * SUBLANES, LANES), lambda bi, j: (bi, j + skip, 0)),
                  pl.BlockSpec((1, N_MOD, d), lambda bi, j: (bi, 0, 0)),
                  pl.BlockSpec((1, d), lambda bi, j: (0, 0))],
        out_specs=pl.BlockSpec((1, tm, d), lambda bi, j: (bi, j, 0)),
        compiler_params=_params(("arbitrary", "arbitrary")),
        name="final_norm",
    )(s, y, mod, g)


def kernel(x, c, ctx, c_ctx, w_ada, b_ada, g_mix, g_ffn, w_in, na_rpb, hg_lb_logits, hg_norm_g,
           gqa_qnorm_g, gqa_knorm_g, w_branch, w_out, w_group_router, b_group_router,
           w_expert_router, b_expert_router, w_exp_gate, w_exp_up, w_exp_down, g_final):
    b, n_lat, d = x.shape
    n_ctx = ctx.shape[1]
    depth = w_in.shape[0]
    assert n_ctx == ROW_TILE and n_lat % ROW_TILE == 0 and (n_ctx + n_lat) % WIDE_ROWS == 0
    rows = n_lat // GRID_W

    s = jnp.concatenate([ctx, x], axis=1)

    c_rows = 16
    c_all = jnp.concatenate([c, c_ctx[None, :], jnp.zeros((c_rows - b - 1, d), c.dtype)], axis=0)
    mod_all = _ada(c_all, w_ada, b_ada).reshape(depth, c_rows, N_MOD, d)

    tables = _rope_tables(n_lat)
    rep = LANES // GQA_HEAD_DIM

    prev = None
    for l in range(depth):
        mod = mod_all[l, :b]
        modc = mod_all[l, b]
        w_pad = jnp.concatenate(
            [w_in[l, :, :COL_RAW_GATES],
             jnp.zeros((d, COL_GATES - COL_RAW_GATES), w_in.dtype),
             w_in[l, :, COL_RAW_GATES:]], axis=1).astype(BF16)
        z, zf, s = _inproj(s, g_mix[l].reshape(1, d), modc, mod, w_pad, n_ctx, prev)

        a = _na_attention(z, _na_bias_table(na_rpb[l], rows), n_ctx)
        cc = _gqa_attention(z, tables,
                            jnp.tile(gqa_qnorm_g[l].astype(F32), rep).reshape(1, LANES),
                            jnp.tile(gqa_knorm_g[l].astype(F32), rep).reshape(1, LANES), n_ctx)
        bb = _hgrn(z, zf, hg_lb_logits, hg_norm_g[l], l, n_ctx)

        wr = jnp.concatenate(
            [w_expert_router[l], w_group_router[l],
             jnp.zeros((d, LANES - N_EXPERTS - N_GROUPS), F32)], axis=1)
        br = jnp.concatenate(
            [b_expert_router[l], b_group_router[l],
             jnp.zeros((LANES - N_EXPERTS - N_GROUPS,), F32)]).reshape(1, LANES)
        s, f, route = _merge(s, a, bb, cc, z, mod, modc, w_branch[l].astype(BF16),
                             w_out[l].astype(BF16), g_ffn[l].reshape(1, d), wr, br)
        y = _moe(f, route, w_exp_gate, w_exp_up, w_exp_down, l)
        prev = (y, mod, modc)

    return _final_norm(s, prev[0], prev[1], g_final.reshape(1, d), n_ctx)
```

```python
import functools

import numpy as np
import jax
import jax.numpy as jnp
from jax import lax
from jax.experimental import pallas as pl
from jax.experimental.pallas import tpu as pltpu

F32 = jnp.float32
BF16 = jnp.bfloat16

RMS_EPS = 1e-6
N_MOD = 6
GRID_W = 64

NA_HEADS = 8
NA_HEAD_DIM = 64
NA_WIDTH = NA_HEADS * NA_HEAD_DIM
WIN_ROWS = 8
WIN_COLS = 16
NA_QROWS = 4
NA_KROWS = 12

HG_HEADS = 4
HG_DK = 128
HG_WIDTH = HG_HEADS * HG_DK
HG_CHUNK = 64

GQA_Q_HEADS = 8
GQA_KV_HEADS = 2
GQA_HEAD_DIM = 64
GQA_Q_WIDTH = GQA_Q_HEADS * GQA_HEAD_DIM
GQA_KV_WIDTH = GQA_KV_HEADS * GQA_HEAD_DIM
ROPE_THETA = 10000.0

N_GROUPS = 4
EXPERTS_PER_GROUP = 4
N_EXPERTS = N_GROUPS * EXPERTS_PER_GROUP
TOP_K = 2

LANES = 128
SUBLANES = 8
ROW_TILE = 256
WIDE_ROWS = 768
MASK_NEG = -1e30

COL_NA_Q = 0
COL_NA_K = 512
COL_NA_V = 1024
COL_HG_Q = 1536
COL_HG_FF = 2048
COL_HG_FB = 2560
COL_HG_I = 3072
COL_HG_G = 3584
COL_GQA_Q = 4096
COL_GQA_K = 4608
COL_GQA_V = 4736
COL_RAW_GATES = 4864
COL_GATES = 5120
IN_COLS_PAD = 8192
IN_TILE = 2048

VMEM_LIMIT = 56 * 1024 * 1024


def _dot(a, b):
    return jnp.dot(a, b, preferred_element_type=F32)


def _dot_nt(a, b):
    return lax.dot_general(a, b, (((1,), (1,)), ((), ())), preferred_element_type=F32)


def _dot_tn(a, b):
    return lax.dot_general(a, b, (((0,), (0,)), ((), ())), preferred_element_type=F32)


def _split_bf16(x):
    hi = x.astype(BF16)
    lo = (x - hi.astype(F32)).astype(BF16)
    return hi, lo


def _silu(x):
    return x * jax.nn.sigmoid(x)


def _rms(x):
    return x * lax.rsqrt(jnp.mean(x * x, axis=-1, keepdims=True) + RMS_EPS)


def _tiles_to_rows(ref, tok0, n):
    return jnp.concatenate(
        [ref[0, pl.ds(tok0 * SUBLANES + s, n, stride=SUBLANES), :] for s in range(SUBLANES)], axis=1)


def _rows_to_tiles(ref, tok0, x):
    n = x.shape[0]
    for s in range(SUBLANES):
        ref[0, pl.ds(tok0 * SUBLANES + s, n, stride=SUBLANES), :] = x[:, s * LANES:(s + 1) * LANES]


def _params(semantics, vmem=VMEM_LIMIT):
    return pltpu.CompilerParams(dimension_semantics=semantics, vmem_limit_bytes=vmem)


def _ada_kernel(c_ref, w_ref, b_ref, o_ref):
    sc = _silu(c_ref[...]).astype(BF16)
    o_ref[0] = _dot(sc, w_ref[0].astype(BF16)) + b_ref[0]


def _ada(c_all, w_ada, b_ada):
    depth, d, n = w_ada.shape
    rows = c_all.shape[0]
    tn = 1536
    return pl.pallas_call(
        _ada_kernel,
        out_shape=jax.ShapeDtypeStruct((depth, rows, n), F32),
        grid=(depth, n // tn),
        in_specs=[
            pl.BlockSpec((rows, d), lambda l, j: (0, 0)),
            pl.BlockSpec((1, d, tn), lambda l, j: (l, 0, j)),
            pl.BlockSpec((1, 1, tn), lambda l, j: (l, 0, j)),
        ],
        out_specs=pl.BlockSpec((1, rows, tn), lambda l, j: (l, 0, j)),
        compiler_params=_params(("arbitrary", "arbitrary")),
        name="ada_mod",
    )(c_all, w_ada, b_ada.reshape(depth, 1, n))


def _inproj_kernel(*refs, n_ctx, tiles_per_batch, residual):
    if residual:
        (s_ref, y_ref, modp_ref, modcp_ref, g_ref, modc_ref, mod_ref, w_ref,
         z_ref, zf_ref, s_out, h_scr) = refs
    else:
        s_ref, g_ref, modc_ref, mod_ref, w_ref, z_ref, zf_ref, h_scr = refs
    i = pl.program_id(0)
    j = pl.program_id(1)
    tm = s_ref.shape[1]

    @pl.when(j == 0)
    def _():
        g = g_ref[...]
        first = (i % tiles_per_batch) == 0
        for r in range(tm // ROW_TILE):
            rows = slice(r * ROW_TILE, (r + 1) * ROW_TILE)
            is_ctx = jnp.logical_and(first, r * ROW_TILE < n_ctx)

            def pick(ctx_ref, lat_ref, k):
                return jnp.where(is_ctx, ctx_ref[k:k + 1, :], lat_ref[0, k:k + 1, :])

            x = s_ref[0, rows, :]
            if residual:
                x = x + pick(modcp_ref, modp_ref, 5) * _tiles_to_rows(y_ref, r * ROW_TILE, ROW_TILE)
                s_out[0, rows, :] = x
            h = (_rms(x) * g) * (1.0 + pick(modc_ref, mod_ref, 1)) + pick(modc_ref, mod_ref, 0)
            h_scr[rows, :] = h.astype(BF16)

    acc = _dot(h_scr[...], w_ref[...])
    z_ref[0] = acc.astype(BF16)

    @pl.when(j == COL_HG_FF // IN_TILE)
    def _():
        lo = COL_HG_FF % IN_TILE
        zf_ref[0] = acc[:, lo:lo + zf_ref.shape[2]]


def _inproj(s, g, modc, mod, w_pad, n_ctx, prev=None):
    b, t, d = s.shape
    n = w_pad.shape[1]
    tm = WIDE_ROWS
    tpb = t // tm
    residual = prev is not None
    f_cols = 2 * HG_WIDTH
    assert COL_HG_FB == COL_HG_FF + HG_WIDTH and d == f_cols
    assert COL_HG_FF // IN_TILE == (COL_HG_FF + f_cols - 1) // IN_TILE

    row_spec = pl.BlockSpec((1, tm, d), lambda i, j: (i // tpb, i % tpb, 0))
    mod_spec = pl.BlockSpec((1, N_MOD, d), lambda i, j: (i // tpb, 0, 0))
    modc_spec = pl.BlockSpec((N_MOD, d), lambda i, j: (0, 0))
    in_specs = [row_spec]
    args = [s]
    if residual:
        y_spec = pl.BlockSpec((1, tm * SUBLANES, LANES), lambda i, j: (i // tpb, i % tpb, 0))
        in_specs += [y_spec, mod_spec, modc_spec]
        args += [prev[0], prev[1], prev[2]]
    in_specs += [pl.BlockSpec((1, d), lambda i, j: (0, 0)), modc_spec, mod_spec,
                 pl.BlockSpec((d, IN_TILE), lambda i, j: (0, j))]
    args += [g, modc, mod, w_pad]
    out_shape = [jax.ShapeDtypeStruct((b, t, n), BF16),
                 jax.ShapeDtypeStruct((b, t, f_cols), F32)]
    out_specs = [
        pl.BlockSpec((1, tm, IN_TILE), lambda i, j: (i // tpb, i % tpb, j)),
        row_spec,
    ]
    if residual:
        out_shape.append(jax.ShapeDtypeStruct((b, t, d), F32))
        out_specs.append(row_spec)
    outs = pl.pallas_call(
        functools.partial(_inproj_kernel, n_ctx=n_ctx, tiles_per_batch=tpb, residual=residual),
        out_shape=tuple(out_shape),
        grid=(b * tpb, n // IN_TILE),
        in_specs=in_specs,
        out_specs=tuple(out_specs),
        scratch_shapes=[pltpu.VMEM((tm, d), BF16)],
        compiler_params=_params(("arbitrary", "arbitrary")),
        name="in_proj",
    )(*args)
    return outs if residual else (outs[0], outs[1], s)


def _na_block_start(blk, rows):
    return jnp.clip(NA_QROWS * blk - WIN_ROWS // 2, 0, rows - NA_KROWS)


def _na_bias_table(rpb, rows):
    n_blk = rows // NA_QROWS
    assert rows % NA_QROWS == 0 and n_blk >= 3 and rows >= NA_KROWS and NA_KROWS % 2 == 0
    assert 2 * GRID_W == LANES
    qc = np.arange(GRID_W)[:, None]
    kc = np.arange(GRID_W)[None, :]
    c0 = np.clip(qc - WIN_COLS // 2, 0, GRID_W - WIN_COLS)
    col_ok = (kc >= c0) & (kc < c0 + WIN_COLS)
    dcol = np.clip(kc - qc + WIN_COLS - 1, 0, 2 * WIN_COLS - 2)
    per_row = jnp.where(col_ok, rpb.astype(F32)[:, :, dcol], MASK_NEG)
    per_row = jnp.concatenate([per_row, per_row], axis=-1)
    h, n_dr = per_row.shape[:2]
    return pl.pallas_call(
        functools.partial(_na_bias_kernel, rows=rows),
        out_shape=jax.ShapeDtypeStruct((h, 3, NA_QROWS * GRID_W, NA_KROWS * GRID_W), F32),
        grid=(h, 3),
        in_specs=[pl.BlockSpec((1, n_dr, GRID_W, LANES), lambda hi, p: (hi, 0, 0, 0))],
        out_specs=pl.BlockSpec((1, 1, NA_QROWS * GRID_W, NA_KROWS * GRID_W), lambda hi, p: (hi, p, 0, 0)),
        compiler_params=_params(("arbitrary", "arbitrary")),
        name="na_bias_table",
    )(per_row)


def _na_bias_kernel(t_ref, o_ref, *, rows):
    p = pl.program_id(1)
    n_blk = rows // NA_QROWS
    lo_half = lax.broadcasted_iota(jnp.int32, (GRID_W, LANES), 1) < GRID_W
    masked = jnp.full((GRID_W, LANES), MASK_NEG, F32)
    for pat, blk in enumerate((0, 1, n_blk - 1)):

        @pl.when(p == pat)
        def _():
            u0 = int(np.clip(NA_QROWS * blk - WIN_ROWS // 2, 0, rows - NA_KROWS))
            for j in range(NA_QROWS):
                r = NA_QROWS * blk + j
                r0 = int(np.clip(r - WIN_ROWS // 2, 0, rows - WIN_ROWS))
                for pair in range(NA_KROWS // 2):
                    halves = []
                    for i in (2 * pair, 2 * pair + 1):
                        krow = u0 + i
                        inside = r0 <= krow < r0 + WIN_ROWS
                        halves.append(t_ref[0, krow - r + WIN_ROWS - 1] if inside else masked)
                    o_ref[0, 0, j * GRID_W:(j + 1) * GRID_W, pair * LANES:(pair + 1) * LANES] = (
                        jnp.where(lo_half, halves[0], halves[1]))


def _na_kernel(q_ref, k_ref, v_ref, bias_ref, o_ref, *, n_ctx, rows):
    i = pl.program_id(1)
    tq = q_ref.shape[1]
    lane = lax.broadcasted_iota(jnp.int32, (tq, LANES), 1)
    lo_half = lane < NA_HEAD_DIM

    scale = NA_HEAD_DIM ** -0.5

    def pair_scores(hp, key_rows, with_bias):
        cols = slice(hp * LANES, (hp + 1) * LANES)
        q2 = q_ref[0, :, cols] * scale
        out = []
        for hh in range(2):
            keep = lo_half if hh == 0 else jnp.logical_not(lo_half)
            qm = jnp.where(keep, q2, jnp.zeros_like(q2))
            blocks = []
            for n, kr in enumerate(key_rows):
                s = _dot_nt(qm, k_ref[0, kr, cols])
                if with_bias and n == 0:
                    s = s + bias_ref[2 * hp + hh, 0]
                blocks.append(s)
            out.append(blocks)
        return out

    def pair_finish(hp, scores, key_rows):
        cols = slice(hp * LANES, (hp + 1) * LANES)
        outs = []
        for blocks in scores:
            m = functools.reduce(jnp.maximum, [jnp.max(s, axis=-1, keepdims=True) for s in blocks])
            den = 0.0
            acc = 0.0
            for s, kr in zip(blocks, key_rows):
                p = jnp.exp(s - m)
                den = den + jnp.sum(p, axis=-1, keepdims=True)
                acc = acc + _dot(p.astype(BF16), v_ref[0, kr, cols])
            outs.append(acc / den)
        o_ref[0, :, cols] = jnp.where(lo_half, outs[0], outs[1]).astype(o_ref.dtype)

    def attend(key_rows, with_bias):
        n_pairs = NA_HEADS // 2
        nxt = pair_scores(0, key_rows, with_bias)
        for hp in range(n_pairs):
            cur = nxt
            if hp + 1 < n_pairs:
                nxt = pair_scores(hp + 1, key_rows, with_bias)
            pair_finish(hp, cur, key_rows)

    ctx_rows = slice(0, n_ctx)

    @pl.when(i == 0)
    def _():
        attend([ctx_rows], False)

    @pl.when(i > 0)
    def _():
        u0 = _na_block_start(i - 1, rows)
        local_rows = pl.ds(pl.multiple_of(n_ctx + u0 * GRID_W, GRID_W), NA_KROWS * GRID_W)
        attend([local_rows, ctx_rows], True)


def _na_attention(z, bias_tab, n_ctx):
    b, t, _ = z.shape
    rows = (t - n_ctx) // GRID_W
    tq = NA_QROWS * GRID_W
    assert n_ctx == tq
    n_blk = rows // NA_QROWS
    wk = NA_KROWS * GRID_W

    def pattern(i):
        return jnp.where(i <= 1, 0, jnp.where(i == n_blk, 2, 1))

    return pl.pallas_call(
        functools.partial(_na_kernel, n_ctx=n_ctx, rows=rows),
        out_shape=jax.ShapeDtypeStruct((b, t, NA_WIDTH), BF16),
        grid=(b, 1 + n_blk),
        in_specs=[
            pl.BlockSpec((1, tq, NA_WIDTH), lambda bi, i: (bi, i, COL_NA_Q // NA_WIDTH)),
            pl.BlockSpec((1, t, NA_WIDTH), lambda bi, i: (bi, 0, COL_NA_K // NA_WIDTH)),
            pl.BlockSpec((1, t, NA_WIDTH), lambda bi, i: (bi, 0, COL_NA_V // NA_WIDTH)),
            pl.BlockSpec((NA_HEADS, 1, tq, wk), lambda bi, i: (0, pattern(i), 0, 0)),
        ],
        out_specs=pl.BlockSpec((1, tq, NA_WIDTH), lambda bi, i: (bi, i, 0)),
        compiler_params=_params(("arbitrary", "arbitrary")),
        name="na_attention",
    )(z, z, z, bias_tab)


def _rope_tables(n_tokens):
    t = jnp.arange(n_tokens)
    pos = jnp.stack([t // GRID_W, t % GRID_W], axis=-1).astype(F32)
    n_freq = GQA_HEAD_DIM // 4
    inv_freq = jnp.power(ROPE_THETA, -jnp.arange(n_freq, dtype=F32) / n_freq)
    ang = pos[:, :, None] * inv_freq
    ang = jnp.concatenate([ang, ang], axis=-1).reshape(n_tokens, GQA_HEAD_DIM)
    cos, sin = jnp.cos(ang), jnp.sin(ang)
    first = (np.arange(GQA_HEAD_DIM) % (2 * n_freq)) < n_freq
    sin_a = jnp.where(first, -sin, 0.0)
    sin_b = jnp.where(first, 0.0, sin)
    rep = LANES // GQA_HEAD_DIM
    return tuple(jnp.tile(a, (1, rep)) for a in (cos, sin_a, sin_b))


def _gqa_kernel(q_ref, k_ref, v_ref, cos_ref, sa_ref, sb_ref, gq_ref, gk_ref, o_ref,
                kk_scr, vv_scr, *, n_ctx):
    j = pl.program_id(1)
    t = k_ref.shape[1]
    tq = q_ref.shape[1]
    hd = GQA_HEAD_DIM
    quarter = hd // 4

    r_i = lax.broadcasted_iota(jnp.int32, (LANES, LANES), 0)
    c_i = lax.broadcasted_iota(jnp.int32, (LANES, LANES), 1)
    head_ones = jnp.where((r_i // hd) == (c_i // hd), 1.0, 0.0).astype(BF16)
    lane = lax.broadcasted_iota(jnp.int32, (tq, LANES), 1)
    lo_half = lane < hd

    def head_rms(x, g):
        hi, lo = _split_bf16(x * x)
        ms = (_dot(hi, head_ones) + _dot(lo, head_ones)) * (1.0 / hd)
        return (x * lax.rsqrt(ms + RMS_EPS)) * g

    def rope(x, rows):
        return (x * cos_ref[rows, :]
                + pltpu.roll(x, LANES - quarter, 1) * sa_ref[rows, :]
                + pltpu.roll(x, quarter, 1) * sb_ref[rows, :])

    @pl.when(j == 0)
    def _():
        zero = jnp.zeros((tq, LANES), F32)
        for r in range(t // tq):
            rows = slice(r * tq, (r + 1) * tq)
            k = head_rms(k_ref[0, rows, :].astype(F32), gk_ref[...])
            if r * tq >= n_ctx:
                k = rope(k, slice(r * tq - n_ctx, (r + 1) * tq - n_ctx))
            v = v_ref[0, rows, :].astype(F32)
            k_sw = pltpu.roll(k, hd, 1)
            v_sw = pltpu.roll(v, hd, 1)
            v_lo_rest = jnp.where(lane == hd, 1.0, zero)
            v_hi_rest = jnp.where(lane == 0, 1.0, zero)
            for scr, a, a_sw, lo_rest, hi_rest in ((kk_scr, k, k_sw, zero, zero),
                                                   (vv_scr, v, v_sw, v_lo_rest, v_hi_rest)):
                scr[0, rows, :] = jnp.where(lo_half, a, lo_rest).astype(BF16)
                scr[1, rows, :] = jnp.where(lo_half, hi_rest, a_sw).astype(BF16)
                scr[2, rows, :] = jnp.where(lo_half, a_sw, lo_rest).astype(BF16)
                scr[3, rows, :] = jnp.where(lo_half, hi_rest, a).astype(BF16)

    scale = hd ** -0.5

    n_chunks = GQA_Q_WIDTH // LANES

    def tile(nk, rope_rows):
        def scores(c):
            qc = head_rms(q_ref[0, :, c * LANES:(c + 1) * LANES].astype(F32), gq_ref[...])
            if rope_rows is not None:
                qc = rope(qc, rope_rows)
            qc = (qc * scale).astype(BF16)
            grp = (2 * c) // (GQA_Q_HEADS // GQA_KV_HEADS)
            return [_dot_nt(qc, kk_scr[2 * grp + hh, 0:nk, :]) for hh in range(2)]

        ahead = 2
        queue = [scores(c) for c in range(min(ahead, n_chunks))]
        for c in range(n_chunks):
            s_pair = queue.pop(0)
            if c + ahead < n_chunks:
                queue.append(scores(c + ahead))
            grp = (2 * c) // (GQA_Q_HEADS // GQA_KV_HEADS)
            outs = []
            for hh in range(2):
                s = s_pair[hh]
                m = jnp.max(s, axis=-1, keepdims=True)
                p = jnp.exp((s - m).astype(BF16))
                o = _dot(p, vv_scr[2 * grp + hh, 0:nk, :])
                sum_lane = hd if hh == 0 else 0
                den = jnp.sum(jnp.where(lane == sum_lane, o, 0.0), axis=-1, keepdims=True)
                outs.append(o / den)
            o_ref[0, :, c * LANES:(c + 1) * LANES] = jnp.where(lo_half, outs[0], outs[1]).astype(o_ref.dtype)

    @pl.when(j == 0)
    def _():
        tile(n_ctx, None)

    @pl.when(j > 0)
    def _():
        tile(t, pl.ds(pl.multiple_of((j - 1) * tq, tq), tq))


def _gqa_attention(z, tables, gq, gk, n_ctx):
    b, t, _ = z.shape
    tq = ROW_TILE
    assert n_ctx == tq
    n_lat = t - n_ctx
    cos, sa, sb = tables
    tab_spec = pl.BlockSpec((n_lat, LANES), lambda bi, j: (0, 0))
    g_spec = pl.BlockSpec((1, LANES), lambda bi, j: (0, 0))
    return pl.pallas_call(
        functools.partial(_gqa_kernel, n_ctx=n_ctx),
        out_shape=jax.ShapeDtypeStruct((b, t, GQA_Q_WIDTH), BF16),
        grid=(b, t // tq),
        in_specs=[
            pl.BlockSpec((1, tq, GQA_Q_WIDTH), lambda bi, j: (bi, j, COL_GQA_Q // GQA_Q_WIDTH)),
            pl.BlockSpec((1, t, GQA_KV_WIDTH), lambda bi, j: (bi, 0, COL_GQA_K // GQA_KV_WIDTH)),
            pl.BlockSpec((1, t, GQA_KV_WIDTH), lambda bi, j: (bi, 0, COL_GQA_V // GQA_KV_WIDTH)),
            tab_spec, tab_spec, tab_spec, g_spec, g_spec,
        ],
        out_specs=pl.BlockSpec((1, tq, GQA_Q_WIDTH), lambda bi, j: (bi, j, 0)),
        scratch_shapes=[pltpu.VMEM((4, t, LANES), BF16), pltpu.VMEM((4, t, LANES), BF16)],
        compiler_params=_params(("arbitrary", "arbitrary")),
        name="gqa_attention",
    )(z, z, z, cos, sa, sb, gq, gk)


HG_LEVELS = 6


def _hgrn_sum_matrices():
    c = HG_CHUNK
    u = np.arange(c)[None, :]
    r = np.arange(c)[:, None]
    fw, bw = [], []
    for lvl in range(HG_LEVELS):
        hs = c >> (lvl + 1)
        blk = (r // (2 * hs)) * (2 * hs)
        upper = (r % (2 * hs)) >= hs
        last_lower = blk + hs - 1
        first_upper = blk + hs
        fw.append(np.where(upper, (u > last_lower) & (u <= r), (u > r) & (u <= last_lower)))
        bw.append(np.where(upper, (u >= first_upper) & (u < r), (u >= r) & (u < first_upper)))
    fw += [u <= r, u > r]
    bw += [u >= r, u < r]

    def twice(parts):
        w = np.concatenate(parts).astype(np.float32)
        return jnp.asarray(np.concatenate([w, w], axis=1), BF16)

    return twice(fw), twice(bw)


def _hgrn_pair_masks():
    c = HG_CHUNK
    row = np.arange(c)[:, None]
    col = np.arange(c)[None, :]
    fw, bw = [], []
    for lvl in range(HG_LEVELS):
        hs = c >> (lvl + 1)
        same = (row // (2 * hs)) == (col // (2 * hs))
        row_up = (row % (2 * hs)) >= hs
        col_up = (col % (2 * hs)) >= hs
        fw.append(same & row_up & ~col_up)
        bw.append(same & ~row_up & col_up)
    fw.append(row == col)
    bw.append(row == col)

    def in_lane_halves(masks):
        m = np.stack(masks).astype(np.float32)
        z = np.zeros_like(m)
        return jnp.asarray(np.stack([np.concatenate([m, z], axis=-1),
                                     np.concatenate([z, m], axis=-1)], axis=1))

    assert 2 * c == LANES
    return in_lane_halves(fw), in_lane_halves(bw)


def _hgrn_kernel(q_ref, ff_ref, fb_ref, i_ref, g_ref, lbl_ref, ng_ref, wf_ref, wb_ref, mf_ref, mb_ref,
                 o_ref, o_scr, st_scr, *, layer, n_ctx):
    t = q_ref.shape[1]
    c = HG_CHUNK
    dk = HG_DK
    n = t // c
    nc = n_ctx // c
    depth = lbl_ref.shape[0]

    def lower_bound(direction):
        logits = [lbl_ref[d, direction] for d in range(depth)]
        m = functools.reduce(jnp.maximum, logits)
        e = [jnp.exp(x - m) for x in logits]
        tot = functools.reduce(lambda a, b: a + b, e)
        p = [x / tot for x in e]
        cum = functools.reduce(lambda a, b: a + b, p[:layer + 1])
        return cum - p[0]

    def heads_on_rows(x):
        return jnp.concatenate([x[:, h * dk:(h + 1) * dk] for h in range(HG_HEADS)], axis=0)

    def stream(off, f_ref, lb, w_ref, m_ref, exit_row, slot):
        rows = pl.ds(off, c)
        f = lb + (1.0 - lb) * jax.nn.sigmoid(f_ref[0, rows, :])
        kk = 1.0 - f
        decay = jnp.exp(_dot(w_ref[...], jnp.concatenate(_split_bf16(jnp.log(f)), axis=0)))
        q = _silu(q_ref[0, rows, :].astype(F32))
        v = i_ref[0, rows, :]
        a2 = [0.0] * HG_HEADS
        for lvl in range(HG_LEVELS + 1):
            if lvl < HG_LEVELS:
                y = decay[lvl * c:(lvl + 1) * c, :]
                qs, ks = heads_on_rows((q * y).astype(BF16)), heads_on_rows((kk * y).astype(BF16))
            else:
                qs, ks = heads_on_rows(q.astype(BF16)), heads_on_rows(kk.astype(BF16))
            p = _dot_nt(qs, ks)
            for h in range(HG_HEADS):
                tile = (h * c) // LANES
                slab = p[h * c:(h + 1) * c, tile * LANES:(tile + 1) * LANES]
                a2[h] = a2[h] + m_ref[lvl, h % 2] * slab
        e_cum = decay[HG_LEVELS * c:(HG_LEVELS + 1) * c, :]
        e_rest = decay[(HG_LEVELS + 1) * c:(HG_LEVELS + 2) * c, :]
        qe = (q * e_cum).astype(BF16)
        kd = (kk * e_rest).astype(BF16)
        outs = []
        for h in range(HG_HEADS):
            cols = slice(h * dk, (h + 1) * dk)
            st = st_scr[slot, cols, :]
            v_h = v[:, cols]
            v2 = jnp.concatenate([v_h, v_h], axis=0)
            outs.append(_dot(a2[h].astype(BF16), v2) + _dot_nt(qe[:, cols], st.astype(BF16)))
            st_scr[slot, cols, :] = st * e_cum[exit_row:exit_row + 1, cols] + _dot_tn(v_h, kd[:, cols])
        o_scr[rows, :] += jnp.concatenate(outs, axis=1)

    lb_f = lower_bound(0)
    lb_b = lower_bound(1)
    st_scr[...] = jnp.zeros_like(st_scr)
    o_scr[...] = jnp.zeros_like(o_scr)

    unroll = 2
    assert n % unroll == 0

    def body(k2, carry):
        for u in range(unroll):
            k = k2 * unroll + u
            off_f = pl.multiple_of(k * c, c)
            kb = jnp.where(k < nc, nc - 1 - k, n + nc - 1 - k)
            off_b = pl.multiple_of(kb * c, c)
            stream(off_f, ff_ref, lb_f, wf_ref, mf_ref, c - 1, 0)
            stream(off_b, fb_ref, lb_b, wb_ref, mb_ref, 0, 1)
        return carry

    lax.fori_loop(0, n // unroll, body, 0)

    ng = ng_ref[...]

    def readout(r, carry):
        rows = pl.ds(pl.multiple_of(r * ROW_TILE, ROW_TILE), ROW_TILE)
        for h in range(HG_HEADS):
            cols = slice(h * dk, (h + 1) * dk)
            y = (_rms(o_scr[rows, cols]) * ng) * _silu(g_ref[0, rows, cols].astype(F32))
            o_ref[0, rows, cols] = y.astype(o_ref.dtype)
        return carry

    lax.fori_loop(0, t // ROW_TILE, readout, 0)


def _hgrn(z, zf, lb_logits, norm_g, layer, n_ctx):
    b, t, _ = z.shape
    depth = lb_logits.shape[0]
    w = HG_WIDTH
    lbl = lb_logits.astype(F32).reshape(depth, 2, 1, w)
    wf, wb = _hgrn_sum_matrices()
    mf, mb = _hgrn_pair_masks()

    def zcol(base):
        return pl.BlockSpec((1, t, w), lambda bi: (bi, 0, base // w))

    def whole(shape):
        return pl.BlockSpec(shape, lambda bi: (0,) * len(shape))

    return pl.pallas_call(
        functools.partial(_hgrn_kernel, layer=layer, n_ctx=n_ctx),
        out_shape=jax.ShapeDtypeStruct((b, t, w), BF16),
        grid=(b,),
        in_specs=[
            zcol(COL_HG_Q),
            pl.BlockSpec((1, t, w), lambda bi: (bi, 0, 0)),
            pl.BlockSpec((1, t, w), lambda bi: (bi, 0, 1)),
            zcol(COL_HG_I),
            zcol(COL_HG_G),
            whole(lbl.shape), whole((1, HG_DK)), whole(wf.shape), whole(wb.shape),
            whole(mf.shape), whole(mb.shape),
        ],
        out_specs=pl.BlockSpec((1, t, w), lambda bi: (bi, 0, 0)),
        scratch_shapes=[pltpu.VMEM((t, w), F32), pltpu.VMEM((2, w, HG_DK), F32)],
        compiler_params=_params(("arbitrary",)),
        name="hgrn2_scan",
    )(z, zf, zf, z, z, lbl, norm_g.reshape(1, HG_DK).astype(F32), wf, wb, mf, mb)


def _merge_kernel(s_ref, a_ref, b_ref, c_ref, ga_ref, gb_ref, gc_ref, mod_ref, modc_ref,
                  wb_ref, wo_ref, gf_ref, wr_ref, br_ref, s_out, f_out, route_out):
    first = pl.program_id(1) == 0
    tm = s_ref.shape[1]

    y = 0.0
    for idx, (br, gr) in enumerate(((a_ref, ga_ref), (b_ref, gb_ref), (c_ref, gc_ref))):
        y = y + jax.nn.sigmoid(gr[0].astype(F32)) * _dot(br[0], wb_ref[idx])
    proj = _dot(y.astype(BF16), wo_ref[...])
    w_hi, w_lo = _split_bf16(wr_ref[...])
    for r in range(tm // ROW_TILE):
        rows = slice(r * ROW_TILE, (r + 1) * ROW_TILE)
        is_ctx = jnp.logical_and(first, r == 0)
        _merge_rows(s_ref, proj[rows, :], rows, r, is_ctx, mod_ref, modc_ref, gf_ref, w_hi, w_lo, br_ref,
                    s_out, f_out, route_out)


def _merge_rows(s_ref, proj, rows, r, is_ctx, mod_ref, modc_ref, gf_ref, w_hi, w_lo, br_ref,
                s_out, f_out, route_out):
    def mod_row(k):
        return jnp.where(is_ctx, modc_ref[k:k + 1, :], mod_ref[0, k:k + 1, :])

    x = s_ref[0, rows, :] + mod_row(2) * proj
    s_out[0, rows, :] = x
    f = (_rms(x) * gf_ref[...]) * (1.0 + mod_row(4)) + mod_row(3)
    _rows_to_tiles(f_out, r * ROW_TILE, f)

    f_hi, f_lo = _split_bf16(f)
    logits = _dot(f_hi, w_hi) + _dot(f_lo, w_hi) + _dot(f_hi, w_lo) + br_ref[...]

    lane = lax.broadcasted_iota(jnp.int32, logits.shape, 1)
    is_group = jnp.logical_and(lane >= N_EXPERTS, lane < N_EXPERTS + N_GROUPS)

    def first_argmax(x, x_max):
        return jnp.min(jnp.where(x == x_max, lane, LANES), axis=-1, keepdims=True)

    gl = jnp.where(is_group, logits, MASK_NEG)
    g_max = jnp.max(gl, axis=-1, keepdims=True)
    g_idx = first_argmax(gl, g_max) - N_EXPERTS
    p_group = 1.0 / jnp.sum(jnp.exp(gl - g_max), axis=-1, keepdims=True)
    in_group = jnp.logical_and(lane < N_EXPERTS, (lane // EXPERTS_PER_GROUP) == g_idx)
    e1 = jnp.where(in_group, logits, MASK_NEG)
    v1 = jnp.max(e1, axis=-1, keepdims=True)
    i1 = first_argmax(e1, v1)
    e2 = jnp.where(lane == i1, MASK_NEG, e1)
    v2 = jnp.max(e2, axis=-1, keepdims=True)
    i2 = first_argmax(e2, v2)
    r21 = jnp.exp(v2 - v1)
    w1 = 1.0 / (1.0 + r21)
    w2 = r21 * w1
    route = jnp.where(lane == 0, i1.astype(F32), 0.0)
    route = jnp.where(lane == 1, i2.astype(F32), route)
    route = jnp.where(lane == 2, w1 * p_group, route)
    route = jnp.where(lane == 3, w2 * p_group, route)
    route_out[0, rows, :] = route


def _merge(s, a, bb, cc, z, mod, modc, wb, wo, gf, wr, br):
    b, t, d = s.shape
    tm = WIDE_ROWS
    bw = a.shape[-1]
    gate0 = COL_GATES // d

    def rows(width, colblk=0):
        return pl.BlockSpec((1, tm, width), lambda bi, j: (bi, j, colblk))

    def whole(shape):
        return pl.BlockSpec(shape, lambda bi, j: (0,) * len(shape))

    return pl.pallas_call(
        _merge_kernel,
        out_shape=(jax.ShapeDtypeStruct((b, t, d), F32),
                   jax.ShapeDtypeStruct((b, t * SUBLANES, LANES), F32),
                   jax.ShapeDtypeStruct((b, t, LANES), F32)),
        grid=(b, t // tm),
        in_specs=[
            rows(d), rows(bw), rows(bw), rows(bw),
            rows(d, gate0), rows(d, gate0 + 1), rows(d, gate0 + 2),
            pl.BlockSpec((1, N_MOD, d), lambda bi, j: (bi, 0, 0)),
            whole((N_MOD, d)),
            whole(wb.shape), whole(wo.shape), whole((1, d)), whole(wr.shape), whole((1, LANES)),
        ],
        out_specs=(rows(d), pl.BlockSpec((1, tm * SUBLANES, LANES), lambda bi, j: (bi, j, 0)),
                   rows(LANES)),
        compiler_params=_params(("arbitrary", "arbitrary")),
        name="merge_router",
    )(s, a, bb, cc, z, z, z, mod, modc, wb, wo, gf, wr, br)


MOE_CHUNK = 384
MOE_PAD_TOKENS = SUBLANES


def _moe_kernel(tok_ref, wgt_ref, cnt_ref, off_ref, f_ref, wg_ref, wu_ref, wd_ref, y_ref,
                xs_scr, ys_scr):
    e = pl.program_id(1)
    sub = SUBLANES
    n_col = f_ref.shape[2]
    t_dummy = f_ref.shape[1]

    @pl.when(e == 0)
    def _():
        y_ref[...] = jnp.zeros_like(y_ref)
        xs_scr[...] = jnp.zeros_like(xs_scr)

    cnt = cnt_ref[0, 0, e]
    off = off_ref[0, 0, e]
    wg = wg_ref[0, 0].astype(BF16)
    wu = wu_ref[0, 0].astype(BF16)
    wd = wd_ref[0, 0].astype(BF16)

    def chunk(ci, carry):
        base = off + ci * MOE_CHUNK
        m = jnp.minimum(MOE_CHUNK, cnt - ci * MOE_CHUNK)
        n_grp = (m + sub - 1) // sub

        def gather(gi, c2):
            for u in range(sub):
                r = gi * sub + u
                tok = tok_ref[0, 0, base + r]
                xs_scr[pl.ds(pl.multiple_of(r * n_col, n_col), n_col), :] = f_ref[0, tok]
            return c2

        lax.fori_loop(0, n_grp, gather, 0)
        x = jnp.concatenate(
            [xs_scr[pl.ds(s, MOE_CHUNK, stride=n_col), :] for s in range(n_col)], axis=1).astype(BF16)
        he = (_silu(_dot(x, wg)) * _dot(x, wu)).astype(BF16)
        y = _dot(he, wd)
        for s in range(n_col):
            ys_scr[pl.ds(s, MOE_CHUNK, stride=n_col), :] = y[:, s * LANES:(s + 1) * LANES]

        def scatter(gi, c2):
            toks, vals = [], []
            for u in range(sub):
                r = gi * sub + u
                tok = jnp.where(r < m, tok_ref[0, 0, base + r], t_dummy)
                contrib = wgt_ref[0, 0, base + r] * ys_scr[pl.ds(pl.multiple_of(r * n_col, n_col), n_col), :]
                toks.append(tok)
                vals.append(y_ref[0, tok] + contrib)
            for tok, val in zip(toks, vals):
                y_ref[0, tok] = val
            return c2

        lax.fori_loop(0, n_grp, scatter, 0)
        return carry

    lax.fori_loop(0, (cnt + MOE_CHUNK - 1) // MOE_CHUNK, chunk, 0)


def _moe(f_tiles, route, w_gate, w_up, w_down, layer):
    b, t, _ = route.shape
    d, hid = w_gate.shape[-2:]
    n_col = d // LANES
    assert n_col == SUBLANES and f_tiles.shape == (b, t * n_col, LANES)
    n_slot = TOP_K * t

    eid = route[:, :, 0:TOP_K].astype(jnp.int32).reshape(b, n_slot)
    wts = route[:, :, TOP_K:2 * TOP_K].reshape(b, n_slot)
    perm = jnp.argsort(eid, axis=1).astype(jnp.int32)
    pad = jnp.zeros((b, MOE_CHUNK), jnp.int32)
    tok_sorted = jnp.concatenate([perm // TOP_K, pad], axis=1).reshape(b, 1, n_slot + MOE_CHUNK)
    wgt_sorted = jnp.concatenate([jnp.take_along_axis(wts, perm, axis=1), pad.astype(F32)],
                                 axis=1).reshape(b, 1, n_slot + MOE_CHUNK)
    counts = jnp.sum(eid[:, :, None] == jnp.arange(N_EXPERTS)[None, None, :], axis=1).astype(jnp.int32)
    offs = (jnp.cumsum(counts, axis=1) - counts).astype(jnp.int32)
    counts = counts.reshape(b, 1, N_EXPERTS)
    offs = offs.reshape(b, 1, N_EXPERTS)

    def smem(n):
        return pl.BlockSpec((1, 1, n), lambda bi, e: (bi, 0, 0), memory_space=pltpu.SMEM)

    t_out = t + MOE_PAD_TOKENS
    y4 = pl.pallas_call(
        _moe_kernel,
        out_shape=jax.ShapeDtypeStruct((b, t_out, n_col, LANES), F32),
        grid=(b, N_EXPERTS),
        in_specs=[
            smem(n_slot + MOE_CHUNK), smem(n_slot + MOE_CHUNK), smem(N_EXPERTS), smem(N_EXPERTS),
            pl.BlockSpec((1, t, n_col, LANES), lambda bi, e: (bi, 0, 0, 0),
                         pipeline_mode=pl.Buffered(1)),
            pl.BlockSpec((1, 1, d, hid), lambda bi, e: (layer, e, 0, 0)),
            pl.BlockSpec((1, 1, d, hid), lambda bi, e: (layer, e, 0, 0)),
            pl.BlockSpec((1, 1, hid, d), lambda bi, e: (layer, e, 0, 0)),
        ],
        out_specs=pl.BlockSpec((1, t_out, n_col, LANES), lambda bi, e: (bi, 0, 0, 0)),
        scratch_shapes=[pltpu.VMEM((MOE_CHUNK * n_col, LANES), F32),
                        pltpu.VMEM((MOE_CHUNK * n_col, LANES), F32)],
        compiler_params=_params(("arbitrary", "arbitrary")),
        name="moe_experts",
    )(tok_sorted, wgt_sorted, counts, offs, f_tiles.reshape(b, t, n_col, LANES), w_gate, w_up, w_down)
    return y4.reshape(b, t_out * n_col, LANES)


def _final_kernel(s_ref, y_ref, mod_ref, g_ref, o_ref):
    x = s_ref[0] + mod_ref[0, 5:6, :] * _tiles_to_rows(y_ref, 0, s_ref.shape[1])
    o_ref[0] = _rms(x) * g_ref[...]


def _final_norm(s, y, mod, g, n_ctx):
    b, t, d = s.shape
    tm = ROW_TILE
    skip = n_ctx // tm
    lat_rows = pl.BlockSpec((1, tm, d), lambda bi, j: (bi, j + skip, 0))
    return pl.pallas_call(
        _final_kernel,
        out_shape=jax.ShapeDtypeStruct((b, t - n_ctx, d), F32),
        grid=(b, (t - n_ctx) // tm),
        in_specs=[lat_rows,
                  pl.BlockSpec((1, tm * SUBLANES, LANES), lambda bi, j: (bi, j + skip, 0)),
                  pl.BlockSpec((1, N_MOD, d), lambda bi, j: (bi, 0, 0)),
                  pl.BlockSpec((1, d), lambda bi, j: (0, 0))],
        out_specs=pl.BlockSpec((1, tm, d), lambda bi, j: (bi, j, 0)),
        compiler_params=_params(("arbitrary", "arbitrary")),
        name="final_norm",
    )(s, y, mod, g)


def kernel(x, c, ctx, c_ctx, w_ada, b_ada, g_mix, g_ffn, w_in, na_rpb, hg_lb_logits, hg_norm_g,
           gqa_qnorm_g, gqa_knorm_g, w_branch, w_out, w_group_router, b_group_router,
           w_expert_router, b_expert_router, w_exp_gate, w_exp_up, w_exp_down, g_final):
    b, n_lat, d = x.shape
    n_ctx = ctx.shape[1]
    depth = w_in.shape[0]
    assert n_ctx == ROW_TILE and n_lat % ROW_TILE == 0 and (n_ctx + n_lat) % WIDE_ROWS == 0
    rows = n_lat // GRID_W

    s = jnp.concatenate([ctx, x], axis=1)

    c_rows = 16
    c_all = jnp.concatenate([c, c_ctx[None, :], jnp.zeros((c_rows - b - 1, d), c.dtype)], axis=0)
    mod_all = _ada(c_all, w_ada, b_ada).reshape(depth, c_rows, N_MOD, d)

    tables = _rope_tables(n_lat)
    rep = LANES // GQA_HEAD_DIM

    prev = None
    for l in range(depth):
        mod = mod_all[l, :b]
        modc = mod_all[l, b]
        w_pad = jnp.concatenate(
            [w_in[l, :, :COL_RAW_GATES],
             jnp.zeros((d, COL_GATES - COL_RAW_GATES), w_in.dtype),
             w_in[l, :, COL_RAW_GATES:]], axis=1).astype(BF16)
        z, zf, s = _inproj(s, g_mix[l].reshape(1, d), modc, mod, w_pad, n_ctx, prev)

        a = _na_attention(z, _na_bias_table(na_rpb[l], rows), n_ctx)
        cc = _gqa_attention(z, tables,
                            jnp.tile(gqa_qnorm_g[l].astype(F32), rep).reshape(1, LANES),
                            jnp.tile(gqa_knorm_g[l].astype(F32), rep).reshape(1, LANES), n_ctx)
        bb = _hgrn(z, zf, hg_lb_logits, hg_norm_g[l], l, n_ctx)

        wr = jnp.concatenate(
            [w_expert_router[l], w_group_router[l],
             jnp.zeros((d, LANES - N_EXPERTS - N_GROUPS), F32)], axis=1)
        br = jnp.concatenate(
            [b_expert_router[l], b_group_router[l],
             jnp.zeros((LANES - N_EXPERTS - N_GROUPS,), F32)]).reshape(1, LANES)
        s, f, route = _merge(s, a, bb, cc, z, mod, modc, w_branch[l].astype(BF16),
                             w_out[l].astype(BF16), g_ffn[l].reshape(1, d), wr, br)
        y = _moe(f, route, w_exp_gate, w_exp_up, w_exp_down, l)
        prev = (y, mod, modc)

    return _final_norm(s, prev[0], prev[1], g_final.reshape(1, d), n_ctx)
```

```python
import functools

import numpy as np
import jax
import jax.numpy as jnp
from jax import lax
from jax.experimental import pallas as pl
from jax.experimental.pallas import tpu as pltpu

F32 = jnp.float32
BF16 = jnp.bfloat16

RMS_EPS = 1e-6
N_MOD = 6
GRID_W = 64

NA_HEADS = 8
NA_HEAD_DIM = 64
NA_WIDTH = NA_HEADS * NA_HEAD_DIM
WIN_ROWS = 8
WIN_COLS = 16
NA_QROWS = 4
NA_KROWS = 12

HG_HEADS = 4
HG_DK = 128
HG_WIDTH = HG_HEADS * HG_DK
HG_CHUNK = 64

GQA_Q_HEADS = 8
GQA_KV_HEADS = 2
GQA_HEAD_DIM = 64
GQA_Q_WIDTH = GQA_Q_HEADS * GQA_HEAD_DIM
GQA_KV_WIDTH = GQA_KV_HEADS * GQA_HEAD_DIM
ROPE_THETA = 10000.0

N_GROUPS = 4
EXPERTS_PER_GROUP = 4
N_EXPERTS = N_GROUPS * EXPERTS_PER_GROUP
TOP_K = 2

LANES = 128
SUBLANES = 8
ROW_TILE = 256
WIDE_ROWS = 768
MASK_NEG = -1e30

COL_NA_Q = 0
COL_NA_K = 512
COL_NA_V = 1024
COL_HG_Q = 1536
COL_HG_FF = 2048
COL_HG_FB = 2560
COL_HG_I = 3072
COL_HG_G = 3584
COL_GQA_Q = 4096
COL_GQA_K = 4608
COL_GQA_V = 4736
COL_RAW_GATES = 4864
COL_GATES = 5120
IN_COLS_PAD = 8192
IN_TILE = 2048

VMEM_LIMIT = 56 * 1024 * 1024


def _dot(a, b):
    return jnp.dot(a, b, preferred_element_type=F32)


def _dot_nt(a, b):
    return lax.dot_general(a, b, (((1,), (1,)), ((), ())), preferred_element_type=F32)


def _dot_tn(a, b):
    return lax.dot_general(a, b, (((0,), (0,)), ((), ())), preferred_element_type=F32)


def _split_bf16(x):
    hi = x.astype(BF16)
    lo = (x - hi.astype(F32)).astype(BF16)
    return hi, lo


def _silu(x):
    return x * jax.nn.sigmoid(x)


def _rms(x):
    return x * lax.rsqrt(jnp.mean(x * x, axis=-1, keepdims=True) + RMS_EPS)


def _tiles_to_rows(ref, tok0, n):
    return jnp.concatenate(
        [ref[0, pl.ds(tok0 * SUBLANES + s, n, stride=SUBLANES), :] for s in range(SUBLANES)], axis=1)


def _rows_to_tiles(ref, tok0, x):
    n = x.shape[0]
    for s in range(SUBLANES):
        ref[0, pl.ds(tok0 * SUBLANES + s, n, stride=SUBLANES), :] = x[:, s * LANES:(s + 1) * LANES]


def _params(semantics, vmem=VMEM_LIMIT):
    return pltpu.CompilerParams(dimension_semantics=semantics, vmem_limit_bytes=vmem)


def _ada_kernel(c_ref, w_ref, b_ref, o_ref):
    sc = _silu(c_ref[...]).astype(BF16)
    o_ref[0] = _dot(sc, w_ref[0].astype(BF16)) + b_ref[0]


def _ada(c_all, w_ada, b_ada):
    depth, d, n = w_ada.shape
    rows = c_all.shape[0]
    tn = 1536
    return pl.pallas_call(
        _ada_kernel,
        out_shape=jax.ShapeDtypeStruct((depth, rows, n), F32),
        grid=(depth, n // tn),
        in_specs=[
            pl.BlockSpec((rows, d), lambda l, j: (0, 0)),
            pl.BlockSpec((1, d, tn), lambda l, j: (l, 0, j)),
            pl.BlockSpec((1, 1, tn), lambda l, j: (l, 0, j)),
        ],
        out_specs=pl.BlockSpec((1, rows, tn), lambda l, j: (l, 0, j)),
        compiler_params=_params(("arbitrary", "arbitrary")),
        name="ada_mod",
    )(c_all, w_ada, b_ada.reshape(depth, 1, n))


def _inproj_kernel(*refs, n_ctx, tiles_per_batch, residual):
    n_sub = WIDE_ROWS // ROW_TILE
    if residual:
        (s_ref, y_ref, modp_ref, modcp_ref, g_ref, modc_ref, mod_ref, w_ref,
         z_ref, zf_ref, s_out, h_scr) = refs
    else:
        ctx_ref, *lat_refs = refs[:1 + n_sub]
        g_ref, modc_ref, mod_ref, w_ref, z_ref, zf_ref, s_out, h_scr = refs[1 + n_sub:]
    i = pl.program_id(0)
    j = pl.program_id(1)

    @pl.when(j == 0)
    def _():
        g = g_ref[...]
        first = (i % tiles_per_batch) == 0
        for r in range(n_sub):
            rows = slice(r * ROW_TILE, (r + 1) * ROW_TILE)
            is_ctx = jnp.logical_and(first, r * ROW_TILE < n_ctx)

            def pick(ctx_ref, lat_ref, k):
                return jnp.where(is_ctx, ctx_ref[k:k + 1, :], lat_ref[0, k:k + 1, :])

            if residual:
                x = s_ref[0, rows, :]
                x = x + pick(modcp_ref, modp_ref, 5) * _tiles_to_rows(y_ref, r * ROW_TILE, ROW_TILE)
            elif r == 0:
                x = jnp.where(is_ctx, ctx_ref[0], lat_refs[0][0])
            else:
                x = lat_refs[r][0]
            s_out[0, rows, :] = x
            h = (_rms(x) * g) * (1.0 + pick(modc_ref, mod_ref, 1)) + pick(modc_ref, mod_ref, 0)
            h_scr[rows, :] = h.astype(BF16)

    acc = _dot(h_scr[...], w_ref[...])
    z_ref[0] = acc.astype(BF16)

    @pl.when(j == COL_HG_FF // IN_TILE)
    def _():
        lo = COL_HG_FF % IN_TILE
        zf_ref[0] = acc[:, lo:lo + zf_ref.shape[2]]


def _inproj(stream, g, modc, mod, w_pad, n_ctx, prev=None):
    residual = prev is not None
    if residual:
        b, t, d = stream.shape
    else:
        ctx, x = stream
        b, n_lat, d = x.shape
        t = n_ctx + n_lat
        assert ctx.shape == (b, n_ctx, d) and n_ctx == ROW_TILE
    n = w_pad.shape[1]
    tm = WIDE_ROWS
    tpb = t // tm
    n_sub = tm // ROW_TILE
    f_cols = 2 * HG_WIDTH
    assert COL_HG_FB == COL_HG_FF + HG_WIDTH and d == f_cols
    assert COL_HG_FF // IN_TILE == (COL_HG_FF + f_cols - 1) // IN_TILE

    row_spec = pl.BlockSpec((1, tm, d), lambda i, j: (i // tpb, i % tpb, 0))
    mod_spec = pl.BlockSpec((1, N_MOD, d), lambda i, j: (i // tpb, 0, 0))
    modc_spec = pl.BlockSpec((N_MOD, d), lambda i, j: (0, 0))
    if residual:
        y_spec = pl.BlockSpec((1, tm * SUBLANES, LANES), lambda i, j: (i // tpb, i % tpb, 0))
        in_specs = [row_spec, y_spec, mod_spec, modc_spec]
        args = [stream, prev[0], prev[1], prev[2]]
    else:
        def lat_spec(r):
            return pl.BlockSpec(
                (1, ROW_TILE, d), lambda i, j: (i // tpb, jnp.maximum((i % tpb) * n_sub + r - 1, 0), 0))

        in_specs = [pl.BlockSpec((1, ROW_TILE, d), lambda i, j: (i // tpb, 0, 0))]
        in_specs += [lat_spec(r) for r in range(n_sub)]
        args = [ctx] + [x] * n_sub
    in_specs += [pl.BlockSpec((1, d), lambda i, j: (0, 0)), modc_spec, mod_spec,
                 pl.BlockSpec((d, IN_TILE), lambda i, j: (0, j))]
    args += [g, modc, mod, w_pad]
    out_shape = [jax.ShapeDtypeStruct((b, t, n), BF16),
                 jax.ShapeDtypeStruct((b, t, f_cols), F32),
                 jax.ShapeDtypeStruct((b, t, d), F32)]
    out_specs = [
        pl.BlockSpec((1, tm, IN_TILE), lambda i, j: (i // tpb, i % tpb, j)),
        row_spec,
        row_spec,
    ]
    return pl.pallas_call(
        functools.partial(_inproj_kernel, n_ctx=n_ctx, tiles_per_batch=tpb, residual=residual),
        out_shape=tuple(out_shape),
        grid=(b * tpb, n // IN_TILE),
        in_specs=in_specs,
        out_specs=tuple(out_specs),
        scratch_shapes=[pltpu.VMEM((tm, d), BF16)],
        compiler_params=_params(("arbitrary", "arbitrary")),
        name="in_proj",
    )(*args)


def _na_block_start(blk, rows):
    return jnp.clip(NA_QROWS * blk - WIN_ROWS // 2, 0, rows - NA_KROWS)


def _na_bias_table(rpb, rows):
    n_blk = rows // NA_QROWS
    assert rows % NA_QROWS == 0 and n_blk >= 3 and rows >= NA_KROWS and NA_KROWS % 2 == 0
    assert 2 * GRID_W == LANES
    qc = np.arange(GRID_W)[:, None]
    kc = np.arange(GRID_W)[None, :]
    c0 = np.clip(qc - WIN_COLS // 2, 0, GRID_W - WIN_COLS)
    col_ok = (kc >= c0) & (kc < c0 + WIN_COLS)
    dcol = np.clip(kc - qc + WIN_COLS - 1, 0, 2 * WIN_COLS - 2)
    col_sel = (np.arange(2 * WIN_COLS - 1)[:, None, None] == dcol[None]).astype(np.float32)
    per_row = jnp.einsum('hab,bqk->haqk', rpb.astype(F32), col_sel, precision=lax.Precision.HIGHEST)
    per_row = jnp.where(col_ok, per_row, MASK_NEG)
    per_row = jnp.concatenate([per_row, per_row], axis=-1)
    h, n_dr = per_row.shape[:2]
    return pl.pallas_call(
        functools.partial(_na_bias_kernel, rows=rows),
        out_shape=jax.ShapeDtypeStruct((h, 3, NA_QROWS * GRID_W, NA_KROWS * GRID_W), F32),
        grid=(h, 3),
        in_specs=[pl.BlockSpec((1, n_dr, GRID_W, LANES), lambda hi, p: (hi, 0, 0, 0))],
        out_specs=pl.BlockSpec((1, 1, NA_QROWS * GRID_W, NA_KROWS * GRID_W), lambda hi, p: (hi, p, 0, 0)),
        compiler_params=_params(("arbitrary", "arbitrary")),
        name="na_bias_table",
    )(per_row)


def _na_bias_kernel(t_ref, o_ref, *, rows):
    p = pl.program_id(1)
    n_blk = rows // NA_QROWS
    lo_half = lax.broadcasted_iota(jnp.int32, (GRID_W, LANES), 1) < GRID_W
    masked = jnp.full((GRID_W, LANES), MASK_NEG, F32)
    for pat, blk in enumerate((0, 1, n_blk - 1)):

        @pl.when(p == pat)
        def _():
            u0 = int(np.clip(NA_QROWS * blk - WIN_ROWS // 2, 0, rows - NA_KROWS))
            for j in range(NA_QROWS):
                r = NA_QROWS * blk + j
                r0 = int(np.clip(r - WIN_ROWS // 2, 0, rows - WIN_ROWS))
                for pair in range(NA_KROWS // 2):
                    halves = []
                    for i in (2 * pair, 2 * pair + 1):
                        krow = u0 + i
                        inside = r0 <= krow < r0 + WIN_ROWS
                        halves.append(t_ref[0, krow - r + WIN_ROWS - 1] if inside else masked)
                    o_ref[0, 0, j * GRID_W:(j + 1) * GRID_W, pair * LANES:(pair + 1) * LANES] = (
                        jnp.where(lo_half, halves[0], halves[1]))


def _na_kernel(q_ref, k_ref, v_ref, bias_ref, o_ref, *, n_ctx, rows):
    i = pl.program_id(1)
    tq = q_ref.shape[1]
    lane = lax.broadcasted_iota(jnp.int32, (tq, LANES), 1)
    lo_half = lane < NA_HEAD_DIM

    scale = NA_HEAD_DIM ** -0.5

    def pair_scores(hp, key_rows, with_bias):
        cols = slice(hp * LANES, (hp + 1) * LANES)
        q2 = q_ref[0, :, cols] * scale
        out = []
        for hh in range(2):
            keep = lo_half if hh == 0 else jnp.logical_not(lo_half)
            qm = jnp.where(keep, q2, jnp.zeros_like(q2))
            blocks = []
            for n, kr in enumerate(key_rows):
                s = _dot_nt(qm, k_ref[0, kr, cols])
                if with_bias and n == 0:
                    s = s + bias_ref[2 * hp + hh, 0]
                blocks.append(s)
            out.append(blocks)
        return out

    def pair_finish(hp, scores, key_rows):
        cols = slice(hp * LANES, (hp + 1) * LANES)
        outs = []
        for blocks in scores:
            m = functools.reduce(jnp.maximum, [jnp.max(s, axis=-1, keepdims=True) for s in blocks])
            den = 0.0
            acc = 0.0
            for s, kr in zip(blocks, key_rows):
                p = jnp.exp(s - m)
                den = den + jnp.sum(p, axis=-1, keepdims=True)
                acc = acc + _dot(p.astype(BF16), v_ref[0, kr, cols])
            outs.append(acc / den)
        o_ref[0, :, cols] = jnp.where(lo_half, outs[0], outs[1]).astype(o_ref.dtype)

    def attend(key_rows, with_bias):
        n_pairs = NA_HEADS // 2
        nxt = pair_scores(0, key_rows, with_bias)
        for hp in range(n_pairs):
            cur = nxt
            if hp + 1 < n_pairs:
                nxt = pair_scores(hp + 1, key_rows, with_bias)
            pair_finish(hp, cur, key_rows)

    ctx_rows = slice(0, n_ctx)

    @pl.when(i == 0)
    def _():
        attend([ctx_rows], False)

    @pl.when(i > 0)
    def _():
        u0 = _na_block_start(i - 1, rows)
        local_rows = pl.ds(pl.multiple_of(n_ctx + u0 * GRID_W, GRID_W), NA_KROWS * GRID_W)
        attend([local_rows, ctx_rows], True)


def _na_attention(z, bias_tab, n_ctx):
    b, t, _ = z.shape
    rows = (t - n_ctx) // GRID_W
    tq = NA_QROWS * GRID_W
    assert n_ctx == tq
    n_blk = rows // NA_QROWS
    wk = NA_KROWS * GRID_W

    def pattern(i):
        return jnp.where(i <= 1, 0, jnp.where(i == n_blk, 2, 1))

    return pl.pallas_call(
        functools.partial(_na_kernel, n_ctx=n_ctx, rows=rows),
        out_shape=jax.ShapeDtypeStruct((b, t, NA_WIDTH), BF16),
        grid=(b, 1 + n_blk),
        in_specs=[
            pl.BlockSpec((1, tq, NA_WIDTH), lambda bi, i: (bi, i, COL_NA_Q // NA_WIDTH)),
            pl.BlockSpec((1, t, NA_WIDTH), lambda bi, i: (bi, 0, COL_NA_K // NA_WIDTH)),
            pl.BlockSpec((1, t, NA_WIDTH), lambda bi, i: (bi, 0, COL_NA_V // NA_WIDTH)),
            pl.BlockSpec((NA_HEADS, 1, tq, wk), lambda bi, i: (0, pattern(i), 0, 0)),
        ],
        out_specs=pl.BlockSpec((1, tq, NA_WIDTH), lambda bi, i: (bi, i, 0)),
        compiler_params=_params(("arbitrary", "arbitrary")),
        name="na_attention",
    )(z, z, z, bias_tab)


def _rope_tables(n_tokens):
    t = jnp.arange(n_tokens)
    pos = jnp.stack([t // GRID_W, t % GRID_W], axis=-1).astype(F32)
    n_freq = GQA_HEAD_DIM // 4
    inv_freq = jnp.power(ROPE_THETA, -jnp.arange(n_freq, dtype=F32) / n_freq)
    ang = pos[:, :, None] * inv_freq
    ang = jnp.concatenate([ang, ang], axis=-1).reshape(n_tokens, GQA_HEAD_DIM)
    cos, sin = jnp.cos(ang), jnp.sin(ang)
    first = (np.arange(GQA_HEAD_DIM) % (2 * n_freq)) < n_freq
    sin_a = jnp.where(first, -sin, 0.0)
    sin_b = jnp.where(first, 0.0, sin)
    rep = LANES // GQA_HEAD_DIM
    return tuple(jnp.tile(a, (1, rep)) for a in (cos, sin_a, sin_b))


def _gqa_kernel(q_ref, k_ref, v_ref, cos_ref, sa_ref, sb_ref, gq_ref, gk_ref, o_ref,
                kk_scr, vv_scr, *, n_ctx):
    j = pl.program_id(1)
    t = k_ref.shape[1]
    tq = q_ref.shape[1]
    hd = GQA_HEAD_DIM
    quarter = hd // 4

    r_i = lax.broadcasted_iota(jnp.int32, (LANES, LANES), 0)
    c_i = lax.broadcasted_iota(jnp.int32, (LANES, LANES), 1)
    head_ones = jnp.where((r_i // hd) == (c_i // hd), 1.0, 0.0).astype(BF16)
    lane = lax.broadcasted_iota(jnp.int32, (tq, LANES), 1)
    lo_half = lane < hd

    def head_rms(x, g):
        hi, lo = _split_bf16(x * x)
        ms = (_dot(hi, head_ones) + _dot(lo, head_ones)) * (1.0 / hd)
        return (x * lax.rsqrt(ms + RMS_EPS)) * g

    def rope(x, rows):
        return (x * cos_ref[rows, :]
                + pltpu.roll(x, LANES - quarter, 1) * sa_ref[rows, :]
                + pltpu.roll(x, quarter, 1) * sb_ref[rows, :])

    @pl.when(j == 0)
    def _():
        zero = jnp.zeros((tq, LANES), F32)
        for r in range(t // tq):
            rows = slice(r * tq, (r + 1) * tq)
            k = head_rms(k_ref[0, rows, :].astype(F32), gk_ref[...])
            if r * tq >= n_ctx:
                k = rope(k, slice(r * tq - n_ctx, (r + 1) * tq - n_ctx))
            v = v_ref[0, rows, :].astype(F32)
            k_sw = pltpu.roll(k, hd, 1)
            v_sw = pltpu.roll(v, hd, 1)
            v_lo_rest = jnp.where(lane == hd, 1.0, zero)
            v_hi_rest = jnp.where(lane == 0, 1.0, zero)
            for scr, a, a_sw, lo_rest, hi_rest in ((kk_scr, k, k_sw, zero, zero),
                                                   (vv_scr, v, v_sw, v_lo_rest, v_hi_rest)):
                scr[0, rows, :] = jnp.where(lo_half, a, lo_rest).astype(BF16)
                scr[1, rows, :] = jnp.where(lo_half, hi_rest, a_sw).astype(BF16)
                scr[2, rows, :] = jnp.where(lo_half, a_sw, lo_rest).astype(BF16)
                scr[3, rows, :] = jnp.where(lo_half, hi_rest, a).astype(BF16)

    scale = hd ** -0.5

    n_chunks = GQA_Q_WIDTH // LANES

    def tile(nk, rope_rows):
        def scores(c):
            qc = head_rms(q_ref[0, :, c * LANES:(c + 1) * LANES].astype(F32), gq_ref[...])
            if rope_rows is not None:
                qc = rope(qc, rope_rows)
            qc = (qc * scale).astype(BF16)
            grp = (2 * c) // (GQA_Q_HEADS // GQA_KV_HEADS)
            return [_dot_nt(qc, kk_scr[2 * grp + hh, 0:nk, :]) for hh in range(2)]

        ahead = 2
        queue = [scores(c) for c in range(min(ahead, n_chunks))]
        for c in range(n_chunks):
            s_pair = queue.pop(0)
            if c + ahead < n_chunks:
                queue.append(scores(c + ahead))
            grp = (2 * c) // (GQA_Q_HEADS // GQA_KV_HEADS)
            outs = []
            for hh in range(2):
                s = s_pair[hh]
                m = jnp.max(s, axis=-1, keepdims=True)
                p = jnp.exp((s - m).astype(BF16))
                o = _dot(p, vv_scr[2 * grp + hh, 0:nk, :])
                sum_lane = hd if hh == 0 else 0
                den = jnp.sum(jnp.where(lane == sum_lane, o, 0.0), axis=-1, keepdims=True)
                outs.append(o / den)
            o_ref[0, :, c * LANES:(c + 1) * LANES] = jnp.where(lo_half, outs[0], outs[1]).astype(o_ref.dtype)

    @pl.when(j == 0)
    def _():
        tile(n_ctx, None)

    @pl.when(j > 0)
    def _():
        tile(t, pl.ds(pl.multiple_of((j - 1) * tq, tq), tq))


def _gqa_attention(z, tables, gq, gk, n_ctx):
    b, t, _ = z.shape
    tq = ROW_TILE
    assert n_ctx == tq
    n_lat = t - n_ctx
    cos, sa, sb = tables
    tab_spec = pl.BlockSpec((n_lat, LANES), lambda bi, j: (0, 0))
    g_spec = pl.BlockSpec((1, LANES), lambda bi, j: (0, 0))
    return pl.pallas_call(
        functools.partial(_gqa_kernel, n_ctx=n_ctx),
        out_shape=jax.ShapeDtypeStruct((b, t, GQA_Q_WIDTH), BF16),
        grid=(b, t // tq),
        in_specs=[
            pl.BlockSpec((1, tq, GQA_Q_WIDTH), lambda bi, j: (bi, j, COL_GQA_Q // GQA_Q_WIDTH)),
            pl.BlockSpec((1, t, GQA_KV_WIDTH), lambda bi, j: (bi, 0, COL_GQA_K // GQA_KV_WIDTH)),
            pl.BlockSpec((1, t, GQA_KV_WIDTH), lambda bi, j: (bi, 0, COL_GQA_V // GQA_KV_WIDTH)),
            tab_spec, tab_spec, tab_spec, g_spec, g_spec,
        ],
        out_specs=pl.BlockSpec((1, tq, GQA_Q_WIDTH), lambda bi, j: (bi, j, 0)),
        scratch_shapes=[pltpu.VMEM((4, t, LANES), BF16), pltpu.VMEM((4, t, LANES), BF16)],
        compiler_params=_params(("arbitrary", "arbitrary")),
        name="gqa_attention",
    )(z, z, z, cos, sa, sb, gq, gk)


HG_LEVELS = 6


def _hgrn_sum_matrices():
    c = HG_CHUNK
    u = np.arange(c)[None, :]
    r = np.arange(c)[:, None]
    fw, bw = [], []
    for lvl in range(HG_LEVELS):
        hs = c >> (lvl + 1)
        blk = (r // (2 * hs)) * (2 * hs)
        upper = (r % (2 * hs)) >= hs
        last_lower = blk + hs - 1
        first_upper = blk + hs
        fw.append(np.where(upper, (u > last_lower) & (u <= r), (u > r) & (u <= last_lower)))
        bw.append(np.where(upper, (u >= first_upper) & (u < r), (u >= r) & (u < first_upper)))
    fw += [u <= r, u > r]
    bw += [u >= r, u < r]

    def twice(parts):
        w = np.concatenate(parts).astype(np.float32)
        return jnp.asarray(np.concatenate([w, w], axis=1), BF16)

    return twice(fw), twice(bw)


def _hgrn_pair_masks():
    c = HG_CHUNK
    row = np.arange(c)[:, None]
    col = np.arange(c)[None, :]
    fw, bw = [], []
    for lvl in range(HG_LEVELS):
        hs = c >> (lvl + 1)
        same = (row // (2 * hs)) == (col // (2 * hs))
        row_up = (row % (2 * hs)) >= hs
        col_up = (col % (2 * hs)) >= hs
        fw.append(same & row_up & ~col_up)
        bw.append(same & ~row_up & col_up)
    fw.append(row == col)
    bw.append(row == col)

    def in_lane_halves(masks):
        m = np.stack(masks).astype(np.float32)
        z = np.zeros_like(m)
        return jnp.asarray(np.stack([np.concatenate([m, z], axis=-1),
                                     np.concatenate([z, m], axis=-1)], axis=1))

    assert 2 * c == LANES
    return in_lane_halves(fw), in_lane_halves(bw)


def _hgrn_kernel(q_ref, ff_ref, fb_ref, i_ref, g_ref, lbl_ref, ng_ref, wf_ref, wb_ref, mf_ref, mb_ref,
                 o_ref, o_scr, st_scr, *, layer, n_ctx):
    t = q_ref.shape[1]
    c = HG_CHUNK
    dk = HG_DK
    n = t // c
    nc = n_ctx // c
    depth = lbl_ref.shape[0]

    def lower_bound(direction):
        logits = [lbl_ref[d, direction] for d in range(depth)]
        m = functools.reduce(jnp.maximum, logits)
        e = [jnp.exp(x - m) for x in logits]
        tot = functools.reduce(lambda a, b: a + b, e)
        p = [x / tot for x in e]
        cum = functools.reduce(lambda a, b: a + b, p[:layer + 1])
        return cum - p[0]

    def heads_on_rows(x):
        return jnp.concatenate([x[:, h * dk:(h + 1) * dk] for h in range(HG_HEADS)], axis=0)

    def stream(off, f_ref, lb, w_ref, m_ref, exit_row, slot):
        rows = pl.ds(off, c)
        f = lb + (1.0 - lb) * jax.nn.sigmoid(f_ref[0, rows, :])
        kk = 1.0 - f
        decay = jnp.exp(_dot(w_ref[...], jnp.concatenate(_split_bf16(jnp.log(f)), axis=0)))
        q = _silu(q_ref[0, rows, :].astype(F32))
        v = i_ref[0, rows, :]
        a2 = [0.0] * HG_HEADS
        for lvl in range(HG_LEVELS + 1):
            if lvl < HG_LEVELS:
                y = decay[lvl * c:(lvl + 1) * c, :]
                qs, ks = heads_on_rows((q * y).astype(BF16)), heads_on_rows((kk * y).astype(BF16))
            else:
                qs, ks = heads_on_rows(q.astype(BF16)), heads_on_rows(kk.astype(BF16))
            p = _dot_nt(qs, ks)
            for h in range(HG_HEADS):
                tile = (h * c) // LANES
                slab = p[h * c:(h + 1) * c, tile * LANES:(tile + 1) * LANES]
                a2[h] = a2[h] + m_ref[lvl, h % 2] * slab
        e_cum = decay[HG_LEVELS * c:(HG_LEVELS + 1) * c, :]
        e_rest = decay[(HG_LEVELS + 1) * c:(HG_LEVELS + 2) * c, :]
        qe = (q * e_cum).astype(BF16)
        kd = (kk * e_rest).astype(BF16)
        outs = []
        for h in range(HG_HEADS):
            cols = slice(h * dk, (h + 1) * dk)
            st = st_scr[slot, cols, :]
            v_h = v[:, cols]
            v2 = jnp.concatenate([v_h, v_h], axis=0)
            outs.append(_dot(a2[h].astype(BF16), v2) + _dot_nt(qe[:, cols], st.astype(BF16)))
            st_scr[slot, cols, :] = st * e_cum[exit_row:exit_row + 1, cols] + _dot_tn(v_h, kd[:, cols])
        o_scr[rows, :] += jnp.concatenate(outs, axis=1)

    lb_f = lower_bound(0)
    lb_b = lower_bound(1)
    st_scr[...] = jnp.zeros_like(st_scr)
    o_scr[...] = jnp.zeros_like(o_scr)

    unroll = 4
    assert n % unroll == 0

    def body(k2, carry):
        for u in range(unroll):
            k = k2 * unroll + u
            off_f = pl.multiple_of(k * c, c)
            kb = jnp.where(k < nc, nc - 1 - k, n + nc - 1 - k)
            off_b = pl.multiple_of(kb * c, c)
            stream(off_f, ff_ref, lb_f, wf_ref, mf_ref, c - 1, 0)
            stream(off_b, fb_ref, lb_b, wb_ref, mb_ref, 0, 1)
        return carry

    lax.fori_loop(0, n // unroll, body, 0)

    ng = ng_ref[...]

    def readout(r, carry):
        rows = pl.ds(pl.multiple_of(r * ROW_TILE, ROW_TILE), ROW_TILE)
        for h in range(HG_HEADS):
            cols = slice(h * dk, (h + 1) * dk)
            y = (_rms(o_scr[rows, cols]) * ng) * _silu(g_ref[0, rows, cols].astype(F32))
            o_ref[0, rows, cols] = y.astype(o_ref.dtype)
        return carry

    lax.fori_loop(0, t // ROW_TILE, readout, 0)


def _hgrn(z, zf, lb_logits, norm_g, layer, n_ctx):
    b, t, _ = z.shape
    depth = lb_logits.shape[0]
    w = HG_WIDTH
    lbl = lb_logits.astype(F32).reshape(depth, 2, 1, w)
    wf, wb = _hgrn_sum_matrices()
    mf, mb = _hgrn_pair_masks()

    def zcol(base):
        return pl.BlockSpec((1, t, w), lambda bi: (bi, 0, base // w))

    def whole(shape):
        return pl.BlockSpec(shape, lambda bi: (0,) * len(shape))

    return pl.pallas_call(
        functools.partial(_hgrn_kernel, layer=layer, n_ctx=n_ctx),
        out_shape=jax.ShapeDtypeStruct((b, t, w), BF16),
        grid=(b,),
        in_specs=[
            zcol(COL_HG_Q),
            pl.BlockSpec((1, t, w), lambda bi: (bi, 0, 0)),
            pl.BlockSpec((1, t, w), lambda bi: (bi, 0, 1)),
            zcol(COL_HG_I),
            zcol(COL_HG_G),
            whole(lbl.shape), whole((1, HG_DK)), whole(wf.shape), whole(wb.shape),
            whole(mf.shape), whole(mb.shape),
        ],
        out_specs=pl.BlockSpec((1, t, w), lambda bi: (bi, 0, 0)),
        scratch_shapes=[pltpu.VMEM((t, w), F32), pltpu.VMEM((2, w, HG_DK), F32)],
        compiler_params=_params(("arbitrary",)),
        name="hgrn2_scan",
    )(z, zf, zf, z, z, lbl, norm_g.reshape(1, HG_DK).astype(F32), wf, wb, mf, mb)


def _merge_kernel(s_ref, a_ref, b_ref, c_ref, ga_ref, gb_ref, gc_ref, mod_ref, modc_ref,
                  wb_ref, wo_ref, gf_ref, wr_ref, br_ref, s_out, f_out, route_out):
    first = pl.program_id(1) == 0
    tm = s_ref.shape[1]

    y = 0.0
    for idx, (br, gr) in enumerate(((a_ref, ga_ref), (b_ref, gb_ref), (c_ref, gc_ref))):
        y = y + jax.nn.sigmoid(gr[0].astype(F32)) * _dot(br[0], wb_ref[idx])
    proj = _dot(y.astype(BF16), wo_ref[...])
    w_hi, w_lo = _split_bf16(wr_ref[...])
    for r in range(tm // ROW_TILE):
        rows = slice(r * ROW_TILE, (r + 1) * ROW_TILE)
        is_ctx = jnp.logical_and(first, r == 0)
        _merge_rows(s_ref, proj[rows, :], rows, r, is_ctx, mod_ref, modc_ref, gf_ref, w_hi, w_lo, br_ref,
                    s_out, f_out, route_out)


def _merge_rows(s_ref, proj, rows, r, is_ctx, mod_ref, modc_ref, gf_ref, w_hi, w_lo, br_ref,
                s_out, f_out, route_out):
    def mod_row(k):
        return jnp.where(is_ctx, modc_ref[k:k + 1, :], mod_ref[0, k:k + 1, :])

    x = s_ref[0, rows, :] + mod_row(2) * proj
    s_out[0, rows, :] = x
    f = (_rms(x) * gf_ref[...]) * (1.0 + mod_row(4)) + mod_row(3)
    _rows_to_tiles(f_out, r * ROW_TILE, f)

    f_hi, f_lo = _split_bf16(f)
    logits = _dot(f_hi, w_hi) + _dot(f_lo, w_hi) + _dot(f_hi, w_lo) + br_ref[...]

    lane = lax.broadcasted_iota(jnp.int32, logits.shape, 1)
    is_group = jnp.logical_and(lane >= N_EXPERTS, lane < N_EXPERTS + N_GROUPS)

    def first_argmax(x, x_max):
        return jnp.min(jnp.where(x == x_max, lane, LANES), axis=-1, keepdims=True)

    gl = jnp.where(is_group, logits, MASK_NEG)
    g_max = jnp.max(gl, axis=-1, keepdims=True)
    g_idx = first_argmax(gl, g_max) - N_EXPERTS
    p_group = 1.0 / jnp.sum(jnp.exp(gl - g_max), axis=-1, keepdims=True)
    in_group = jnp.logical_and(lane < N_EXPERTS, (lane // EXPERTS_PER_GROUP) == g_idx)
    e1 = jnp.where(in_group, logits, MASK_NEG)
    v1 = jnp.max(e1, axis=-1, keepdims=True)
    i1 = first_argmax(e1, v1)
    e2 = jnp.where(lane == i1, MASK_NEG, e1)
    v2 = jnp.max(e2, axis=-1, keepdims=True)
    i2 = first_argmax(e2, v2)
    r21 = jnp.exp(v2 - v1)
    w1 = 1.0 / (1.0 + r21)
    w2 = r21 * w1
    route = jnp.where(lane == 0, i1.astype(F32), 0.0)
    route = jnp.where(lane == 1, i2.astype(F32), route)
    route = jnp.where(lane == 2, w1 * p_group, route)
    route = jnp.where(lane == 3, w2 * p_group, route)
    route_out[0, rows, :] = route


def _merge(s, a, bb, cc, z, mod, modc, wb, wo, gf, wr, br):
    b, t, d = s.shape
    tm = WIDE_ROWS
    bw = a.shape[-1]
    gate0 = COL_GATES // d

    def rows(width, colblk=0):
        return pl.BlockSpec((1, tm, width), lambda bi, j: (bi, j, colblk))

    def whole(shape):
        return pl.BlockSpec(shape, lambda bi, j: (0,) * len(shape))

    return pl.pallas_call(
        _merge_kernel,
        out_shape=(jax.ShapeDtypeStruct((b, t, d), F32),
                   jax.ShapeDtypeStruct((b, t * SUBLANES, LANES), F32),
                   jax.ShapeDtypeStruct((b, t, LANES), F32)),
        grid=(b, t // tm),
        in_specs=[
            rows(d), rows(bw), rows(bw), rows(bw),
            rows(d, gate0), rows(d, gate0 + 1), rows(d, gate0 + 2),
            pl.BlockSpec((1, N_MOD, d), lambda bi, j: (bi, 0, 0)),
            whole((N_MOD, d)),
            whole(wb.shape), whole(wo.shape), whole((1, d)), whole(wr.shape), whole((1, LANES)),
        ],
        out_specs=(rows(d), pl.BlockSpec((1, tm * SUBLANES, LANES), lambda bi, j: (bi, j, 0)),
                   rows(LANES)),
        compiler_params=_params(("arbitrary", "arbitrary")),
        name="merge_router",
    )(s, a, bb, cc, z, z, z, mod, modc, wb, wo, gf, wr, br)


MOE_CHUNK = 320
MOE_PAD_TOKENS = SUBLANES


def _moe_kernel(tok_ref, wgt_ref, cnt_ref, off_ref, f_ref, wg_ref, wu_ref, wd_ref, y_ref,
                xs_scr, ys_scr):
    e = pl.program_id(1)
    sub = SUBLANES
    n_col = f_ref.shape[2]
    t_dummy = f_ref.shape[1]

    @pl.when(e == 0)
    def _():
        y_ref[...] = jnp.zeros_like(y_ref)
        xs_scr[...] = jnp.zeros_like(xs_scr)

    cnt = cnt_ref[0, 0, e]
    off = off_ref[0, 0, e]
    wg = wg_ref[0]
    wu = wu_ref[0]
    wd = wd_ref[0]

    def chunk(ci, carry):
        base = off + ci * MOE_CHUNK
        m = jnp.minimum(MOE_CHUNK, cnt - ci * MOE_CHUNK)
        n_grp = (m + sub - 1) // sub

        def gather(gi, c2):
            for u in range(sub):
                r = gi * sub + u
                tok = tok_ref[0, 0, base + r]
                xs_scr[pl.ds(pl.multiple_of(r * n_col, n_col), n_col), :] = f_ref[0, tok]
            return c2

        lax.fori_loop(0, n_grp, gather, 0)
        x = jnp.concatenate(
            [xs_scr[pl.ds(s, MOE_CHUNK, stride=n_col), :] for s in range(n_col)], axis=1).astype(BF16)
        he = (_silu(_dot(x, wg)) * _dot(x, wu)).astype(BF16)
        y = _dot(he, wd)
        for s in range(n_col):
            ys_scr[pl.ds(s, MOE_CHUNK, stride=n_col), :] = y[:, s * LANES:(s + 1) * LANES]

        def scatter(gi, c2):
            toks, vals = [], []
            for u in range(sub):
                r = gi * sub + u
                tok = jnp.where(r < m, tok_ref[0, 0, base + r], t_dummy)
                contrib = wgt_ref[0, 0, base + r] * ys_scr[pl.ds(pl.multiple_of(r * n_col, n_col), n_col), :]
                toks.append(tok)
                vals.append(y_ref[0, tok] + contrib)
            for tok, val in zip(toks, vals):
                y_ref[0, tok] = val
            return c2

        lax.fori_loop(0, n_grp, scatter, 0)
        return carry

    lax.fori_loop(0, (cnt + MOE_CHUNK - 1) // MOE_CHUNK, chunk, 0)


def _moe(f_tiles, route, w_gate, w_up, w_down):
    b, t, _ = route.shape
    d, hid = w_gate.shape[-2:]
    n_col = d // LANES
    assert n_col == SUBLANES and f_tiles.shape == (b, t * n_col, LANES)
    n_slot = TOP_K * t

    eid = route[:, :, 0:TOP_K].astype(jnp.int32).reshape(b, n_slot)
    wts = route[:, :, TOP_K:2 * TOP_K].reshape(b, n_slot)
    perm = jnp.argsort(eid, axis=1).astype(jnp.int32)
    pad = jnp.zeros((b, MOE_CHUNK), jnp.int32)
    tok_sorted = jnp.concatenate([perm // TOP_K, pad], axis=1).reshape(b, 1, n_slot + MOE_CHUNK)
    wgt_sorted = jnp.concatenate([jnp.take_along_axis(wts, perm, axis=1), pad.astype(F32)],
                                 axis=1).reshape(b, 1, n_slot + MOE_CHUNK)
    counts = jnp.sum(eid[:, :, None] == jnp.arange(N_EXPERTS)[None, None, :], axis=1).astype(jnp.int32)
    offs = (jnp.cumsum(counts, axis=1) - counts).astype(jnp.int32)
    counts = counts.reshape(b, 1, N_EXPERTS)
    offs = offs.reshape(b, 1, N_EXPERTS)

    def smem(n):
        return pl.BlockSpec((1, 1, n), lambda bi, e: (bi, 0, 0), memory_space=pltpu.SMEM)

    t_out = t + MOE_PAD_TOKENS
    y4 = pl.pallas_call(
        _moe_kernel,
        out_shape=jax.ShapeDtypeStruct((b, t_out, n_col, LANES), F32),
        grid=(b, N_EXPERTS),
        in_specs=[
            smem(n_slot + MOE_CHUNK), smem(n_slot + MOE_CHUNK), smem(N_EXPERTS), smem(N_EXPERTS),
            pl.BlockSpec((1, t, n_col, LANES), lambda bi, e: (bi, 0, 0, 0),
                         pipeline_mode=pl.Buffered(1)),
            pl.BlockSpec((1, d, hid), lambda bi, e: (e, 0, 0)),
            pl.BlockSpec((1, d, hid), lambda bi, e: (e, 0, 0)),
            pl.BlockSpec((1, hid, d), lambda bi, e: (e, 0, 0)),
        ],
        out_specs=pl.BlockSpec((1, t_out, n_col, LANES), lambda bi, e: (bi, 0, 0, 0)),
        scratch_shapes=[pltpu.VMEM((MOE_CHUNK * n_col, LANES), F32),
                        pltpu.VMEM((MOE_CHUNK * n_col, LANES), F32)],
        compiler_params=_params(("arbitrary", "arbitrary")),
        name="moe_experts",
    )(tok_sorted, wgt_sorted, counts, offs, f_tiles.reshape(b, t, n_col, LANES), w_gate, w_up, w_down)
    return y4.reshape(b, t_out * n_col, LANES)


def _final_kernel(s_ref, y_ref, mod_ref, g_ref, o_ref):
    x = s_ref[0] + mod_ref[0, 5:6, :] * _tiles_to_rows(y_ref, 0, s_ref.shape[1])
    o_ref[0] = _rms(x) * g_ref[...]


def _final_norm(s, y, mod, g, n_ctx):
    b, t, d = s.shape
    tm = ROW_TILE
    skip = n_ctx // tm
    lat_rows = pl.BlockSpec((1, tm, d), lambda bi, j: (bi, j + skip, 0))
    return pl.pallas_call(
        _final_kernel,
        out_shape=jax.ShapeDtypeStruct((b, t - n_ctx, d), F32),
        grid=(b, (t - n_ctx) // tm),
        in_specs=[lat_rows,
                  pl.BlockSpec((1, tm * SUBLANES, LANES), lambda bi, j: (bi, j + skip, 0)),
                  pl.BlockSpec((1, N_MOD, d), lambda bi, j: (bi, 0, 0)),
                  pl.BlockSpec((1, d), lambda bi, j: (0, 0))],
        out_specs=pl.BlockSpec((1, tm, d), lambda bi, j: (bi, j, 0)),
        compiler_params=_params(("arbitrary", "arbitrary")),
        name="final_norm",
    )(s, y, mod, g)


def kernel(x, c, ctx, c_ctx, w_ada, b_ada, g_mix, g_ffn, w_in, na_rpb, hg_lb_logits, hg_norm_g,
           gqa_qnorm_g, gqa_knorm_g, w_branch, w_out, w_group_router, b_group_router,
           w_expert_router, b_expert_router, w_exp_gate, w_exp_up, w_exp_down, g_final):
    b, n_lat, d = x.shape
    n_ctx = ctx.shape[1]
    depth = w_in.shape[0]
    assert n_ctx == ROW_TILE and n_lat % ROW_TILE == 0 and (n_ctx + n_lat) % WIDE_ROWS == 0
    rows = n_lat // GRID_W

    s = (ctx, x)

    c_rows = 16
    c_all = jnp.concatenate([c, c_ctx[None, :], jnp.zeros((c_rows - b - 1, d), c.dtype)], axis=0)
    mod_all = _ada(c_all, w_ada, b_ada).reshape(depth, c_rows, N_MOD, d)

    tables = _rope_tables(n_lat)
    rep = LANES // GQA_HEAD_DIM

    prev = None
    for l in range(depth):
        mod = mod_all[l, :b]
        modc = mod_all[l, b]
        w_pad = jnp.concatenate(
            [w_in[l, :, :COL_RAW_GATES],
             jnp.zeros((d, COL_GATES - COL_RAW_GATES), w_in.dtype),
             w_in[l, :, COL_RAW_GATES:]], axis=1).astype(BF16)
        z, zf, s = _inproj(s, g_mix[l].reshape(1, d), modc, mod, w_pad, n_ctx, prev)

        a = _na_attention(z, _na_bias_table(na_rpb[l], rows), n_ctx)
        cc = _gqa_attention(z, tables,
                            jnp.tile(gqa_qnorm_g[l].astype(F32), rep).reshape(1, LANES),
                            jnp.tile(gqa_knorm_g[l].astype(F32), rep).reshape(1, LANES), n_ctx)
        bb = _hgrn(z, zf, hg_lb_logits, hg_norm_g[l], l, n_ctx)

        wr = jnp.concatenate(
            [w_expert_router[l], w_group_router[l],
             jnp.zeros((d, LANES - N_EXPERTS - N_GROUPS), F32)], axis=1)
        br = jnp.concatenate(
            [b_expert_router[l], b_group_router[l],
             jnp.zeros((LANES - N_EXPERTS - N_GROUPS,), F32)]).reshape(1, LANES)
        s, f, route = _merge(s, a, bb, cc, z, mod, modc, w_branch[l].astype(BF16),
                             w_out[l].astype(BF16), g_ffn[l].reshape(1, d), wr, br)
        y = _moe(f, route, w_exp_gate[l].astype(BF16), w_exp_up[l].astype(BF16),
                 w_exp_down[l].astype(BF16))
        prev = (y, mod, modc)

    return _final_norm(s, prev[0], prev[1], g_final.reshape(1, d), n_ctx)
```

```python
import functools

import numpy as np
import jax
import jax.numpy as jnp
from jax import lax
from jax.experimental import pallas as pl
from jax.experimental.pallas import tpu as pltpu

F32 = jnp.float32
BF16 = jnp.bfloat16

RMS_EPS = 1e-6
N_MOD = 6
GRID_W = 64

NA_HEADS = 8
NA_HEAD_DIM = 64
NA_WIDTH = NA_HEADS * NA_HEAD_DIM
WIN_ROWS = 8
WIN_COLS = 16
NA_QROWS = 4
NA_KROWS = 12

HG_HEADS = 4
HG_DK = 128
HG_WIDTH = HG_HEADS * HG_DK
HG_CHUNK = 64

GQA_Q_HEADS = 8
GQA_KV_HEADS = 2
GQA_HEAD_DIM = 64
GQA_Q_WIDTH = GQA_Q_HEADS * GQA_HEAD_DIM
GQA_KV_WIDTH = GQA_KV_HEADS * GQA_HEAD_DIM
ROPE_THETA = 10000.0

N_GROUPS = 4
EXPERTS_PER_GROUP = 4
N_EXPERTS = N_GROUPS * EXPERTS_PER_GROUP
TOP_K = 2

LANES = 128
SUBLANES = 8
ROW_TILE = 256
WIDE_ROWS = 768
MASK_NEG = -1e30

COL_NA_Q = 0
COL_NA_K = 512
COL_NA_V = 1024
COL_HG_Q = 1536
COL_HG_FF = 2048
COL_HG_FB = 2560
COL_HG_I = 3072
COL_HG_G = 3584
COL_GQA_Q = 4096
COL_GQA_K = 4608
COL_GQA_V = 4736
COL_RAW_GATES = 4864
COL_GATES = 5120
IN_COLS_PAD = 8192
IN_TILE = 2048

VMEM_LIMIT = 56 * 1024 * 1024


def _dot(a, b):
    return jnp.dot(a, b, preferred_element_type=F32)


def _dot_nt(a, b):
    return lax.dot_general(a, b, (((1,), (1,)), ((), ())), preferred_element_type=F32)


def _dot_tn(a, b):
    return lax.dot_general(a, b, (((0,), (0,)), ((), ())), preferred_element_type=F32)


def _split_bf16(x):
    hi = x.astype(BF16)
    lo = (x - hi.astype(F32)).astype(BF16)
    return hi, lo


def _silu(x):
    return x * jax.nn.sigmoid(x)


def _rms(x):
    return x * lax.rsqrt(jnp.mean(x * x, axis=-1, keepdims=True) + RMS_EPS)


def _tiles_to_rows(ref, tok0, n):
    return jnp.concatenate(
        [ref[0, pl.ds(tok0 * SUBLANES + s, n, stride=SUBLANES), :] for s in range(SUBLANES)], axis=1)


def _rows_to_tiles(ref, tok0, x):
    n = x.shape[0]
    for s in range(SUBLANES):
        ref[0, pl.ds(tok0 * SUBLANES + s, n, stride=SUBLANES), :] = x[:, s * LANES:(s + 1) * LANES]


def _params(semantics, vmem=VMEM_LIMIT):
    return pltpu.CompilerParams(dimension_semantics=semantics, vmem_limit_bytes=vmem)


def _ada_kernel(c_ref, w_ref, b_ref, o_ref):
    sc = _silu(c_ref[...]).astype(BF16)
    o_ref[0] = _dot(sc, w_ref[0].astype(BF16)) + b_ref[0]


def _ada(c_all, w_ada, b_ada):
    depth, d, n = w_ada.shape
    rows = c_all.shape[0]
    tn = 1536
    return pl.pallas_call(
        _ada_kernel,
        out_shape=jax.ShapeDtypeStruct((depth, rows, n), F32),
        grid=(depth, n // tn),
        in_specs=[
            pl.BlockSpec((rows, d), lambda l, j: (0, 0)),
            pl.BlockSpec((1, d, tn), lambda l, j: (l, 0, j)),
            pl.BlockSpec((1, 1, tn), lambda l, j: (l, 0, j)),
        ],
        out_specs=pl.BlockSpec((1, rows, tn), lambda l, j: (l, 0, j)),
        compiler_params=_params(("arbitrary", "arbitrary")),
        name="ada_mod",
    )(c_all, w_ada, b_ada.reshape(depth, 1, n))


def _inproj_kernel(*refs, n_ctx, tiles_per_batch, residual):
    n_sub = WIDE_ROWS // ROW_TILE
    if residual:
        (s_ref, y_ref, modp_ref, modcp_ref, g_ref, modc_ref, mod_ref, w_ref,
         z_ref, zf_ref, s_out, h_scr) = refs
    else:
        ctx_ref, *lat_refs = refs[:1 + n_sub]
        g_ref, modc_ref, mod_ref, w_ref, z_ref, zf_ref, s_out, h_scr = refs[1 + n_sub:]
    i = pl.program_id(0)
    j = pl.program_id(1)

    @pl.when(j == 0)
    def _():
        g = g_ref[...]
        first = (i % tiles_per_batch) == 0
        for r in range(n_sub):
            rows = slice(r * ROW_TILE, (r + 1) * ROW_TILE)
            is_ctx = jnp.logical_and(first, r * ROW_TILE < n_ctx)

            def pick(ctx_ref, lat_ref, k):
                return jnp.where(is_ctx, ctx_ref[k:k + 1, :], lat_ref[0, k:k + 1, :])

            if residual:
                x = s_ref[0, rows, :]
                x = x + pick(modcp_ref, modp_ref, 5) * _tiles_to_rows(y_ref, r * ROW_TILE, ROW_TILE)
            elif r == 0:
                x = jnp.where(is_ctx, ctx_ref[0], lat_refs[0][0])
            else:
                x = lat_refs[r][0]
            s_out[0, rows, :] = x
            h = (_rms(x) * g) * (1.0 + pick(modc_ref, mod_ref, 1)) + pick(modc_ref, mod_ref, 0)
            h_scr[rows, :] = h.astype(BF16)

    acc = _dot(h_scr[...], w_ref[0])
    z_ref[0] = acc.astype(BF16)

    @pl.when(j == COL_HG_FF // IN_TILE)
    def _():
        lo = COL_HG_FF % IN_TILE
        zf_ref[0] = acc[:, lo:lo + zf_ref.shape[2]]


def _inproj(stream, g, modc, mod, w_pad, layer, n_ctx, prev=None):
    residual = prev is not None
    if residual:
        b, t, d = stream.shape
    else:
        ctx, x = stream
        b, n_lat, d = x.shape
        t = n_ctx + n_lat
        assert ctx.shape == (b, n_ctx, d) and n_ctx == ROW_TILE
    n = w_pad.shape[2]
    tm = WIDE_ROWS
    tpb = t // tm
    n_sub = tm // ROW_TILE
    f_cols = 2 * HG_WIDTH
    assert COL_HG_FB == COL_HG_FF + HG_WIDTH and d == f_cols
    assert COL_HG_FF // IN_TILE == (COL_HG_FF + f_cols - 1) // IN_TILE

    row_spec = pl.BlockSpec((1, tm, d), lambda i, j: (i // tpb, i % tpb, 0))
    mod_spec = pl.BlockSpec((1, N_MOD, d), lambda i, j: (i // tpb, 0, 0))
    modc_spec = pl.BlockSpec((N_MOD, d), lambda i, j: (0, 0))
    if residual:
        y_spec = pl.BlockSpec((1, tm * SUBLANES, LANES), lambda i, j: (i // tpb, i % tpb, 0))
        in_specs = [row_spec, y_spec, mod_spec, modc_spec]
        args = [stream, prev[0], prev[1], prev[2]]
    else:
        def lat_spec(r):
            return pl.BlockSpec(
                (1, ROW_TILE, d), lambda i, j: (i // tpb, jnp.maximum((i % tpb) * n_sub + r - 1, 0), 0))

        in_specs = [pl.BlockSpec((1, ROW_TILE, d), lambda i, j: (i // tpb, 0, 0))]
        in_specs += [lat_spec(r) for r in range(n_sub)]
        args = [ctx] + [x] * n_sub
    in_specs += [pl.BlockSpec((1, d), lambda i, j: (0, 0)), modc_spec, mod_spec,
                 pl.BlockSpec((1, d, IN_TILE), lambda i, j: (layer, 0, j))]
    args += [g, modc, mod, w_pad]
    out_shape = [jax.ShapeDtypeStruct((b, t, n), BF16),
                 jax.ShapeDtypeStruct((b, t, f_cols), F32),
                 jax.ShapeDtypeStruct((b, t, d), F32)]
    out_specs = [
        pl.BlockSpec((1, tm, IN_TILE), lambda i, j: (i // tpb, i % tpb, j)),
        row_spec,
        row_spec,
    ]
    return pl.pallas_call(
        functools.partial(_inproj_kernel, n_ctx=n_ctx, tiles_per_batch=tpb, residual=residual),
        out_shape=tuple(out_shape),
        grid=(b * tpb, n // IN_TILE),
        in_specs=in_specs,
        out_specs=tuple(out_specs),
        scratch_shapes=[pltpu.VMEM((tm, d), BF16)],
        compiler_params=_params(("arbitrary", "arbitrary")),
        name="in_proj",
    )(*args)


def _na_block_start(blk, rows):
    return jnp.clip(NA_QROWS * blk - WIN_ROWS // 2, 0, rows - NA_KROWS)


def _na_bias_table(rpb, rows):
    n_blk = rows // NA_QROWS
    assert rows % NA_QROWS == 0 and n_blk >= 3 and rows >= NA_KROWS and NA_KROWS % 2 == 0
    assert 2 * GRID_W == LANES
    qc = np.arange(GRID_W)[:, None]
    kc = np.arange(GRID_W)[None, :]
    c0 = np.clip(qc - WIN_COLS // 2, 0, GRID_W - WIN_COLS)
    col_ok = (kc >= c0) & (kc < c0 + WIN_COLS)
    dcol = np.clip(kc - qc + WIN_COLS - 1, 0, 2 * WIN_COLS - 2)
    col_sel = (np.arange(2 * WIN_COLS - 1)[:, None, None] == dcol[None]).astype(np.float32)
    per_row = jnp.einsum('hab,bqk->haqk', rpb.astype(F32), col_sel, precision=lax.Precision.HIGHEST)
    per_row = jnp.where(col_ok, per_row, MASK_NEG)
    per_row = jnp.concatenate([per_row, per_row], axis=-1)
    h, n_dr = per_row.shape[:2]
    return pl.pallas_call(
        functools.partial(_na_bias_kernel, rows=rows),
        out_shape=jax.ShapeDtypeStruct((h, 3, NA_QROWS * GRID_W, NA_KROWS * GRID_W), F32),
        grid=(h, 3),
        in_specs=[pl.BlockSpec((1, n_dr, GRID_W, LANES), lambda hi, p: (hi, 0, 0, 0))],
        out_specs=pl.BlockSpec((1, 1, NA_QROWS * GRID_W, NA_KROWS * GRID_W), lambda hi, p: (hi, p, 0, 0)),
        compiler_params=_params(("arbitrary", "arbitrary")),
        name="na_bias_table",
    )(per_row)


def _na_bias_kernel(t_ref, o_ref, *, rows):
    p = pl.program_id(1)
    n_blk = rows // NA_QROWS
    lo_half = lax.broadcasted_iota(jnp.int32, (GRID_W, LANES), 1) < GRID_W
    masked = jnp.full((GRID_W, LANES), MASK_NEG, F32)
    for pat, blk in enumerate((0, 1, n_blk - 1)):

        @pl.when(p == pat)
        def _():
            u0 = int(np.clip(NA_QROWS * blk - WIN_ROWS // 2, 0, rows - NA_KROWS))
            for j in range(NA_QROWS):
                r = NA_QROWS * blk + j
                r0 = int(np.clip(r - WIN_ROWS // 2, 0, rows - WIN_ROWS))
                for pair in range(NA_KROWS // 2):
                    halves = []
                    for i in (2 * pair, 2 * pair + 1):
                        krow = u0 + i
                        inside = r0 <= krow < r0 + WIN_ROWS
                        halves.append(t_ref[0, krow - r + WIN_ROWS - 1] if inside else masked)
                    o_ref[0, 0, j * GRID_W:(j + 1) * GRID_W, pair * LANES:(pair + 1) * LANES] = (
                        jnp.where(lo_half, halves[0], halves[1]))


def _na_kernel(q_ref, k_ref, v_ref, bias_ref, o_ref, *, n_ctx, rows):
    i = pl.program_id(1)
    tq = q_ref.shape[1]
    lane = lax.broadcasted_iota(jnp.int32, (tq, LANES), 1)
    lo_half = lane < NA_HEAD_DIM

    scale = NA_HEAD_DIM ** -0.5

    def pair_scores(hp, key_rows, with_bias):
        cols = slice(hp * LANES, (hp + 1) * LANES)
        q2 = q_ref[0, :, cols] * scale
        out = []
        for hh in range(2):
            keep = lo_half if hh == 0 else jnp.logical_not(lo_half)
            qm = jnp.where(keep, q2, jnp.zeros_like(q2))
            blocks = []
            for n, kr in enumerate(key_rows):
                s = _dot_nt(qm, k_ref[0, kr, cols])
                if with_bias and n == 0:
                    s = s + bias_ref[2 * hp + hh, 0]
                blocks.append(s)
            out.append(blocks)
        return out

    def pair_finish(hp, scores, key_rows):
        cols = slice(hp * LANES, (hp + 1) * LANES)
        outs = []
        for blocks in scores:
            m = functools.reduce(jnp.maximum, [jnp.max(s, axis=-1, keepdims=True) for s in blocks])
            den = 0.0
            acc = 0.0
            for s, kr in zip(blocks, key_rows):
                p = jnp.exp(s - m)
                den = den + jnp.sum(p, axis=-1, keepdims=True)
                acc = acc + _dot(p.astype(BF16), v_ref[0, kr, cols])
            outs.append(acc / den)
        o_ref[0, :, cols] = jnp.where(lo_half, outs[0], outs[1]).astype(o_ref.dtype)

    def attend(key_rows, with_bias):
        n_pairs = NA_HEADS // 2
        nxt = pair_scores(0, key_rows, with_bias)
        for hp in range(n_pairs):
            cur = nxt
            if hp + 1 < n_pairs:
                nxt = pair_scores(hp + 1, key_rows, with_bias)
            pair_finish(hp, cur, key_rows)

    ctx_rows = slice(0, n_ctx)

    @pl.when(i == 0)
    def _():
        attend([ctx_rows], False)

    @pl.when(i > 0)
    def _():
        u0 = _na_block_start(i - 1, rows)
        local_rows = pl.ds(pl.multiple_of(n_ctx + u0 * GRID_W, GRID_W), NA_KROWS * GRID_W)
        attend([local_rows, ctx_rows], True)


def _na_attention(z, bias_tab, n_ctx):
    b, t, _ = z.shape
    rows = (t - n_ctx) // GRID_W
    tq = NA_QROWS * GRID_W
    assert n_ctx == tq
    n_blk = rows // NA_QROWS
    wk = NA_KROWS * GRID_W

    def pattern(i):
        return jnp.where(i <= 1, 0, jnp.where(i == n_blk, 2, 1))

    return pl.pallas_call(
        functools.partial(_na_kernel, n_ctx=n_ctx, rows=rows),
        out_shape=jax.ShapeDtypeStruct((b, t, NA_WIDTH), BF16),
        grid=(b, 1 + n_blk),
        in_specs=[
            pl.BlockSpec((1, tq, NA_WIDTH), lambda bi, i: (bi, i, COL_NA_Q // NA_WIDTH)),
            pl.BlockSpec((1, t, NA_WIDTH), lambda bi, i: (bi, 0, COL_NA_K // NA_WIDTH)),
            pl.BlockSpec((1, t, NA_WIDTH), lambda bi, i: (bi, 0, COL_NA_V // NA_WIDTH)),
            pl.BlockSpec((NA_HEADS, 1, tq, wk), lambda bi, i: (0, pattern(i), 0, 0)),
        ],
        out_specs=pl.BlockSpec((1, tq, NA_WIDTH), lambda bi, i: (bi, i, 0)),
        compiler_params=_params(("arbitrary", "arbitrary")),
        name="na_attention",
    )(z, z, z, bias_tab)


def _rope_tables(n_tokens):
    t = jnp.arange(n_tokens)
    pos = jnp.stack([t // GRID_W, t % GRID_W], axis=-1).astype(F32)
    n_freq = GQA_HEAD_DIM // 4
    inv_freq = jnp.power(ROPE_THETA, -jnp.arange(n_freq, dtype=F32) / n_freq)
    ang = pos[:, :, None] * inv_freq
    ang = jnp.concatenate([ang, ang], axis=-1).reshape(n_tokens, GQA_HEAD_DIM)
    cos, sin = jnp.cos(ang), jnp.sin(ang)
    first = (np.arange(GQA_HEAD_DIM) % (2 * n_freq)) < n_freq
    sin_a = jnp.where(first, -sin, 0.0)
    sin_b = jnp.where(first, 0.0, sin)
    rep = LANES // GQA_HEAD_DIM
    return tuple(jnp.tile(a, (1, rep)) for a in (cos, sin_a, sin_b))


def _gqa_kernel(q_ref, k_ref, v_ref, cos_ref, sa_ref, sb_ref, gq_ref, gk_ref, o_ref,
                kk_scr, vv_scr, *, n_ctx):
    j = pl.program_id(1)
    t = k_ref.shape[1]
    tq = q_ref.shape[1]
    hd = GQA_HEAD_DIM
    quarter = hd // 4

    r_i = lax.broadcasted_iota(jnp.int32, (LANES, LANES), 0)
    c_i = lax.broadcasted_iota(jnp.int32, (LANES, LANES), 1)
    head_ones = jnp.where((r_i // hd) == (c_i // hd), 1.0, 0.0).astype(BF16)
    lane = lax.broadcasted_iota(jnp.int32, (tq, LANES), 1)
    lo_half = lane < hd

    def head_rms(x, g):
        hi, lo = _split_bf16(x * x)
        ms = (_dot(hi, head_ones) + _dot(lo, head_ones)) * (1.0 / hd)
        return (x * lax.rsqrt(ms + RMS_EPS)) * g

    def rope(x, rows):
        return (x * cos_ref[rows, :]
                + pltpu.roll(x, LANES - quarter, 1) * sa_ref[rows, :]
                + pltpu.roll(x, quarter, 1) * sb_ref[rows, :])

    @pl.when(j == 0)
    def _():
        zero = jnp.zeros((tq, LANES), F32)
        for r in range(t // tq):
            rows = slice(r * tq, (r + 1) * tq)
            k = head_rms(k_ref[0, rows, :].astype(F32), gk_ref[...])
            if r * tq >= n_ctx:
                k = rope(k, slice(r * tq - n_ctx, (r + 1) * tq - n_ctx))
            v = v_ref[0, rows, :].astype(F32)
            k_sw = pltpu.roll(k, hd, 1)
            v_sw = pltpu.roll(v, hd, 1)
            v_lo_rest = jnp.where(lane == hd, 1.0, zero)
            v_hi_rest = jnp.where(lane == 0, 1.0, zero)
            for scr, a, a_sw, lo_rest, hi_rest in ((kk_scr, k, k_sw, zero, zero),
                                                   (vv_scr, v, v_sw, v_lo_rest, v_hi_rest)):
                scr[0, rows, :] = jnp.where(lo_half, a, lo_rest).astype(BF16)
                scr[1, rows, :] = jnp.where(lo_half, hi_rest, a_sw).astype(BF16)
                scr[2, rows, :] = jnp.where(lo_half, a_sw, lo_rest).astype(BF16)
                scr[3, rows, :] = jnp.where(lo_half, hi_rest, a).astype(BF16)

    scale = hd ** -0.5

    n_chunks = GQA_Q_WIDTH // LANES

    def tile(nk, rope_rows):
        def scores(c):
            qc = head_rms(q_ref[0, :, c * LANES:(c + 1) * LANES].astype(F32), gq_ref[...])
            if rope_rows is not None:
                qc = rope(qc, rope_rows)
            qc = (qc * scale).astype(BF16)
            grp = (2 * c) // (GQA_Q_HEADS // GQA_KV_HEADS)
            return [_dot_nt(qc, kk_scr[2 * grp + hh, 0:nk, :]) for hh in range(2)]

        ahead = 2
        queue = [scores(c) for c in range(min(ahead, n_chunks))]
        for c in range(n_chunks):
            s_pair = queue.pop(0)
            if c + ahead < n_chunks:
                queue.append(scores(c + ahead))
            grp = (2 * c) // (GQA_Q_HEADS // GQA_KV_HEADS)
            outs = []
            for hh in range(2):
                s = s_pair[hh]
                m = jnp.max(s, axis=-1, keepdims=True)
                p = jnp.exp((s - m).astype(BF16))
                o = _dot(p, vv_scr[2 * grp + hh, 0:nk, :])
                sum_lane = hd if hh == 0 else 0
                den = jnp.sum(jnp.where(lane == sum_lane, o, 0.0), axis=-1, keepdims=True)
                outs.append(o / den)
            o_ref[0, :, c * LANES:(c + 1) * LANES] = jnp.where(lo_half, outs[0], outs[1]).astype(o_ref.dtype)

    @pl.when(j == 0)
    def _():
        tile(n_ctx, None)

    @pl.when(j > 0)
    def _():
        tile(t, pl.ds(pl.multiple_of((j - 1) * tq, tq), tq))


def _gqa_attention(z, tables, gq, gk, n_ctx):
    b, t, _ = z.shape
    tq = ROW_TILE
    assert n_ctx == tq
    n_lat = t - n_ctx
    cos, sa, sb = tables
    tab_spec = pl.BlockSpec((n_lat, LANES), lambda bi, j: (0, 0))
    g_spec = pl.BlockSpec((1, LANES), lambda bi, j: (0, 0))
    return pl.pallas_call(
        functools.partial(_gqa_kernel, n_ctx=n_ctx),
        out_shape=jax.ShapeDtypeStruct((b, t, GQA_Q_WIDTH), BF16),
        grid=(b, t // tq),
        in_specs=[
            pl.BlockSpec((1, tq, GQA_Q_WIDTH), lambda bi, j: (bi, j, COL_GQA_Q // GQA_Q_WIDTH)),
            pl.BlockSpec((1, t, GQA_KV_WIDTH), lambda bi, j: (bi, 0, COL_GQA_K // GQA_KV_WIDTH)),
            pl.BlockSpec((1, t, GQA_KV_WIDTH), lambda bi, j: (bi, 0, COL_GQA_V // GQA_KV_WIDTH)),
            tab_spec, tab_spec, tab_spec, g_spec, g_spec,
        ],
        out_specs=pl.BlockSpec((1, tq, GQA_Q_WIDTH), lambda bi, j: (bi, j, 0)),
        scratch_shapes=[pltpu.VMEM((4, t, LANES), BF16), pltpu.VMEM((4, t, LANES), BF16)],
        compiler_params=_params(("arbitrary", "arbitrary")),
        name="gqa_attention",
    )(z, z, z, cos, sa, sb, gq, gk)


HG_LEVELS = 6


def _hgrn_sum_matrices():
    c = HG_CHUNK
    u = np.arange(c)[None, :]
    r = np.arange(c)[:, None]
    fw, bw = [], []
    for lvl in range(HG_LEVELS):
        hs = c >> (lvl + 1)
        blk = (r // (2 * hs)) * (2 * hs)
        upper = (r % (2 * hs)) >= hs
        last_lower = blk + hs - 1
        first_upper = blk + hs
        fw.append(np.where(upper, (u > last_lower) & (u <= r), (u > r) & (u <= last_lower)))
        bw.append(np.where(upper, (u >= first_upper) & (u < r), (u >= r) & (u < first_upper)))
    fw += [u <= r, u > r]
    bw += [u >= r, u < r]

    def twice(parts):
        w = np.concatenate(parts).astype(np.float32)
        return jnp.asarray(np.concatenate([w, w], axis=1), BF16)

    return twice(fw), twice(bw)


def _hgrn_pair_masks():
    c = HG_CHUNK
    row = np.arange(c)[:, None]
    col = np.arange(c)[None, :]
    fw, bw = [], []
    for lvl in range(HG_LEVELS):
        hs = c >> (lvl + 1)
        same = (row // (2 * hs)) == (col // (2 * hs))
        row_up = (row % (2 * hs)) >= hs
        col_up = (col % (2 * hs)) >= hs
        fw.append(same & row_up & ~col_up)
        bw.append(same & ~row_up & col_up)
    fw.append(row == col)
    bw.append(row == col)

    def in_lane_halves(masks):
        m = np.stack(masks).astype(np.float32)
        z = np.zeros_like(m)
        return jnp.asarray(np.stack([np.concatenate([m, z], axis=-1),
                                     np.concatenate([z, m], axis=-1)], axis=1))

    assert 2 * c == LANES
    return in_lane_halves(fw), in_lane_halves(bw)


def _hgrn_kernel(q_ref, ff_ref, fb_ref, i_ref, g_ref, lbl_ref, ng_ref, wf_ref, wb_ref, mf_ref, mb_ref,
                 o_ref, o_scr, st_scr, *, layer, n_ctx):
    t = q_ref.shape[1]
    c = HG_CHUNK
    dk = HG_DK
    n = t // c
    nc = n_ctx // c
    depth = lbl_ref.shape[0]

    def lower_bound(direction):
        logits = [lbl_ref[d, direction] for d in range(depth)]
        m = functools.reduce(jnp.maximum, logits)
        e = [jnp.exp(x - m) for x in logits]
        tot = functools.reduce(lambda a, b: a + b, e)
        p = [x / tot for x in e]
        cum = functools.reduce(lambda a, b: a + b, p[:layer + 1])
        return cum - p[0]

    def heads_on_rows(x):
        return jnp.concatenate([x[:, h * dk:(h + 1) * dk] for h in range(HG_HEADS)], axis=0)

    def stream(off, f_ref, lb, w_ref, m_ref, exit_row, slot):
        rows = pl.ds(off, c)
        f = lb + (1.0 - lb) * jax.nn.sigmoid(f_ref[0, rows, :])
        kk = 1.0 - f
        decay = jnp.exp(_dot(w_ref[...], jnp.concatenate(_split_bf16(jnp.log(f)), axis=0)))
        q = _silu(q_ref[0, rows, :].astype(F32))
        v = i_ref[0, rows, :]
        a2 = [0.0] * HG_HEADS
        for lvl in range(HG_LEVELS + 1):
            if lvl < HG_LEVELS:
                y = decay[lvl * c:(lvl + 1) * c, :]
                qs, ks = heads_on_rows((q * y).astype(BF16)), heads_on_rows((kk * y).astype(BF16))
            else:
                qs, ks = heads_on_rows(q.astype(BF16)), heads_on_rows(kk.astype(BF16))
            p = _dot_nt(qs, ks)
            for h in range(HG_HEADS):
                tile = (h * c) // LANES
                slab = p[h * c:(h + 1) * c, tile * LANES:(tile + 1) * LANES]
                a2[h] = a2[h] + m_ref[lvl, h % 2] * slab
        e_cum = decay[HG_LEVELS * c:(HG_LEVELS + 1) * c, :]
        e_rest = decay[(HG_LEVELS + 1) * c:(HG_LEVELS + 2) * c, :]
        qe = (q * e_cum).astype(BF16)
        kd = (kk * e_rest).astype(BF16)
        outs = []
        for h in range(HG_HEADS):
            cols = slice(h * dk, (h + 1) * dk)
            st = st_scr[slot, cols, :]
            v_h = v[:, cols]
            v2 = jnp.concatenate([v_h, v_h], axis=0)
            outs.append(_dot(a2[h].astype(BF16), v2) + _dot_nt(qe[:, cols], st.astype(BF16)))
            st_scr[slot, cols, :] = st * e_cum[exit_row:exit_row + 1, cols] + _dot_tn(v_h, kd[:, cols])
        o_scr[rows, :] += jnp.concatenate(outs, axis=1)

    lb_f = lower_bound(0)
    lb_b = lower_bound(1)
    st_scr[...] = jnp.zeros_like(st_scr)
    o_scr[...] = jnp.zeros_like(o_scr)

    unroll = 4
    assert n % unroll == 0

    def body(k2, carry):
        for u in range(unroll):
            k = k2 * unroll + u
            off_f = pl.multiple_of(k * c, c)
            kb = jnp.where(k < nc, nc - 1 - k, n + nc - 1 - k)
            off_b = pl.multiple_of(kb * c, c)
            stream(off_f, ff_ref, lb_f, wf_ref, mf_ref, c - 1, 0)
            stream(off_b, fb_ref, lb_b, wb_ref, mb_ref, 0, 1)
        return carry

    lax.fori_loop(0, n // unroll, body, 0)

    ng = ng_ref[...]

    def readout(r, carry):
        rows = pl.ds(pl.multiple_of(r * ROW_TILE, ROW_TILE), ROW_TILE)
        for h in range(HG_HEADS):
            cols = slice(h * dk, (h + 1) * dk)
            y = (_rms(o_scr[rows, cols]) * ng) * _silu(g_ref[0, rows, cols].astype(F32))
            o_ref[0, rows, cols] = y.astype(o_ref.dtype)
        return carry

    lax.fori_loop(0, t // ROW_TILE, readout, 0)


def _hgrn(z, zf, lb_logits, norm_g, layer, n_ctx):
    b, t, _ = z.shape
    depth = lb_logits.shape[0]
    w = HG_WIDTH
    lbl = lb_logits.astype(F32).reshape(depth, 2, 1, w)
    wf, wb = _hgrn_sum_matrices()
    mf, mb = _hgrn_pair_masks()

    def zcol(base):
        return pl.BlockSpec((1, t, w), lambda bi: (bi, 0, base // w))

    def whole(shape):
        return pl.BlockSpec(shape, lambda bi: (0,) * len(shape))

    return pl.pallas_call(
        functools.partial(_hgrn_kernel, layer=layer, n_ctx=n_ctx),
        out_shape=jax.ShapeDtypeStruct((b, t, w), BF16),
        grid=(b,),
        in_specs=[
            zcol(COL_HG_Q),
            pl.BlockSpec((1, t, w), lambda bi: (bi, 0, 0)),
            pl.BlockSpec((1, t, w), lambda bi: (bi, 0, 1)),
            zcol(COL_HG_I),
            zcol(COL_HG_G),
            whole(lbl.shape), whole((1, HG_DK)), whole(wf.shape), whole(wb.shape),
            whole(mf.shape), whole(mb.shape),
        ],
        out_specs=pl.BlockSpec((1, t, w), lambda bi: (bi, 0, 0)),
        scratch_shapes=[pltpu.VMEM((t, w), F32), pltpu.VMEM((2, w, HG_DK), F32)],
        compiler_params=_params(("arbitrary",)),
        name="hgrn2_scan",
    )(z, zf, zf, z, z, lbl, norm_g.reshape(1, HG_DK).astype(F32), wf, wb, mf, mb)


def _merge_kernel(s_ref, a_ref, b_ref, c_ref, ga_ref, gb_ref, gc_ref, mod_ref, modc_ref,
                  wb_ref, wo_ref, gf_ref, wr_ref, br_ref, s_out, f_out, route_out):
    first = pl.program_id(1) == 0
    tm = s_ref.shape[1]

    y = 0.0
    for idx, (br, gr) in enumerate(((a_ref, ga_ref), (b_ref, gb_ref), (c_ref, gc_ref))):
        y = y + jax.nn.sigmoid(gr[0].astype(F32)) * _dot(br[0], wb_ref[idx])
    proj = _dot(y.astype(BF16), wo_ref[...])
    w_hi, w_lo = _split_bf16(wr_ref[...])
    for r in range(tm // ROW_TILE):
        rows = slice(r * ROW_TILE, (r + 1) * ROW_TILE)
        is_ctx = jnp.logical_and(first, r == 0)
        _merge_rows(s_ref, proj[rows, :], rows, r, is_ctx, mod_ref, modc_ref, gf_ref, w_hi, w_lo, br_ref,
                    s_out, f_out, route_out)


def _merge_rows(s_ref, proj, rows, r, is_ctx, mod_ref, modc_ref, gf_ref, w_hi, w_lo, br_ref,
                s_out, f_out, route_out):
    def mod_row(k):
        return jnp.where(is_ctx, modc_ref[k:k + 1, :], mod_ref[0, k:k + 1, :])

    x = s_ref[0, rows, :] + mod_row(2) * proj
    s_out[0, rows, :] = x
    f = (_rms(x) * gf_ref[...]) * (1.0 + mod_row(4)) + mod_row(3)
    _rows_to_tiles(f_out, r * ROW_TILE, f)

    f_hi, f_lo = _split_bf16(f)
    logits = _dot(f_hi, w_hi) + _dot(f_lo, w_hi) + _dot(f_hi, w_lo) + br_ref[...]

    lane = lax.broadcasted_iota(jnp.int32, logits.shape, 1)
    is_group = jnp.logical_and(lane >= N_EXPERTS, lane < N_EXPERTS + N_GROUPS)

    def first_argmax(x, x_max):
        return jnp.min(jnp.where(x == x_max, lane, LANES), axis=-1, keepdims=True)

    gl = jnp.where(is_group, logits, MASK_NEG)
    g_max = jnp.max(gl, axis=-1, keepdims=True)
    g_idx = first_argmax(gl, g_max) - N_EXPERTS
    p_group = 1.0 / jnp.sum(jnp.exp(gl - g_max), axis=-1, keepdims=True)
    in_group = jnp.logical_and(lane < N_EXPERTS, (lane // EXPERTS_PER_GROUP) == g_idx)
    e1 = jnp.where(in_group, logits, MASK_NEG)
    v1 = jnp.max(e1, axis=-1, keepdims=True)
    i1 = first_argmax(e1, v1)
    e2 = jnp.where(lane == i1, MASK_NEG, e1)
    v2 = jnp.max(e2, axis=-1, keepdims=True)
    i2 = first_argmax(e2, v2)
    r21 = jnp.exp(v2 - v1)
    w1 = 1.0 / (1.0 + r21)
    w2 = r21 * w1
    route = jnp.where(lane == 0, i1.astype(F32), 0.0)
    route = jnp.where(lane == 1, i2.astype(F32), route)
    route = jnp.where(lane == 2, w1 * p_group, route)
    route = jnp.where(lane == 3, w2 * p_group, route)
    route_out[0, :, rows] = route.T[0:SUBLANES, :]


def _merge(s, a, bb, cc, z, mod, modc, wb, wo, gf, wr, br):
    b, t, d = s.shape
    tm = WIDE_ROWS
    bw = a.shape[-1]
    gate0 = COL_GATES // d

    def rows(width, colblk=0):
        return pl.BlockSpec((1, tm, width), lambda bi, j: (bi, j, colblk))

    def whole(shape):
        return pl.BlockSpec(shape, lambda bi, j: (0,) * len(shape))

    return pl.pallas_call(
        _merge_kernel,
        out_shape=(jax.ShapeDtypeStruct((b, t, d), F32),
                   jax.ShapeDtypeStruct((b, t * SUBLANES, LANES), F32),
                   jax.ShapeDtypeStruct((b, SUBLANES, t), F32)),
        grid=(b, t // tm),
        in_specs=[
            rows(d), rows(bw), rows(bw), rows(bw),
            rows(d, gate0), rows(d, gate0 + 1), rows(d, gate0 + 2),
            pl.BlockSpec((1, N_MOD, d), lambda bi, j: (bi, 0, 0)),
            whole((N_MOD, d)),
            whole(wb.shape), whole(wo.shape), whole((1, d)), whole(wr.shape), whole((1, LANES)),
        ],
        out_specs=(rows(d), pl.BlockSpec((1, tm * SUBLANES, LANES), lambda bi, j: (bi, j, 0)),
                   pl.BlockSpec((1, SUBLANES, tm), lambda bi, j: (bi, 0, j))),
        compiler_params=_params(("arbitrary", "arbitrary")),
        name="merge_router",
    )(s, a, bb, cc, z, z, z, mod, modc, wb, wo, gf, wr, br)


MOE_CHUNK = 320
MOE_PAD_TOKENS = SUBLANES


def _moe_kernel(tok_ref, wgt_ref, cnt_ref, off_ref, f_ref, wg_ref, wu_ref, wd_ref, y_ref,
                xs_scr, ys_scr):
    e = pl.program_id(1)
    sub = SUBLANES
    n_col = f_ref.shape[2]
    t_dummy = f_ref.shape[1]

    @pl.when(e == 0)
    def _():
        y_ref[...] = jnp.zeros_like(y_ref)
        xs_scr[...] = jnp.zeros_like(xs_scr)

    cnt = cnt_ref[0, 0, e]
    off = off_ref[0, 0, e]
    wg = wg_ref[0, 0].astype(BF16)
    wu = wu_ref[0, 0].astype(BF16)
    wd = wd_ref[0, 0].astype(BF16)

    def chunk(ci, carry):
        base = off + ci * MOE_CHUNK
        m = jnp.minimum(MOE_CHUNK, cnt - ci * MOE_CHUNK)
        n_grp = (m + sub - 1) // sub

        def gather(gi, c2):
            for u in range(sub):
                r = gi * sub + u
                tok = tok_ref[0, 0, base + r]
                xs_scr[pl.ds(pl.multiple_of(r * n_col, n_col), n_col), :] = f_ref[0, tok]
            return c2

        lax.fori_loop(0, n_grp, gather, 0)
        x = jnp.concatenate(
            [xs_scr[pl.ds(s, MOE_CHUNK, stride=n_col), :] for s in range(n_col)], axis=1).astype(BF16)
        he = (_silu(_dot(x, wg)) * _dot(x, wu)).astype(BF16)
        y = _dot(he, wd)
        for s in range(n_col):
            ys_scr[pl.ds(s, MOE_CHUNK, stride=n_col), :] = y[:, s * LANES:(s + 1) * LANES]

        def scatter(gi, c2):
            toks, vals = [], []
            for u in range(sub):
                r = gi * sub + u
                tok = jnp.where(r < m, tok_ref[0, 0, base + r], t_dummy)
                contrib = wgt_ref[0, 0, base + r] * ys_scr[pl.ds(pl.multiple_of(r * n_col, n_col), n_col), :]
                toks.append(tok)
                vals.append(y_ref[0, tok] + contrib)
            for tok, val in zip(toks, vals):
                y_ref[0, tok] = val
            return c2

        lax.fori_loop(0, n_grp, scatter, 0)
        return carry

    lax.fori_loop(0, (cnt + MOE_CHUNK - 1) // MOE_CHUNK, chunk, 0)


def _moe(f_tiles, route, w_gate, w_up, w_down, layer):
    b, _, t = route.shape
    d, hid = w_gate.shape[-2:]
    n_col = d // LANES
    assert n_col == SUBLANES and f_tiles.shape == (b, t * n_col, LANES)
    n_slot = TOP_K * t

    eid = route[:, 0:TOP_K, :].astype(jnp.int32).reshape(b, n_slot)
    wts = route[:, TOP_K:2 * TOP_K, :].reshape(b, n_slot)
    perm = jnp.argsort(eid, axis=1).astype(jnp.int32)
    pad = jnp.zeros((b, MOE_CHUNK), jnp.int32)
    tok_sorted = jnp.concatenate([perm % t, pad], axis=1).reshape(b, 1, n_slot + MOE_CHUNK)
    wgt_sorted = jnp.concatenate([jnp.take_along_axis(wts, perm, axis=1), pad.astype(F32)],
                                 axis=1).reshape(b, 1, n_slot + MOE_CHUNK)
    counts = jnp.sum(eid[:, :, None] == jnp.arange(N_EXPERTS)[None, None, :], axis=1).astype(jnp.int32)
    offs = (jnp.cumsum(counts, axis=1) - counts).astype(jnp.int32)
    counts = counts.reshape(b, 1, N_EXPERTS)
    offs = offs.reshape(b, 1, N_EXPERTS)

    def smem(n):
        return pl.BlockSpec((1, 1, n), lambda bi, e: (bi, 0, 0), memory_space=pltpu.SMEM)

    t_out = t + MOE_PAD_TOKENS
    y4 = pl.pallas_call(
        _moe_kernel,
        out_shape=jax.ShapeDtypeStruct((b, t_out, n_col, LANES), F32),
        grid=(b, N_EXPERTS),
        in_specs=[
            smem(n_slot + MOE_CHUNK), smem(n_slot + MOE_CHUNK), smem(N_EXPERTS), smem(N_EXPERTS),
            pl.BlockSpec((1, t, n_col, LANES), lambda bi, e: (bi, 0, 0, 0),
                         pipeline_mode=pl.Buffered(1)),
            pl.BlockSpec((1, 1, d, hid), lambda bi, e: (layer, e, 0, 0)),
            pl.BlockSpec((1, 1, d, hid), lambda bi, e: (layer, e, 0, 0)),
            pl.BlockSpec((1, 1, hid, d), lambda bi, e: (layer, e, 0, 0)),
        ],
        out_specs=pl.BlockSpec((1, t_out, n_col, LANES), lambda bi, e: (bi, 0, 0, 0)),
        scratch_shapes=[pltpu.VMEM((MOE_CHUNK * n_col, LANES), F32),
                        pltpu.VMEM((MOE_CHUNK * n_col, LANES), F32)],
        compiler_params=_params(("arbitrary", "arbitrary")),
        name="moe_experts",
    )(tok_sorted, wgt_sorted, counts, offs, f_tiles.reshape(b, t, n_col, LANES), w_gate, w_up, w_down)
    return y4.reshape(b, t_out * n_col, LANES)


def _final_kernel(s_ref, y_ref, mod_ref, g_ref, o_ref):
    x = s_ref[0] + mod_ref[0, 5:6, :] * _tiles_to_rows(y_ref, 0, s_ref.shape[1])
    o_ref[0] = _rms(x) * g_ref[...]


def _final_norm(s, y, mod, g, n_ctx):
    b, t, d = s.shape
    tm = ROW_TILE
    skip = n_ctx // tm
    lat_rows = pl.BlockSpec((1, tm, d), lambda bi, j: (bi, j + skip, 0))
    return pl.pallas_call(
        _final_kernel,
        out_shape=jax.ShapeDtypeStruct((b, t - n_ctx, d), F32),
        grid=(b, (t - n_ctx) // tm),
        in_specs=[lat_rows,
                  pl.BlockSpec((1, tm * SUBLANES, LANES), lambda bi, j: (bi, j + skip, 0)),
                  pl.BlockSpec((1, N_MOD, d), lambda bi, j: (bi, 0, 0)),
                  pl.BlockSpec((1, d), lambda bi, j: (0, 0))],
        out_specs=pl.BlockSpec((1, tm, d), lambda bi, j: (bi, j, 0)),
        compiler_params=_params(("arbitrary", "arbitrary")),
        name="final_norm",
    )(s, y, mod, g)


def kernel(x, c, ctx, c_ctx, w_ada, b_ada, g_mix, g_ffn, w_in, na_rpb, hg_lb_logits, hg_norm_g,
           gqa_qnorm_g, gqa_knorm_g, w_branch, w_out, w_group_router, b_group_router,
           w_expert_router, b_expert_router, w_exp_gate, w_exp_up, w_exp_down, g_final):
    b, n_lat, d = x.shape
    n_ctx = ctx.shape[1]
    depth = w_in.shape[0]
    assert n_ctx == ROW_TILE and n_lat % ROW_TILE == 0 and (n_ctx + n_lat) % WIDE_ROWS == 0
    rows = n_lat // GRID_W

    s = (ctx, x)

    c_rows = 16
    c_all = jnp.concatenate([c, c_ctx[None, :], jnp.zeros((c_rows - b - 1, d), c.dtype)], axis=0)
    mod_all = _ada(c_all, w_ada, b_ada).reshape(depth, c_rows, N_MOD, d)

    tables = _rope_tables(n_lat)
    rep = LANES // GQA_HEAD_DIM

    w_pad = jnp.concatenate(
        [w_in[:, :, :COL_RAW_GATES],
         jnp.zeros((depth, d, COL_GATES - COL_RAW_GATES), w_in.dtype),
         w_in[:, :, COL_RAW_GATES:]], axis=2).astype(BF16)

    prev = None
    for l in range(depth):
        mod = mod_all[l, :b]
        modc = mod_all[l, b]
        z, zf, s = _inproj(s, g_mix[l].reshape(1, d), modc, mod, w_pad, l, n_ctx, prev)

        a = _na_attention(z, _na_bias_table(na_rpb[l], rows), n_ctx)
        cc = _gqa_attention(z, tables,
                            jnp.tile(gqa_qnorm_g[l].astype(F32), rep).reshape(1, LANES),
                            jnp.tile(gqa_knorm_g[l].astype(F32), rep).reshape(1, LANES), n_ctx)
        bb = _hgrn(z, zf, hg_lb_logits, hg_norm_g[l], l, n_ctx)

        wr = jnp.concatenate(
            [w_expert_router[l], w_group_router[l],
             jnp.zeros((d, LANES - N_EXPERTS - N_GROUPS), F32)], axis=1)
        br = jnp.concatenate(
            [b_expert_router[l], b_group_router[l],
             jnp.zeros((LANES - N_EXPERTS - N_GROUPS,), F32)]).reshape(1, LANES)
        s, f, route = _merge(s, a, bb, cc, z, mod, modc, w_branch[l].astype(BF16),
                             w_out[l].astype(BF16), g_ffn[l].reshape(1, d), wr, br)
        y = _moe(f, route, w_exp_gate, w_exp_up, w_exp_down, l)
        prev = (y, mod, modc)

    return _final_norm(s, prev[0], prev[1], g_final.reshape(1, d), n_ctx)
```

```python
import functools

import numpy as np
import jax
import jax.numpy as jnp
from jax import lax
from jax.experimental import pallas as pl
from jax.experimental.pallas import tpu as pltpu

F32 = jnp.float32
BF16 = jnp.bfloat16

RMS_EPS = 1e-6
N_MOD = 6
GRID_W = 64

NA_HEADS = 8
NA_HEAD_DIM = 64
NA_WIDTH = NA_HEADS * NA_HEAD_DIM
WIN_ROWS = 8
WIN_COLS = 16
NA_QROWS = 4
NA_KROWS = 12

HG_HEADS = 4
HG_DK = 128
HG_WIDTH = HG_HEADS * HG_DK
HG_CHUNK = 64

GQA_Q_HEADS = 8
GQA_KV_HEADS = 2
GQA_HEAD_DIM = 64
GQA_Q_WIDTH = GQA_Q_HEADS * GQA_HEAD_DIM
GQA_KV_WIDTH = GQA_KV_HEADS * GQA_HEAD_DIM
ROPE_THETA = 10000.0

N_GROUPS = 4
EXPERTS_PER_GROUP = 4
N_EXPERTS = N_GROUPS * EXPERTS_PER_GROUP
TOP_K = 2

LANES = 128
SUBLANES = 8
ROW_TILE = 256
WIDE_ROWS = 768
MASK_NEG = -1e30

COL_NA_Q = 0
COL_NA_K = 512
COL_NA_V = 1024
COL_HG_Q = 1536
COL_HG_FF = 2048
COL_HG_FB = 2560
COL_HG_I = 3072
COL_HG_G = 3584
COL_GQA_Q = 4096
COL_GQA_K = 4608
COL_GQA_V = 4736
COL_RAW_GATES = 4864
COL_GATES = 5120
IN_COLS_PAD = 8192
IN_TILE = 2048

VMEM_LIMIT = 56 * 1024 * 1024


def _dot(a, b):
    return jnp.dot(a, b, preferred_element_type=F32)


def _dot_nt(a, b):
    return lax.dot_general(a, b, (((1,), (1,)), ((), ())), preferred_element_type=F32)


def _dot_tn(a, b):
    return lax.dot_general(a, b, (((0,), (0,)), ((), ())), preferred_element_type=F32)


def _split_bf16(x):
    hi = x.astype(BF16)
    lo = (x - hi.astype(F32)).astype(BF16)
    return hi, lo


def _silu(x):
    return x * jax.nn.sigmoid(x)


def _rms(x):
    return x * lax.rsqrt(jnp.mean(x * x, axis=-1, keepdims=True) + RMS_EPS)


def _tiles_to_rows(ref, tok0, n):
    return jnp.concatenate(
        [ref[0, pl.ds(tok0 * SUBLANES + s, n, stride=SUBLANES), :] for s in range(SUBLANES)], axis=1)


def _rows_to_tiles(ref, tok0, x):
    n = x.shape[0]
    for s in range(SUBLANES):
        ref[0, pl.ds(tok0 * SUBLANES + s, n, stride=SUBLANES), :] = x[:, s * LANES:(s + 1) * LANES]


def _params(semantics, vmem=VMEM_LIMIT):
    return pltpu.CompilerParams(dimension_semantics=semantics, vmem_limit_bytes=vmem)


def _ada_kernel(c_ref, w_ref, b_ref, o_ref):
    sc = _silu(c_ref[...]).astype(BF16)
    o_ref[0] = _dot(sc, w_ref[0].astype(BF16)) + b_ref[0]


def _ada(c_all, w_ada, b_ada):
    depth, d, n = w_ada.shape
    rows = c_all.shape[0]
    tn = 1536
    return pl.pallas_call(
        _ada_kernel,
        out_shape=jax.ShapeDtypeStruct((depth, rows, n), F32),
        grid=(depth, n // tn),
        in_specs=[
            pl.BlockSpec((rows, d), lambda l, j: (0, 0)),
            pl.BlockSpec((1, d, tn), lambda l, j: (l, 0, j)),
            pl.BlockSpec((1, 1, tn), lambda l, j: (l, 0, j)),
        ],
        out_specs=pl.BlockSpec((1, rows, tn), lambda l, j: (l, 0, j)),
        compiler_params=_params(("arbitrary", "arbitrary")),
        name="ada_mod",
    )(c_all, w_ada, b_ada.reshape(depth, 1, n))


PREP_TILE = 256


def _wprep_kernel(w_ref, o_ref):
    is_pad = pl.program_id(1) == COL_RAW_GATES // PREP_TILE
    o_ref[0] = jnp.where(is_pad, 0.0, w_ref[0]).astype(o_ref.dtype)


def _prep_w_in(w_in):
    depth, d, n = w_in.shape
    gap = COL_GATES - COL_RAW_GATES
    assert gap == PREP_TILE and COL_RAW_GATES % PREP_TILE == 0 and n + gap == IN_COLS_PAD
    pad_tile = COL_RAW_GATES // PREP_TILE

    def src(j):
        return jnp.where(j <= pad_tile, jnp.minimum(j, n // PREP_TILE - 1), j - 1)

    return pl.pallas_call(
        _wprep_kernel,
        out_shape=jax.ShapeDtypeStruct((depth, d, IN_COLS_PAD), BF16),
        grid=(depth, IN_COLS_PAD // PREP_TILE),
        in_specs=[pl.BlockSpec((1, d, PREP_TILE), lambda l, j: (l, 0, src(j)))],
        out_specs=pl.BlockSpec((1, d, PREP_TILE), lambda l, j: (l, 0, j)),
        compiler_params=_params(("arbitrary", "arbitrary")),
        name="w_in_prep",
    )(w_in)


def _inproj_kernel(*refs, n_ctx, tiles_per_batch, residual):
    n_sub = WIDE_ROWS // ROW_TILE
    if residual:
        (s_ref, y_ref, modp_ref, modcp_ref, g_ref, modc_ref, mod_ref, w_ref,
         z_ref, zf_ref, s_out, h_scr) = refs
    else:
        ctx_ref, *lat_refs = refs[:1 + n_sub]
        g_ref, modc_ref, mod_ref, w_ref, z_ref, zf_ref, s_out, h_scr = refs[1 + n_sub:]
    i = pl.program_id(0)
    j = pl.program_id(1)

    @pl.when(j == 0)
    def _():
        g = g_ref[...]
        first = (i % tiles_per_batch) == 0
        for r in range(n_sub):
            rows = slice(r * ROW_TILE, (r + 1) * ROW_TILE)
            is_ctx = jnp.logical_and(first, r * ROW_TILE < n_ctx)

            def pick(ctx_ref, lat_ref, k):
                return jnp.where(is_ctx, ctx_ref[k:k + 1, :], lat_ref[0, k:k + 1, :])

            if residual:
                x = s_ref[0, rows, :]
                x = x + pick(modcp_ref, modp_ref, 5) * _tiles_to_rows(y_ref, r * ROW_TILE, ROW_TILE)
            elif r == 0:
                x = jnp.where(is_ctx, ctx_ref[0], lat_refs[0][0])
            else:
                x = lat_refs[r][0]
            s_out[0, rows, :] = x
            h = (_rms(x) * g) * (1.0 + pick(modc_ref, mod_ref, 1)) + pick(modc_ref, mod_ref, 0)
            h_scr[rows, :] = h.astype(BF16)

    acc = _dot(h_scr[...], w_ref[0])
    z_ref[0] = acc.astype(BF16)

    @pl.when(j == COL_HG_FF // IN_TILE)
    def _():
        lo = COL_HG_FF % IN_TILE
        zf_ref[0] = acc[:, lo:lo + zf_ref.shape[2]]


def _inproj(stream, g, modc, mod, w_pad, layer, n_ctx, prev=None):
    residual = prev is not None
    if residual:
        b, t, d = stream.shape
    else:
        ctx, x = stream
        b, n_lat, d = x.shape
        t = n_ctx + n_lat
        assert ctx.shape == (b, n_ctx, d) and n_ctx == ROW_TILE
    n = w_pad.shape[2]
    tm = WIDE_ROWS
    tpb = t // tm
    n_sub = tm // ROW_TILE
    f_cols = 2 * HG_WIDTH
    assert COL_HG_FB == COL_HG_FF + HG_WIDTH and d == f_cols
    assert COL_HG_FF // IN_TILE == (COL_HG_FF + f_cols - 1) // IN_TILE

    row_spec = pl.BlockSpec((1, tm, d), lambda i, j: (i // tpb, i % tpb, 0))
    mod_spec = pl.BlockSpec((1, N_MOD, d), lambda i, j: (i // tpb, 0, 0))
    modc_spec = pl.BlockSpec((N_MOD, d), lambda i, j: (0, 0))
    if residual:
        y_spec = pl.BlockSpec((1, tm * SUBLANES, LANES), lambda i, j: (i // tpb, i % tpb, 0))
        in_specs = [row_spec, y_spec, mod_spec, modc_spec]
        args = [stream, prev[0], prev[1], prev[2]]
    else:
        def lat_spec(r):
            return pl.BlockSpec(
                (1, ROW_TILE, d), lambda i, j: (i // tpb, jnp.maximum((i % tpb) * n_sub + r - 1, 0), 0))

        in_specs = [pl.BlockSpec((1, ROW_TILE, d), lambda i, j: (i // tpb, 0, 0))]
        in_specs += [lat_spec(r) for r in range(n_sub)]
        args = [ctx] + [x] * n_sub
    in_specs += [pl.BlockSpec((1, d), lambda i, j: (0, 0)), modc_spec, mod_spec,
                 pl.BlockSpec((1, d, IN_TILE), lambda i, j: (layer, 0, j))]
    args += [g, modc, mod, w_pad]
    out_shape = [jax.ShapeDtypeStruct((b, t, n), BF16),
                 jax.ShapeDtypeStruct((b, t, f_cols), F32),
                 jax.ShapeDtypeStruct((b, t, d), F32)]
    out_specs = [
        pl.BlockSpec((1, tm, IN_TILE), lambda i, j: (i // tpb, i % tpb, j)),
        row_spec,
        row_spec,
    ]
    return pl.pallas_call(
        functools.partial(_inproj_kernel, n_ctx=n_ctx, tiles_per_batch=tpb, residual=residual),
        out_shape=tuple(out_shape),
        grid=(b * tpb, n // IN_TILE),
        in_specs=in_specs,
        out_specs=tuple(out_specs),
        scratch_shapes=[pltpu.VMEM((tm, d), BF16)],
        compiler_params=_params(("arbitrary", "arbitrary")),
        name="in_proj",
    )(*args)


def _na_block_start(blk, rows):
    return jnp.clip(NA_QROWS * blk - WIN_ROWS // 2, 0, rows - NA_KROWS)


def _na_bias_table(rpb, rows):
    n_blk = rows // NA_QROWS
    assert rows % NA_QROWS == 0 and n_blk >= 3 and rows >= NA_KROWS and NA_KROWS % 2 == 0
    assert 2 * GRID_W == LANES
    qc = np.arange(GRID_W)[:, None]
    kc = np.arange(GRID_W)[None, :]
    c0 = np.clip(qc - WIN_COLS // 2, 0, GRID_W - WIN_COLS)
    col_ok = (kc >= c0) & (kc < c0 + WIN_COLS)
    dcol = np.clip(kc - qc + WIN_COLS - 1, 0, 2 * WIN_COLS - 2)
    col_sel = (np.arange(2 * WIN_COLS - 1)[:, None, None] == dcol[None]).astype(np.float32)
    per_row = jnp.einsum('hab,bqk->haqk', rpb.astype(F32), col_sel, precision=lax.Precision.HIGHEST)
    per_row = jnp.where(col_ok, per_row, MASK_NEG)
    per_row = jnp.concatenate([per_row, per_row], axis=-1)
    h, n_dr = per_row.shape[:2]
    return pl.pallas_call(
        functools.partial(_na_bias_kernel, rows=rows),
        out_shape=jax.ShapeDtypeStruct((h, 3, NA_QROWS * GRID_W, NA_KROWS * GRID_W), F32),
        grid=(h, 3),
        in_specs=[pl.BlockSpec((1, n_dr, GRID_W, LANES), lambda hi, p: (hi, 0, 0, 0))],
        out_specs=pl.BlockSpec((1, 1, NA_QROWS * GRID_W, NA_KROWS * GRID_W), lambda hi, p: (hi, p, 0, 0)),
        compiler_params=_params(("arbitrary", "arbitrary")),
        name="na_bias_table",
    )(per_row)


def _na_bias_kernel(t_ref, o_ref, *, rows):
    p = pl.program_id(1)
    n_blk = rows // NA_QROWS
    lo_half = lax.broadcasted_iota(jnp.int32, (GRID_W, LANES), 1) < GRID_W
    masked = jnp.full((GRID_W, LANES), MASK_NEG, F32)
    for pat, blk in enumerate((0, 1, n_blk - 1)):

        @pl.when(p == pat)
        def _():
            u0 = int(np.clip(NA_QROWS * blk - WIN_ROWS // 2, 0, rows - NA_KROWS))
            for j in range(NA_QROWS):
                r = NA_QROWS * blk + j
                r0 = int(np.clip(r - WIN_ROWS // 2, 0, rows - WIN_ROWS))
                for pair in range(NA_KROWS // 2):
                    halves = []
                    for i in (2 * pair, 2 * pair + 1):
                        krow = u0 + i
                        inside = r0 <= krow < r0 + WIN_ROWS
                        halves.append(t_ref[0, krow - r + WIN_ROWS - 1] if inside else masked)
                    o_ref[0, 0, j * GRID_W:(j + 1) * GRID_W, pair * LANES:(pair + 1) * LANES] = (
                        jnp.where(lo_half, halves[0], halves[1]))


def _na_kernel(q_ref, k_ref, v_ref, bias_ref, o_ref, *, n_ctx, rows):
    i = pl.program_id(1)
    tq = q_ref.shape[1]
    lane = lax.broadcasted_iota(jnp.int32, (tq, LANES), 1)
    lo_half = lane < NA_HEAD_DIM

    scale = NA_HEAD_DIM ** -0.5

    def pair_scores(hp, key_rows, with_bias):
        cols = slice(hp * LANES, (hp + 1) * LANES)
        q2 = q_ref[0, :, cols] * scale
        out = []
        for hh in range(2):
            keep = lo_half if hh == 0 else jnp.logical_not(lo_half)
            qm = jnp.where(keep, q2, jnp.zeros_like(q2))
            blocks = []
            for n, kr in enumerate(key_rows):
                s = _dot_nt(qm, k_ref[0, kr, cols])
                if with_bias and n == 0:
                    s = s + bias_ref[2 * hp + hh, 0]
                blocks.append(s)
            out.append(blocks)
        return out

    def pair_finish(hp, scores, key_rows):
        cols = slice(hp * LANES, (hp + 1) * LANES)
        outs = []
        for blocks in scores:
            m = functools.reduce(jnp.maximum, [jnp.max(s, axis=-1, keepdims=True) for s in blocks])
            den = 0.0
            acc = 0.0
            for s, kr in zip(blocks, key_rows):
                p = jnp.exp(s - m)
                den = den + jnp.sum(p, axis=-1, keepdims=True)
                acc = acc + _dot(p.astype(BF16), v_ref[0, kr, cols])
            outs.append(acc / den)
        o_ref[0, :, cols] = jnp.where(lo_half, outs[0], outs[1]).astype(o_ref.dtype)

    def attend(key_rows, with_bias):
        n_pairs = NA_HEADS // 2
        nxt = pair_scores(0, key_rows, with_bias)
        for hp in range(n_pairs):
            cur = nxt
            if hp + 1 < n_pairs:
                nxt = pair_scores(hp + 1, key_rows, with_bias)
            pair_finish(hp, cur, key_rows)

    ctx_rows = slice(0, n_ctx)

    @pl.when(i == 0)
    def _():
        attend([ctx_rows], False)

    @pl.when(i > 0)
    def _():
        u0 = _na_block_start(i - 1, rows)
        local_rows = pl.ds(pl.multiple_of(n_ctx + u0 * GRID_W, GRID_W), NA_KROWS * GRID_W)
        attend([local_rows, ctx_rows], True)


def _na_attention(z, bias_tab, n_ctx):
    b, t, _ = z.shape
    rows = (t - n_ctx) // GRID_W
    tq = NA_QROWS * GRID_W
    assert n_ctx == tq
    n_blk = rows // NA_QROWS
    wk = NA_KROWS * GRID_W

    def pattern(i):
        return jnp.where(i <= 1, 0, jnp.where(i == n_blk, 2, 1))

    return pl.pallas_call(
        functools.partial(_na_kernel, n_ctx=n_ctx, rows=rows),
        out_shape=jax.ShapeDtypeStruct((b, t, NA_WIDTH), BF16),
        grid=(b, 1 + n_blk),
        in_specs=[
            pl.BlockSpec((1, tq, NA_WIDTH), lambda bi, i: (bi, i, COL_NA_Q // NA_WIDTH)),
            pl.BlockSpec((1, t, NA_WIDTH), lambda bi, i: (bi, 0, COL_NA_K // NA_WIDTH)),
            pl.BlockSpec((1, t, NA_WIDTH), lambda bi, i: (bi, 0, COL_NA_V // NA_WIDTH)),
            pl.BlockSpec((NA_HEADS, 1, tq, wk), lambda bi, i: (0, pattern(i), 0, 0)),
        ],
        out_specs=pl.BlockSpec((1, tq, NA_WIDTH), lambda bi, i: (bi, i, 0)),
        compiler_params=_params(("arbitrary", "arbitrary")),
        name="na_attention",
    )(z, z, z, bias_tab)


def _rope_tables(n_tokens):
    t = jnp.arange(n_tokens)
    pos = jnp.stack([t // GRID_W, t % GRID_W], axis=-1).astype(F32)
    n_freq = GQA_HEAD_DIM // 4
    inv_freq = jnp.power(ROPE_THETA, -jnp.arange(n_freq, dtype=F32) / n_freq)
    ang = pos[:, :, None] * inv_freq
    ang = jnp.concatenate([ang, ang], axis=-1).reshape(n_tokens, GQA_HEAD_DIM)
    cos, sin = jnp.cos(ang), jnp.sin(ang)
    first = (np.arange(GQA_HEAD_DIM) % (2 * n_freq)) < n_freq
    sin_a = jnp.where(first, -sin, 0.0)
    sin_b = jnp.where(first, 0.0, sin)
    rep = LANES // GQA_HEAD_DIM
    return tuple(jnp.tile(a, (1, rep)) for a in (cos, sin_a, sin_b))


def _gqa_kernel(q_ref, k_ref, v_ref, cos_ref, sa_ref, sb_ref, gq_ref, gk_ref, o_ref,
                kk_scr, vv_scr, *, n_ctx):
    j = pl.program_id(1)
    t = k_ref.shape[1]
    tq = q_ref.shape[1]
    hd = GQA_HEAD_DIM
    quarter = hd // 4

    r_i = lax.broadcasted_iota(jnp.int32, (LANES, LANES), 0)
    c_i = lax.broadcasted_iota(jnp.int32, (LANES, LANES), 1)
    head_ones = jnp.where((r_i // hd) == (c_i // hd), 1.0, 0.0).astype(BF16)
    lane = lax.broadcasted_iota(jnp.int32, (tq, LANES), 1)
    lo_half = lane < hd

    def head_rms(x, g):
        hi, lo = _split_bf16(x * x)
        ms = (_dot(hi, head_ones) + _dot(lo, head_ones)) * (1.0 / hd)
        return (x * lax.rsqrt(ms + RMS_EPS)) * g

    def rope(x, rows):
        return (x * cos_ref[rows, :]
                + pltpu.roll(x, LANES - quarter, 1) * sa_ref[rows, :]
                + pltpu.roll(x, quarter, 1) * sb_ref[rows, :])

    @pl.when(j == 0)
    def _():
        zero = jnp.zeros((tq, LANES), F32)
        for r in range(t // tq):
            rows = slice(r * tq, (r + 1) * tq)
            k = head_rms(k_ref[0, rows, :].astype(F32), gk_ref[...])
            if r * tq >= n_ctx:
                k = rope(k, slice(r * tq - n_ctx, (r + 1) * tq - n_ctx))
            v = v_ref[0, rows, :].astype(F32)
            k_sw = pltpu.roll(k, hd, 1)
            v_sw = pltpu.roll(v, hd, 1)
            v_lo_rest = jnp.where(lane == hd, 1.0, zero)
            v_hi_rest = jnp.where(lane == 0, 1.0, zero)
            for scr, a, a_sw, lo_rest, hi_rest in ((kk_scr, k, k_sw, zero, zero),
                                                   (vv_scr, v, v_sw, v_lo_rest, v_hi_rest)):
                scr[0, rows, :] = jnp.where(lo_half, a, lo_rest).astype(BF16)
                scr[1, rows, :] = jnp.where(lo_half, hi_rest, a_sw).astype(BF16)
                scr[2, rows, :] = jnp.where(lo_half, a_sw, lo_rest).astype(BF16)
                scr[3, rows, :] = jnp.where(lo_half, hi_rest, a).astype(BF16)

    scale = hd ** -0.5

    n_chunks = GQA_Q_WIDTH // LANES

    def tile(nk, rope_rows):
        def scores(c):
            qc = head_rms(q_ref[0, :, c * LANES:(c + 1) * LANES].astype(F32), gq_ref[...])
            if rope_rows is not None:
                qc = rope(qc, rope_rows)
            qc = (qc * scale).astype(BF16)
            grp = (2 * c) // (GQA_Q_HEADS // GQA_KV_HEADS)
            return [_dot_nt(qc, kk_scr[2 * grp + hh, 0:nk, :]) for hh in range(2)]

        ahead = 2
        queue = [scores(c) for c in range(min(ahead, n_chunks))]
        for c in range(n_chunks):
            s_pair = queue.pop(0)
            if c + ahead < n_chunks:
                queue.append(scores(c + ahead))
            grp = (2 * c) // (GQA_Q_HEADS // GQA_KV_HEADS)
            outs = []
            for hh in range(2):
                s = s_pair[hh]
                m = jnp.max(s, axis=-1, keepdims=True)
                p = jnp.exp((s - m).astype(BF16))
                o = _dot(p, vv_scr[2 * grp + hh, 0:nk, :])
                sum_lane = hd if hh == 0 else 0
                den = jnp.sum(jnp.where(lane == sum_lane, o, 0.0), axis=-1, keepdims=True)
                outs.append(o / den)
            o_ref[0, :, c * LANES:(c + 1) * LANES] = jnp.where(lo_half, outs[0], outs[1]).astype(o_ref.dtype)

    @pl.when(j == 0)
    def _():
        tile(n_ctx, None)

    @pl.when(j > 0)
    def _():
        tile(t, pl.ds(pl.multiple_of((j - 1) * tq, tq), tq))


def _gqa_attention(z, tables, gq, gk, n_ctx):
    b, t, _ = z.shape
    tq = ROW_TILE
    assert n_ctx == tq
    n_lat = t - n_ctx
    cos, sa, sb = tables
    tab_spec = pl.BlockSpec((n_lat, LANES), lambda bi, j: (0, 0))
    g_spec = pl.BlockSpec((1, LANES), lambda bi, j: (0, 0))
    return pl.pallas_call(
        functools.partial(_gqa_kernel, n_ctx=n_ctx),
        out_shape=jax.ShapeDtypeStruct((b, t, GQA_Q_WIDTH), BF16),
        grid=(b, t // tq),
        in_specs=[
            pl.BlockSpec((1, tq, GQA_Q_WIDTH), lambda bi, j: (bi, j, COL_GQA_Q // GQA_Q_WIDTH)),
            pl.BlockSpec((1, t, GQA_KV_WIDTH), lambda bi, j: (bi, 0, COL_GQA_K // GQA_KV_WIDTH)),
            pl.BlockSpec((1, t, GQA_KV_WIDTH), lambda bi, j: (bi, 0, COL_GQA_V // GQA_KV_WIDTH)),
            tab_spec, tab_spec, tab_spec, g_spec, g_spec,
        ],
        out_specs=pl.BlockSpec((1, tq, GQA_Q_WIDTH), lambda bi, j: (bi, j, 0)),
        scratch_shapes=[pltpu.VMEM((4, t, LANES), BF16), pltpu.VMEM((4, t, LANES), BF16)],
        compiler_params=_params(("arbitrary", "arbitrary")),
        name="gqa_attention",
    )(z, z, z, cos, sa, sb, gq, gk)


HG_LEVELS = 6


def _hgrn_sum_matrices():
    c = HG_CHUNK
    u = np.arange(c)[None, :]
    r = np.arange(c)[:, None]
    fw, bw = [], []
    for lvl in range(HG_LEVELS):
        hs = c >> (lvl + 1)
        blk = (r // (2 * hs)) * (2 * hs)
        upper = (r % (2 * hs)) >= hs
        last_lower = blk + hs - 1
        first_upper = blk + hs
        fw.append(np.where(upper, (u > last_lower) & (u <= r), (u > r) & (u <= last_lower)))
        bw.append(np.where(upper, (u >= first_upper) & (u < r), (u >= r) & (u < first_upper)))
    fw += [u <= r, u > r]
    bw += [u >= r, u < r]

    def twice(parts):
        w = np.concatenate(parts).astype(np.float32)
        return jnp.asarray(np.concatenate([w, w], axis=1), BF16)

    return twice(fw), twice(bw)


def _hgrn_pair_masks():
    c = HG_CHUNK
    row = np.arange(c)[:, None]
    col = np.arange(c)[None, :]
    fw, bw = [], []
    for lvl in range(HG_LEVELS):
        hs = c >> (lvl + 1)
        same = (row // (2 * hs)) == (col // (2 * hs))
        row_up = (row % (2 * hs)) >= hs
        col_up = (col % (2 * hs)) >= hs
        fw.append(same & row_up & ~col_up)
        bw.append(same & ~row_up & col_up)
    fw.append(row == col)
    bw.append(row == col)

    def in_lane_halves(masks):
        m = np.stack(masks).astype(np.float32)
        z = np.zeros_like(m)
        return jnp.asarray(np.stack([np.concatenate([m, z], axis=-1),
                                     np.concatenate([z, m], axis=-1)], axis=1))

    assert 2 * c == LANES
    return in_lane_halves(fw), in_lane_halves(bw)


def _hgrn_kernel(q_ref, ff_ref, fb_ref, i_ref, g_ref, lbl_ref, ng_ref, wf_ref, wb_ref, mf_ref, mb_ref,
                 o_ref, o_scr, st_scr, *, layer, n_ctx):
    t = q_ref.shape[1]
    c = HG_CHUNK
    dk = HG_DK
    n = t // c
    nc = n_ctx // c
    depth = lbl_ref.shape[0]

    def lower_bound(direction):
        logits = [lbl_ref[d, direction] for d in range(depth)]
        m = functools.reduce(jnp.maximum, logits)
        e = [jnp.exp(x - m) for x in logits]
        tot = functools.reduce(lambda a, b: a + b, e)
        p = [x / tot for x in e]
        cum = functools.reduce(lambda a, b: a + b, p[:layer + 1])
        return cum - p[0]

    def heads_on_rows(x):
        return jnp.concatenate([x[:, h * dk:(h + 1) * dk] for h in range(HG_HEADS)], axis=0)

    def stream(off, f_ref, lb, w_ref, m_ref, exit_row, slot):
        rows = pl.ds(off, c)
        f = lb + (1.0 - lb) * jax.nn.sigmoid(f_ref[0, rows, :])
        kk = 1.0 - f
        decay = jnp.exp(_dot(w_ref[...], jnp.concatenate(_split_bf16(jnp.log(f)), axis=0)))
        q = _silu(q_ref[0, rows, :].astype(F32))
        v = i_ref[0, rows, :]
        a2 = [0.0] * HG_HEADS
        for lvl in range(HG_LEVELS + 1):
            if lvl < HG_LEVELS:
                y = decay[lvl * c:(lvl + 1) * c, :]
                qs, ks = heads_on_rows((q * y).astype(BF16)), heads_on_rows((kk * y).astype(BF16))
            else:
                qs, ks = heads_on_rows(q.astype(BF16)), heads_on_rows(kk.astype(BF16))
            p = _dot_nt(qs, ks)
            for h in range(HG_HEADS):
                tile = (h * c) // LANES
                slab = p[h * c:(h + 1) * c, tile * LANES:(tile + 1) * LANES]
                a2[h] = a2[h] + m_ref[lvl, h % 2] * slab
        e_cum = decay[HG_LEVELS * c:(HG_LEVELS + 1) * c, :]
        e_rest = decay[(HG_LEVELS + 1) * c:(HG_LEVELS + 2) * c, :]
        qe = (q * e_cum).astype(BF16)
        kd = (kk * e_rest).astype(BF16)
        outs = []
        for h in range(HG_HEADS):
            cols = slice(h * dk, (h + 1) * dk)
            st = st_scr[slot, cols, :]
            v_h = v[:, cols]
            v2 = jnp.concatenate([v_h, v_h], axis=0)
            outs.append(_dot(a2[h].astype(BF16), v2) + _dot_nt(qe[:, cols], st.astype(BF16)))
            st_scr[slot, cols, :] = st * e_cum[exit_row:exit_row + 1, cols] + _dot_tn(v_h, kd[:, cols])
        o_scr[rows, :] += jnp.concatenate(outs, axis=1)

    lb_f = lower_bound(0)
    lb_b = lower_bound(1)
    st_scr[...] = jnp.zeros_like(st_scr)
    o_scr[...] = jnp.zeros_like(o_scr)

    unroll = 4
    assert n % unroll == 0

    def body(k2, carry):
        for u in range(unroll):
            k = k2 * unroll + u
            off_f = pl.multiple_of(k * c, c)
            kb = jnp.where(k < nc, nc - 1 - k, n + nc - 1 - k)
            off_b = pl.multiple_of(kb * c, c)
            stream(off_f, ff_ref, lb_f, wf_ref, mf_ref, c - 1, 0)
            stream(off_b, fb_ref, lb_b, wb_ref, mb_ref, 0, 1)
        return carry

    lax.fori_loop(0, n // unroll, body, 0)

    ng = ng_ref[...]

    def readout(r, carry):
        rows = pl.ds(pl.multiple_of(r * ROW_TILE, ROW_TILE), ROW_TILE)
        for h in range(HG_HEADS):
            cols = slice(h * dk, (h + 1) * dk)
            y = (_rms(o_scr[rows, cols]) * ng) * _silu(g_ref[0, rows, cols].astype(F32))
            o_ref[0, rows, cols] = y.astype(o_ref.dtype)
        return carry

    lax.fori_loop(0, t // ROW_TILE, readout, 0)


def _hgrn(z, zf, lb_logits, norm_g, layer, n_ctx):
    b, t, _ = z.shape
    depth = lb_logits.shape[0]
    w = HG_WIDTH
    lbl = lb_logits.astype(F32).reshape(depth, 2, 1, w)
    wf, wb = _hgrn_sum_matrices()
    mf, mb = _hgrn_pair_masks()

    def zcol(base):
        return pl.BlockSpec((1, t, w), lambda bi: (bi, 0, base // w))

    def whole(shape):
        return pl.BlockSpec(shape, lambda bi: (0,) * len(shape))

    return pl.pallas_call(
        functools.partial(_hgrn_kernel, layer=layer, n_ctx=n_ctx),
        out_shape=jax.ShapeDtypeStruct((b, t, w), BF16),
        grid=(b,),
        in_specs=[
            zcol(COL_HG_Q),
            pl.BlockSpec((1, t, w), lambda bi: (bi, 0, 0)),
            pl.BlockSpec((1, t, w), lambda bi: (bi, 0, 1)),
            zcol(COL_HG_I),
            zcol(COL_HG_G),
            whole(lbl.shape), whole((1, HG_DK)), whole(wf.shape), whole(wb.shape),
            whole(mf.shape), whole(mb.shape),
        ],
        out_specs=pl.BlockSpec((1, t, w), lambda bi: (bi, 0, 0)),
        scratch_shapes=[pltpu.VMEM((t, w), F32), pltpu.VMEM((2, w, HG_DK), F32)],
        compiler_params=_params(("arbitrary",)),
        name="hgrn2_scan",
    )(z, zf, zf, z, z, lbl, norm_g.reshape(1, HG_DK).astype(F32), wf, wb, mf, mb)


def _merge_kernel(s_ref, a_ref, b_ref, c_ref, ga_ref, gb_ref, gc_ref, mod_ref, modc_ref,
                  wb_ref, wo_ref, gf_ref, wr_ref, br_ref, s_out, f_out, route_out):
    first = pl.program_id(1) == 0
    tm = s_ref.shape[1]

    y = 0.0
    for idx, (br, gr) in enumerate(((a_ref, ga_ref), (b_ref, gb_ref), (c_ref, gc_ref))):
        y = y + jax.nn.sigmoid(gr[0].astype(F32)) * _dot(br[0], wb_ref[idx])
    proj = _dot(y.astype(BF16), wo_ref[...])
    w_hi, w_lo = _split_bf16(wr_ref[...])
    for r in range(tm // ROW_TILE):
        rows = slice(r * ROW_TILE, (r + 1) * ROW_TILE)
        is_ctx = jnp.logical_and(first, r == 0)
        _merge_rows(s_ref, proj[rows, :], rows, r, is_ctx, mod_ref, modc_ref, gf_ref, w_hi, w_lo, br_ref,
                    s_out, f_out, route_out)


def _merge_rows(s_ref, proj, rows, r, is_ctx, mod_ref, modc_ref, gf_ref, w_hi, w_lo, br_ref,
                s_out, f_out, route_out):
    def mod_row(k):
        return jnp.where(is_ctx, modc_ref[k:k + 1, :], mod_ref[0, k:k + 1, :])

    x = s_ref[0, rows, :] + mod_row(2) * proj
    s_out[0, rows, :] = x
    f = (_rms(x) * gf_ref[...]) * (1.0 + mod_row(4)) + mod_row(3)
    _rows_to_tiles(f_out, r * ROW_TILE, f)

    f_hi, f_lo = _split_bf16(f)
    logits = _dot(f_hi, w_hi) + _dot(f_lo, w_hi) + _dot(f_hi, w_lo) + br_ref[...]

    lane = lax.broadcasted_iota(jnp.int32, logits.shape, 1)
    is_group = jnp.logical_and(lane >= N_EXPERTS, lane < N_EXPERTS + N_GROUPS)

    def first_argmax(x, x_max):
        return jnp.min(jnp.where(x == x_max, lane, LANES), axis=-1, keepdims=True)

    gl = jnp.where(is_group, logits, MASK_NEG)
    g_max = jnp.max(gl, axis=-1, keepdims=True)
    g_idx = first_argmax(gl, g_max) - N_EXPERTS
    p_group = 1.0 / jnp.sum(jnp.exp(gl - g_max), axis=-1, keepdims=True)
    in_group = jnp.logical_and(lane < N_EXPERTS, (lane // EXPERTS_PER_GROUP) == g_idx)
    e1 = jnp.where(in_group, logits, MASK_NEG)
    v1 = jnp.max(e1, axis=-1, keepdims=True)
    i1 = first_argmax(e1, v1)
    e2 = jnp.where(lane == i1, MASK_NEG, e1)
    v2 = jnp.max(e2, axis=-1, keepdims=True)
    i2 = first_argmax(e2, v2)
    r21 = jnp.exp(v2 - v1)
    w1 = 1.0 / (1.0 + r21)
    w2 = r21 * w1
    route = jnp.where(lane == 0, i1.astype(F32), 0.0)
    route = jnp.where(lane == 1, i2.astype(F32), route)
    route = jnp.where(lane == 2, w1 * p_group, route)
    route = jnp.where(lane == 3, w2 * p_group, route)
    route_out[0, :, rows] = route.T[0:SUBLANES, :]


def _merge(s, a, bb, cc, z, mod, modc, wb, wo, gf, wr, br):
    b, t, d = s.shape
    tm = WIDE_ROWS
    bw = a.shape[-1]
    gate0 = COL_GATES // d

    def rows(width, colblk=0):
        return pl.BlockSpec((1, tm, width), lambda bi, j: (bi, j, colblk))

    def whole(shape):
        return pl.BlockSpec(shape, lambda bi, j: (0,) * len(shape))

    return pl.pallas_call(
        _merge_kernel,
        out_shape=(jax.ShapeDtypeStruct((b, t, d), F32),
                   jax.ShapeDtypeStruct((b, t * SUBLANES, LANES), F32),
                   jax.ShapeDtypeStruct((b, SUBLANES, t), F32)),
        grid=(b, t // tm),
        in_specs=[
            rows(d), rows(bw), rows(bw), rows(bw),
            rows(d, gate0), rows(d, gate0 + 1), rows(d, gate0 + 2),
            pl.BlockSpec((1, N_MOD, d), lambda bi, j: (bi, 0, 0)),
            whole((N_MOD, d)),
            whole(wb.shape), whole(wo.shape), whole((1, d)), whole(wr.shape), whole((1, LANES)),
        ],
        out_specs=(rows(d), pl.BlockSpec((1, tm * SUBLANES, LANES), lambda bi, j: (bi, j, 0)),
                   pl.BlockSpec((1, SUBLANES, tm), lambda bi, j: (bi, 0, j))),
        compiler_params=_params(("arbitrary", "arbitrary")),
        name="merge_router",
    )(s, a, bb, cc, z, z, z, mod, modc, wb, wo, gf, wr, br)


MOE_CHUNK = 320
MOE_PAD_TOKENS = SUBLANES


def _moe_kernel(tok_ref, wgt_ref, cnt_ref, off_ref, f_ref, wg_ref, wu_ref, wd_ref, y_ref,
                xs_scr, ys_scr):
    e = pl.program_id(1)
    sub = SUBLANES
    n_col = f_ref.shape[2]
    t_dummy = f_ref.shape[1]

    @pl.when(e == 0)
    def _():
        y_ref[...] = jnp.zeros_like(y_ref)
        xs_scr[...] = jnp.zeros_like(xs_scr)

    cnt = cnt_ref[0, 0, e]
    off = off_ref[0, 0, e]
    wg = wg_ref[0, 0].astype(BF16)
    wu = wu_ref[0, 0].astype(BF16)
    wd = wd_ref[0, 0].astype(BF16)

    def chunk(ci, carry):
        base = off + ci * MOE_CHUNK
        m = jnp.minimum(MOE_CHUNK, cnt - ci * MOE_CHUNK)
        n_grp = (m + sub - 1) // sub

        def gather(gi, c2):
            for u in range(sub):
                r = gi * sub + u
                tok = tok_ref[0, 0, base + r]
                xs_scr[pl.ds(pl.multiple_of(r * n_col, n_col), n_col), :] = f_ref[0, tok]
            return c2

        lax.fori_loop(0, n_grp, gather, 0)
        x = jnp.concatenate(
            [xs_scr[pl.ds(s, MOE_CHUNK, stride=n_col), :] for s in range(n_col)], axis=1).astype(BF16)
        he = (_silu(_dot(x, wg)) * _dot(x, wu)).astype(BF16)
        y = _dot(he, wd)
        for s in range(n_col):
            ys_scr[pl.ds(s, MOE_CHUNK, stride=n_col), :] = y[:, s * LANES:(s + 1) * LANES]

        def scatter(gi, c2):
            toks, vals = [], []
            for u in range(sub):
                r = gi * sub + u
                tok = jnp.where(r < m, tok_ref[0, 0, base + r], t_dummy)
                contrib = wgt_ref[0, 0, base + r] * ys_scr[pl.ds(pl.multiple_of(r * n_col, n_col), n_col), :]
                toks.append(tok)
                vals.append(y_ref[0, tok] + contrib)
            for tok, val in zip(toks, vals):
                y_ref[0, tok] = val
            return c2

        lax.fori_loop(0, n_grp, scatter, 0)
        return carry

    lax.fori_loop(0, (cnt + MOE_CHUNK - 1) // MOE_CHUNK, chunk, 0)


def _moe(f_tiles, route, w_gate, w_up, w_down, layer):
    b, _, t = route.shape
    d, hid = w_gate.shape[-2:]
    n_col = d // LANES
    assert n_col == SUBLANES and f_tiles.shape == (b, t * n_col, LANES)
    n_slot = TOP_K * t

    eid = route[:, 0:TOP_K, :].astype(jnp.int32).reshape(b, n_slot)
    wts = route[:, TOP_K:2 * TOP_K, :].reshape(b, n_slot)
    perm = jnp.argsort(eid, axis=1).astype(jnp.int32)
    pad = jnp.zeros((b, MOE_CHUNK), jnp.int32)
    tok_sorted = jnp.concatenate([perm % t, pad], axis=1).reshape(b, 1, n_slot + MOE_CHUNK)
    wgt_sorted = jnp.concatenate([jnp.take_along_axis(wts, perm, axis=1), pad.astype(F32)],
                                 axis=1).reshape(b, 1, n_slot + MOE_CHUNK)
    counts = jnp.sum(eid[:, :, None] == jnp.arange(N_EXPERTS)[None, None, :], axis=1).astype(jnp.int32)
    offs = (jnp.cumsum(counts, axis=1) - counts).astype(jnp.int32)
    counts = counts.reshape(b, 1, N_EXPERTS)
    offs = offs.reshape(b, 1, N_EXPERTS)

    def smem(n):
        return pl.BlockSpec((1, 1, n), lambda bi, e: (bi, 0, 0), memory_space=pltpu.SMEM)

    t_out = t + MOE_PAD_TOKENS
    y4 = pl.pallas_call(
        _moe_kernel,
        out_shape=jax.ShapeDtypeStruct((b, t_out, n_col, LANES), F32),
        grid=(b, N_EXPERTS),
        in_specs=[
            smem(n_slot + MOE_CHUNK), smem(n_slot + MOE_CHUNK), smem(N_EXPERTS), smem(N_EXPERTS),
            pl.BlockSpec((1, t, n_col, LANES), lambda bi, e: (bi, 0, 0, 0),
                         pipeline_mode=pl.Buffered(1)),
            pl.BlockSpec((1, 1, d, hid), lambda bi, e: (layer, e, 0, 0)),
            pl.BlockSpec((1, 1, d, hid), lambda bi, e: (layer, e, 0, 0)),
            pl.BlockSpec((1, 1, hid, d), lambda bi, e: (layer, e, 0, 0)),
        ],
        out_specs=pl.BlockSpec((1, t_out, n_col, LANES), lambda bi, e: (bi, 0, 0, 0)),
        scratch_shapes=[pltpu.VMEM((MOE_CHUNK * n_col, LANES), F32),
                        pltpu.VMEM((MOE_CHUNK * n_col, LANES), F32)],
        compiler_params=_params(("arbitrary", "arbitrary")),
        name="moe_experts",
    )(tok_sorted, wgt_sorted, counts, offs, f_tiles.reshape(b, t, n_col, LANES), w_gate, w_up, w_down)
    return y4.reshape(b, t_out * n_col, LANES)


def _final_kernel(s_ref, y_ref, mod_ref, g_ref, o_ref):
    x = s_ref[0] + mod_ref[0, 5:6, :] * _tiles_to_rows(y_ref, 0, s_ref.shape[1])
    o_ref[0] = _rms(x) * g_ref[...]


def _final_norm(s, y, mod, g, n_ctx):
    b, t, d = s.shape
    tm = ROW_TILE
    skip = n_ctx // tm
    lat_rows = pl.BlockSpec((1, tm, d), lambda bi, j: (bi, j + skip, 0))
    return pl.pallas_call(
        _final_kernel,
        out_shape=jax.ShapeDtypeStruct((b, t - n_ctx, d), F32),
        grid=(b, (t - n_ctx) // tm),
        in_specs=[lat_rows,
                  pl.BlockSpec((1, tm * SUBLANES, LANES), lambda bi, j: (bi, j + skip, 0)),
                  pl.BlockSpec((1, N_MOD, d), lambda bi, j: (bi, 0, 0)),
                  pl.BlockSpec((1, d), lambda bi, j: (0, 0))],
        out_specs=pl.BlockSpec((1, tm, d), lambda bi, j: (bi, j, 0)),
        compiler_params=_params(("arbitrary", "arbitrary")),
        name="final_norm",
    )(s, y, mod, g)


def kernel(x, c, ctx, c_ctx, w_ada, b_ada, g_mix, g_ffn, w_in, na_rpb, hg_lb_logits, hg_norm_g,
           gqa_qnorm_g, gqa_knorm_g, w_branch, w_out, w_group_router, b_group_router,
           w_expert_router, b_expert_router, w_exp_gate, w_exp_up, w_exp_down, g_final):
    b, n_lat, d = x.shape
    n_ctx = ctx.shape[1]
    depth = w_in.shape[0]
    assert n_ctx == ROW_TILE and n_lat % ROW_TILE == 0 and (n_ctx + n_lat) % WIDE_ROWS == 0
    rows = n_lat // GRID_W

    s = (ctx, x)

    c_rows = 16
    c_all = jnp.concatenate([c, c_ctx[None, :], jnp.zeros((c_rows - b - 1, d), c.dtype)], axis=0)
    mod_all = _ada(c_all, w_ada, b_ada).reshape(depth, c_rows, N_MOD, d)

    tables = _rope_tables(n_lat)
    rep = LANES // GQA_HEAD_DIM

    w_pad = _prep_w_in(w_in)

    prev = None
    for l in range(depth):
        mod = mod_all[l, :b]
        modc = mod_all[l, b]
        z, zf, s = _inproj(s, g_mix[l].reshape(1, d), modc, mod, w_pad, l, n_ctx, prev)

        a = _na_attention(z, _na_bias_table(na_rpb[l], rows), n_ctx)
        cc = _gqa_attention(z, tables,
                            jnp.tile(gqa_qnorm_g[l].astype(F32), rep).reshape(1, LANES),
                            jnp.tile(gqa_knorm_g[l].astype(F32), rep).reshape(1, LANES), n_ctx)
        bb = _hgrn(z, zf, hg_lb_logits, hg_norm_g[l], l, n_ctx)

        wr = jnp.concatenate(
            [w_expert_router[l], w_group_router[l],
             jnp.zeros((d, LANES - N_EXPERTS - N_GROUPS), F32)], axis=1)
        br = jnp.concatenate(
            [b_expert_router[l], b_group_router[l],
             jnp.zeros((LANES - N_EXPERTS - N_GROUPS,), F32)]).reshape(1, LANES)
        s, f, route = _merge(s, a, bb, cc, z, mod, modc, w_branch[l].astype(BF16),
                             w_out[l].astype(BF16), g_ffn[l].reshape(1, d), wr, br)
        y = _moe(f, route, w_exp_gate, w_exp_up, w_exp_down, l)
        prev = (y, mod, modc)

    return _final_norm(s, prev[0], prev[1], g_final.reshape(1, d), n_ctx)
```

```python
import functools

import numpy as np
import jax
import jax.numpy as jnp
from jax import lax
from jax.experimental import pallas as pl
from jax.experimental.pallas import tpu as pltpu

F32 = jnp.float32
BF16 = jnp.bfloat16

RMS_EPS = 1e-6
N_MOD = 6
GRID_W = 64

NA_HEADS = 8
NA_HEAD_DIM = 64
NA_WIDTH = NA_HEADS * NA_HEAD_DIM
WIN_ROWS = 8
WIN_COLS = 16
NA_QROWS = 4
NA_KROWS = 12

HG_HEADS = 4
HG_DK = 128
HG_WIDTH = HG_HEADS * HG_DK
HG_CHUNK = 64

GQA_Q_HEADS = 8
GQA_KV_HEADS = 2
GQA_HEAD_DIM = 64
GQA_Q_WIDTH = GQA_Q_HEADS * GQA_HEAD_DIM
GQA_KV_WIDTH = GQA_KV_HEADS * GQA_HEAD_DIM
ROPE_THETA = 10000.0

N_GROUPS = 4
EXPERTS_PER_GROUP = 4
N_EXPERTS = N_GROUPS * EXPERTS_PER_GROUP
TOP_K = 2

LANES = 128
SUBLANES = 8
ROW_TILE = 256
WIDE_ROWS = 768
MASK_NEG = -1e30

COL_NA_Q = 0
COL_NA_K = 512
COL_NA_V = 1024
COL_HG_Q = 1536
COL_HG_FF = 2048
COL_HG_FB = 2560
COL_HG_I = 3072
COL_HG_G = 3584
COL_GQA_Q = 4096
COL_GQA_K = 4608
COL_GQA_V = 4736
COL_RAW_GATES = 4864
COL_GATES = 5120
IN_COLS_PAD = 8192
IN_TILE = 2048

VMEM_LIMIT = 56 * 1024 * 1024


def _dot(a, b):
    return jnp.dot(a, b, preferred_element_type=F32)


def _dot_nt(a, b):
    return lax.dot_general(a, b, (((1,), (1,)), ((), ())), preferred_element_type=F32)


def _dot_tn(a, b):
    return lax.dot_general(a, b, (((0,), (0,)), ((), ())), preferred_element_type=F32)


def _split_bf16(x):
    hi = x.astype(BF16)
    lo = (x - hi.astype(F32)).astype(BF16)
    return hi, lo


def _silu(x):
    return x * jax.nn.sigmoid(x)


def _rms(x):
    return x * lax.rsqrt(jnp.mean(x * x, axis=-1, keepdims=True) + RMS_EPS)


def _tiles_to_rows(ref, tok0, n):
    return jnp.concatenate(
        [ref[0, pl.ds(tok0 * SUBLANES + s, n, stride=SUBLANES), :] for s in range(SUBLANES)], axis=1)


def _rows_to_tiles(ref, tok0, x):
    n = x.shape[0]
    for s in range(SUBLANES):
        ref[0, pl.ds(tok0 * SUBLANES + s, n, stride=SUBLANES), :] = x[:, s * LANES:(s + 1) * LANES]


def _params(semantics, vmem=VMEM_LIMIT):
    return pltpu.CompilerParams(dimension_semantics=semantics, vmem_limit_bytes=vmem)


def _ada_kernel(c_ref, w_ref, b_ref, o_ref):
    sc = _silu(c_ref[...]).astype(BF16)
    o_ref[0] = _dot(sc, w_ref[0].astype(BF16)) + b_ref[0]


def _ada(c_all, w_ada, b_ada):
    depth, d, n = w_ada.shape
    rows = c_all.shape[0]
    tn = 1536
    return pl.pallas_call(
        _ada_kernel,
        out_shape=jax.ShapeDtypeStruct((depth, rows, n), F32),
        grid=(depth, n // tn),
        in_specs=[
            pl.BlockSpec((rows, d), lambda l, j: (0, 0)),
            pl.BlockSpec((1, d, tn), lambda l, j: (l, 0, j)),
            pl.BlockSpec((1, 1, tn), lambda l, j: (l, 0, j)),
        ],
        out_specs=pl.BlockSpec((1, rows, tn), lambda l, j: (l, 0, j)),
        compiler_params=_params(("arbitrary", "arbitrary")),
        name="ada_mod",
    )(c_all, w_ada, b_ada.reshape(depth, 1, n))


PREP_ROWS = 128


def _wprep_kernel(w_ref, o_ref):
    rows = w_ref.shape[1]
    o_ref[0, :, :COL_RAW_GATES] = w_ref[0, :, :COL_RAW_GATES].astype(o_ref.dtype)
    o_ref[0, :, COL_RAW_GATES:COL_GATES] = jnp.zeros((rows, COL_GATES - COL_RAW_GATES), o_ref.dtype)
    o_ref[0, :, COL_GATES:] = w_ref[0, :, COL_RAW_GATES:].astype(o_ref.dtype)


def _prep_w_in(w_in):
    depth, d, n = w_in.shape
    assert n + COL_GATES - COL_RAW_GATES == IN_COLS_PAD and d % PREP_ROWS == 0
    assert COL_RAW_GATES % LANES == 0 and COL_GATES % LANES == 0
    return pl.pallas_call(
        _wprep_kernel,
        out_shape=jax.ShapeDtypeStruct((depth, d, IN_COLS_PAD), BF16),
        grid=(depth, d // PREP_ROWS),
        in_specs=[pl.BlockSpec((1, PREP_ROWS, n), lambda l, r: (l, r, 0))],
        out_specs=pl.BlockSpec((1, PREP_ROWS, IN_COLS_PAD), lambda l, r: (l, r, 0)),
        compiler_params=_params(("arbitrary", "arbitrary")),
        name="w_in_prep",
    )(w_in)


def _inproj_kernel(*refs, n_ctx, tiles_per_batch, residual):
    n_sub = WIDE_ROWS // ROW_TILE
    if residual:
        (s_ref, y_ref, modp_ref, modcp_ref, g_ref, modc_ref, mod_ref, w_ref,
         z_ref, zf_ref, s_out, h_scr) = refs
    else:
        ctx_ref, *lat_refs = refs[:1 + n_sub]
        g_ref, modc_ref, mod_ref, w_ref, z_ref, zf_ref, s_out, h_scr = refs[1 + n_sub:]
    i = pl.program_id(0)
    j = pl.program_id(1)

    @pl.when(j == 0)
    def _():
        g = g_ref[...]
        first = (i % tiles_per_batch) == 0
        for r in range(n_sub):
            rows = slice(r * ROW_TILE, (r + 1) * ROW_TILE)
            is_ctx = jnp.logical_and(first, r * ROW_TILE < n_ctx)

            def pick(ctx_ref, lat_ref, k):
                return jnp.where(is_ctx, ctx_ref[k:k + 1, :], lat_ref[0, k:k + 1, :])

            if residual:
                x = s_ref[0, rows, :]
                x = x + pick(modcp_ref, modp_ref, 5) * _tiles_to_rows(y_ref, r * ROW_TILE, ROW_TILE)
            elif r == 0:
                x = jnp.where(is_ctx, ctx_ref[0], lat_refs[0][0])
            else:
                x = lat_refs[r][0]
            s_out[0, rows, :] = x
            h = (_rms(x) * g) * (1.0 + pick(modc_ref, mod_ref, 1)) + pick(modc_ref, mod_ref, 0)
            h_scr[rows, :] = h.astype(BF16)

    acc = _dot(h_scr[...], w_ref[0])
    z_ref[0] = acc.astype(BF16)

    @pl.when(j == COL_HG_FF // IN_TILE)
    def _():
        lo = COL_HG_FF % IN_TILE
        zf_ref[0] = acc[:, lo:lo + zf_ref.shape[2]]


def _inproj(stream, g, modc, mod, w_pad, layer, n_ctx, prev=None):
    residual = prev is not None
    if residual:
        b, t, d = stream.shape
    else:
        ctx, x = stream
        b, n_lat, d = x.shape
        t = n_ctx + n_lat
        assert ctx.shape == (b, n_ctx, d) and n_ctx == ROW_TILE
    n = w_pad.shape[2]
    tm = WIDE_ROWS
    tpb = t // tm
    n_sub = tm // ROW_TILE
    f_cols = 2 * HG_WIDTH
    assert COL_HG_FB == COL_HG_FF + HG_WIDTH and d == f_cols
    assert COL_HG_FF // IN_TILE == (COL_HG_FF + f_cols - 1) // IN_TILE

    row_spec = pl.BlockSpec((1, tm, d), lambda i, j: (i // tpb, i % tpb, 0))
    mod_spec = pl.BlockSpec((1, N_MOD, d), lambda i, j: (i // tpb, 0, 0))
    modc_spec = pl.BlockSpec((N_MOD, d), lambda i, j: (0, 0))
    if residual:
        y_spec = pl.BlockSpec((1, tm * SUBLANES, LANES), lambda i, j: (i // tpb, i % tpb, 0))
        in_specs = [row_spec, y_spec, mod_spec, modc_spec]
        args = [stream, prev[0], prev[1], prev[2]]
    else:
        def lat_spec(r):
            return pl.BlockSpec(
                (1, ROW_TILE, d), lambda i, j: (i // tpb, jnp.maximum((i % tpb) * n_sub + r - 1, 0), 0))

        in_specs = [pl.BlockSpec((1, ROW_TILE, d), lambda i, j: (i // tpb, 0, 0))]
        in_specs += [lat_spec(r) for r in range(n_sub)]
        args = [ctx] + [x] * n_sub
    in_specs += [pl.BlockSpec((1, d), lambda i, j: (0, 0)), modc_spec, mod_spec,
                 pl.BlockSpec((1, d, IN_TILE), lambda i, j: (layer, 0, j))]
    args += [g, modc, mod, w_pad]
    out_shape = [jax.ShapeDtypeStruct((b, t, n), BF16),
                 jax.ShapeDtypeStruct((b, t, f_cols), F32),
                 jax.ShapeDtypeStruct((b, t, d), F32)]
    out_specs = [
        pl.BlockSpec((1, tm, IN_TILE), lambda i, j: (i // tpb, i % tpb, j)),
        row_spec,
        row_spec,
    ]
    return pl.pallas_call(
        functools.partial(_inproj_kernel, n_ctx=n_ctx, tiles_per_batch=tpb, residual=residual),
        out_shape=tuple(out_shape),
        grid=(b * tpb, n // IN_TILE),
        in_specs=in_specs,
        out_specs=tuple(out_specs),
        scratch_shapes=[pltpu.VMEM((tm, d), BF16)],
        compiler_params=_params(("arbitrary", "arbitrary")),
        name="in_proj",
    )(*args)


def _na_block_start(blk, rows):
    return jnp.clip(NA_QROWS * blk - WIN_ROWS // 2, 0, rows - NA_KROWS)


def _na_bias_table(rpb, rows):
    n_blk = rows // NA_QROWS
    assert rows % NA_QROWS == 0 and n_blk >= 3 and rows >= NA_KROWS and NA_KROWS % 2 == 0
    assert 2 * GRID_W == LANES
    qc = np.arange(GRID_W)[:, None]
    kc = np.arange(GRID_W)[None, :]
    c0 = np.clip(qc - WIN_COLS // 2, 0, GRID_W - WIN_COLS)
    col_ok = (kc >= c0) & (kc < c0 + WIN_COLS)
    dcol = np.clip(kc - qc + WIN_COLS - 1, 0, 2 * WIN_COLS - 2)
    col_sel = (np.arange(2 * WIN_COLS - 1)[:, None, None] == dcol[None]).astype(np.float32)
    per_row = jnp.einsum('hab,bqk->haqk', rpb.astype(F32), col_sel, precision=lax.Precision.HIGHEST)
    per_row = jnp.where(col_ok, per_row, MASK_NEG)
    per_row = jnp.concatenate([per_row, per_row], axis=-1)
    h, n_dr = per_row.shape[:2]
    return pl.pallas_call(
        functools.partial(_na_bias_kernel, rows=rows),
        out_shape=jax.ShapeDtypeStruct((h, 3, NA_QROWS * GRID_W, NA_KROWS * GRID_W), F32),
        grid=(h, 3),
        in_specs=[pl.BlockSpec((1, n_dr, GRID_W, LANES), lambda hi, p: (hi, 0, 0, 0))],
        out_specs=pl.BlockSpec((1, 1, NA_QROWS * GRID_W, NA_KROWS * GRID_W), lambda hi, p: (hi, p, 0, 0)),
        compiler_params=_params(("arbitrary", "arbitrary")),
        name="na_bias_table",
    )(per_row)


def _na_bias_kernel(t_ref, o_ref, *, rows):
    p = pl.program_id(1)
    n_blk = rows // NA_QROWS
    lo_half = lax.broadcasted_iota(jnp.int32, (GRID_W, LANES), 1) < GRID_W
    masked = jnp.full((GRID_W, LANES), MASK_NEG, F32)
    for pat, blk in enumerate((0, 1, n_blk - 1)):

        @pl.when(p == pat)
        def _():
            u0 = int(np.clip(NA_QROWS * blk - WIN_ROWS // 2, 0, rows - NA_KROWS))
            for j in range(NA_QROWS):
                r = NA_QROWS * blk + j
                r0 = int(np.clip(r - WIN_ROWS // 2, 0, rows - WIN_ROWS))
                for pair in range(NA_KROWS // 2):
                    halves = []
                    for i in (2 * pair, 2 * pair + 1):
                        krow = u0 + i
                        inside = r0 <= krow < r0 + WIN_ROWS
                        halves.append(t_ref[0, krow - r + WIN_ROWS - 1] if inside else masked)
                    o_ref[0, 0, j * GRID_W:(j + 1) * GRID_W, pair * LANES:(pair + 1) * LANES] = (
                        jnp.where(lo_half, halves[0], halves[1]))


def _na_kernel(q_ref, k_ref, v_ref, bias_ref, o_ref, *, n_ctx, rows):
    i = pl.program_id(1)
    tq = q_ref.shape[1]
    lane = lax.broadcasted_iota(jnp.int32, (tq, LANES), 1)
    lo_half = lane < NA_HEAD_DIM

    scale = NA_HEAD_DIM ** -0.5

    def pair_scores(hp, key_rows, with_bias):
        cols = slice(hp * LANES, (hp + 1) * LANES)
        q2 = q_ref[0, :, cols] * scale
        out = []
        for hh in range(2):
            keep = lo_half if hh == 0 else jnp.logical_not(lo_half)
            qm = jnp.where(keep, q2, jnp.zeros_like(q2))
            blocks = []
            for n, kr in enumerate(key_rows):
                s = _dot_nt(qm, k_ref[0, kr, cols])
                if with_bias and n == 0:
                    s = s + bias_ref[2 * hp + hh, 0]
                blocks.append(s)
            out.append(blocks)
        return out

    def pair_finish(hp, scores, key_rows):
        cols = slice(hp * LANES, (hp + 1) * LANES)
        outs = []
        for blocks in scores:
            m = functools.reduce(jnp.maximum, [jnp.max(s, axis=-1, keepdims=True) for s in blocks])
            den = 0.0
            acc = 0.0
            for s, kr in zip(blocks, key_rows):
                p = jnp.exp(s - m)
                den = den + jnp.sum(p, axis=-1, keepdims=True)
                acc = acc + _dot(p.astype(BF16), v_ref[0, kr, cols])
            outs.append(acc / den)
        o_ref[0, :, cols] = jnp.where(lo_half, outs[0], outs[1]).astype(o_ref.dtype)

    def attend(key_rows, with_bias):
        n_pairs = NA_HEADS // 2
        nxt = pair_scores(0, key_rows, with_bias)
        for hp in range(n_pairs):
            cur = nxt
            if hp + 1 < n_pairs:
                nxt = pair_scores(hp + 1, key_rows, with_bias)
            pair_finish(hp, cur, key_rows)

    ctx_rows = slice(0, n_ctx)

    @pl.when(i == 0)
    def _():
        attend([ctx_rows], False)

    @pl.when(i > 0)
    def _():
        u0 = _na_block_start(i - 1, rows)
        local_rows = pl.ds(pl.multiple_of(n_ctx + u0 * GRID_W, GRID_W), NA_KROWS * GRID_W)
        attend([local_rows, ctx_rows], True)


def _na_attention(z, bias_tab, n_ctx):
    b, t, _ = z.shape
    rows = (t - n_ctx) // GRID_W
    tq = NA_QROWS * GRID_W
    assert n_ctx == tq
    n_blk = rows // NA_QROWS
    wk = NA_KROWS * GRID_W

    def pattern(i):
        return jnp.where(i <= 1, 0, jnp.where(i == n_blk, 2, 1))

    return pl.pallas_call(
        functools.partial(_na_kernel, n_ctx=n_ctx, rows=rows),
        out_shape=jax.ShapeDtypeStruct((b, t, NA_WIDTH), BF16),
        grid=(b, 1 + n_blk),
        in_specs=[
            pl.BlockSpec((1, tq, NA_WIDTH), lambda bi, i: (bi, i, COL_NA_Q // NA_WIDTH)),
            pl.BlockSpec((1, t, NA_WIDTH), lambda bi, i: (bi, 0, COL_NA_K // NA_WIDTH)),
            pl.BlockSpec((1, t, NA_WIDTH), lambda bi, i: (bi, 0, COL_NA_V // NA_WIDTH)),
            pl.BlockSpec((NA_HEADS, 1, tq, wk), lambda bi, i: (0, pattern(i), 0, 0)),
        ],
        out_specs=pl.BlockSpec((1, tq, NA_WIDTH), lambda bi, i: (bi, i, 0)),
        compiler_params=_params(("arbitrary", "arbitrary")),
        name="na_attention",
    )(z, z, z, bias_tab)


def _rope_tables(n_tokens):
    t = jnp.arange(n_tokens)
    pos = jnp.stack([t // GRID_W, t % GRID_W], axis=-1).astype(F32)
    n_freq = GQA_HEAD_DIM // 4
    inv_freq = jnp.power(ROPE_THETA, -jnp.arange(n_freq, dtype=F32) / n_freq)
    ang = pos[:, :, None] * inv_freq
    ang = jnp.concatenate([ang, ang], axis=-1).reshape(n_tokens, GQA_HEAD_DIM)
    cos, sin = jnp.cos(ang), jnp.sin(ang)
    first = (np.arange(GQA_HEAD_DIM) % (2 * n_freq)) < n_freq
    sin_a = jnp.where(first, -sin, 0.0)
    sin_b = jnp.where(first, 0.0, sin)
    rep = LANES // GQA_HEAD_DIM
    return tuple(jnp.tile(a, (1, rep)) for a in (cos, sin_a, sin_b))


def _gqa_kernel(q_ref, k_ref, v_ref, cos_ref, sa_ref, sb_ref, gq_ref, gk_ref, o_ref,
                kk_scr, vv_scr, *, n_ctx):
    j = pl.program_id(1)
    t = k_ref.shape[1]
    tq = q_ref.shape[1]
    hd = GQA_HEAD_DIM
    quarter = hd // 4

    r_i = lax.broadcasted_iota(jnp.int32, (LANES, LANES), 0)
    c_i = lax.broadcasted_iota(jnp.int32, (LANES, LANES), 1)
    head_ones = jnp.where((r_i // hd) == (c_i // hd), 1.0, 0.0).astype(BF16)
    lane = lax.broadcasted_iota(jnp.int32, (tq, LANES), 1)
    lo_half = lane < hd

    def head_rms(x, g):
        hi, lo = _split_bf16(x * x)
        ms = (_dot(hi, head_ones) + _dot(lo, head_ones)) * (1.0 / hd)
        return (x * lax.rsqrt(ms + RMS_EPS)) * g

    def rope(x, rows):
        return (x * cos_ref[rows, :]
                + pltpu.roll(x, LANES - quarter, 1) * sa_ref[rows, :]
                + pltpu.roll(x, quarter, 1) * sb_ref[rows, :])

    @pl.when(j == 0)
    def _():
        zero = jnp.zeros((tq, LANES), F32)
        for r in range(t // tq):
            rows = slice(r * tq, (r + 1) * tq)
            k = head_rms(k_ref[0, rows, :].astype(F32), gk_ref[...])
            if r * tq >= n_ctx:
                k = rope(k, slice(r * tq - n_ctx, (r + 1) * tq - n_ctx))
            v = v_ref[0, rows, :].astype(F32)
            k_sw = pltpu.roll(k, hd, 1)
            v_sw = pltpu.roll(v, hd, 1)
            v_lo_rest = jnp.where(lane == hd, 1.0, zero)
            v_hi_rest = jnp.where(lane == 0, 1.0, zero)
            for scr, a, a_sw, lo_rest, hi_rest in ((kk_scr, k, k_sw, zero, zero),
                                                   (vv_scr, v, v_sw, v_lo_rest, v_hi_rest)):
                scr[0, rows, :] = jnp.where(lo_half, a, lo_rest).astype(BF16)
                scr[1, rows, :] = jnp.where(lo_half, hi_rest, a_sw).astype(BF16)
                scr[2, rows, :] = jnp.where(lo_half, a_sw, lo_rest).astype(BF16)
                scr[3, rows, :] = jnp.where(lo_half, hi_rest, a).astype(BF16)

    scale = hd ** -0.5

    n_chunks = GQA_Q_WIDTH // LANES

    def tile(nk, rope_rows):
        def scores(c):
            qc = head_rms(q_ref[0, :, c * LANES:(c + 1) * LANES].astype(F32), gq_ref[...])
            if rope_rows is not None:
                qc = rope(qc, rope_rows)
            qc = (qc * scale).astype(BF16)
            grp = (2 * c) // (GQA_Q_HEADS // GQA_KV_HEADS)
            return [_dot_nt(qc, kk_scr[2 * grp + hh, 0:nk, :]) for hh in range(2)]

        ahead = 2
        queue = [scores(c) for c in range(min(ahead, n_chunks))]
        for c in range(n_chunks):
            s_pair = queue.pop(0)
            if c + ahead < n_chunks:
                queue.append(scores(c + ahead))
            grp = (2 * c) // (GQA_Q_HEADS // GQA_KV_HEADS)
            outs = []
            for hh in range(2):
                s = s_pair[hh]
                m = jnp.max(s, axis=-1, keepdims=True)
                p = jnp.exp((s - m).astype(BF16))
                o = _dot(p, vv_scr[2 * grp + hh, 0:nk, :])
                sum_lane = hd if hh == 0 else 0
                den = jnp.sum(jnp.where(lane == sum_lane, o, 0.0), axis=-1, keepdims=True)
                outs.append(o / den)
            o_ref[0, :, c * LANES:(c + 1) * LANES] = jnp.where(lo_half, outs[0], outs[1]).astype(o_ref.dtype)

    @pl.when(j == 0)
    def _():
        tile(n_ctx, None)

    @pl.when(j > 0)
    def _():
        tile(t, pl.ds(pl.multiple_of((j - 1) * tq, tq), tq))


def _gqa_attention(z, tables, gq, gk, n_ctx):
    b, t, _ = z.shape
    tq = ROW_TILE
    assert n_ctx == tq
    n_lat = t - n_ctx
    cos, sa, sb = tables
    tab_spec = pl.BlockSpec((n_lat, LANES), lambda bi, j: (0, 0))
    g_spec = pl.BlockSpec((1, LANES), lambda bi, j: (0, 0))
    return pl.pallas_call(
        functools.partial(_gqa_kernel, n_ctx=n_ctx),
        out_shape=jax.ShapeDtypeStruct((b, t, GQA_Q_WIDTH), BF16),
        grid=(b, t // tq),
        in_specs=[
            pl.BlockSpec((1, tq, GQA_Q_WIDTH), lambda bi, j: (bi, j, COL_GQA_Q // GQA_Q_WIDTH)),
            pl.BlockSpec((1, t, GQA_KV_WIDTH), lambda bi, j: (bi, 0, COL_GQA_K // GQA_KV_WIDTH)),
            pl.BlockSpec((1, t, GQA_KV_WIDTH), lambda bi, j: (bi, 0, COL_GQA_V // GQA_KV_WIDTH)),
            tab_spec, tab_spec, tab_spec, g_spec, g_spec,
        ],
        out_specs=pl.BlockSpec((1, tq, GQA_Q_WIDTH), lambda bi, j: (bi, j, 0)),
        scratch_shapes=[pltpu.VMEM((4, t, LANES), BF16), pltpu.VMEM((4, t, LANES), BF16)],
        compiler_params=_params(("arbitrary", "arbitrary")),
        name="gqa_attention",
    )(z, z, z, cos, sa, sb, gq, gk)


HG_LEVELS = 6


def _hgrn_sum_matrices():
    c = HG_CHUNK
    u = np.arange(c)[None, :]
    r = np.arange(c)[:, None]
    fw, bw = [], []
    for lvl in range(HG_LEVELS):
        hs = c >> (lvl + 1)
        blk = (r // (2 * hs)) * (2 * hs)
        upper = (r % (2 * hs)) >= hs
        last_lower = blk + hs - 1
        first_upper = blk + hs
        fw.append(np.where(upper, (u > last_lower) & (u <= r), (u > r) & (u <= last_lower)))
        bw.append(np.where(upper, (u >= first_upper) & (u < r), (u >= r) & (u < first_upper)))
    fw += [u <= r, u > r]
    bw += [u >= r, u < r]

    def twice(parts):
        w = np.concatenate(parts).astype(np.float32)
        return jnp.asarray(np.concatenate([w, w], axis=1), BF16)

    return twice(fw), twice(bw)


def _hgrn_pair_masks():
    c = HG_CHUNK
    row = np.arange(c)[:, None]
    col = np.arange(c)[None, :]
    fw, bw = [], []
    for lvl in range(HG_LEVELS):
        hs = c >> (lvl + 1)
        same = (row // (2 * hs)) == (col // (2 * hs))
        row_up = (row % (2 * hs)) >= hs
        col_up = (col % (2 * hs)) >= hs
        fw.append(same & row_up & ~col_up)
        bw.append(same & ~row_up & col_up)
    fw.append(row == col)
    bw.append(row == col)

    def in_lane_halves(masks):
        m = np.stack(masks).astype(np.float32)
        z = np.zeros_like(m)
        return jnp.asarray(np.stack([np.concatenate([m, z], axis=-1),
                                     np.concatenate([z, m], axis=-1)], axis=1))

    assert 2 * c == LANES
    return in_lane_halves(fw), in_lane_halves(bw)


def _hgrn_kernel(q_ref, ff_ref, fb_ref, i_ref, g_ref, lbl_ref, ng_ref, wf_ref, wb_ref, mf_ref, mb_ref,
                 o_ref, o_scr, st_scr, *, layer, n_ctx):
    t = q_ref.shape[1]
    c = HG_CHUNK
    dk = HG_DK
    n = t // c
    nc = n_ctx // c
    depth = lbl_ref.shape[0]

    def lower_bound(direction):
        logits = [lbl_ref[d, direction] for d in range(depth)]
        m = functools.reduce(jnp.maximum, logits)
        e = [jnp.exp(x - m) for x in logits]
        tot = functools.reduce(lambda a, b: a + b, e)
        p = [x / tot for x in e]
        cum = functools.reduce(lambda a, b: a + b, p[:layer + 1])
        return cum - p[0]

    def heads_on_rows(x):
        return jnp.concatenate([x[:, h * dk:(h + 1) * dk] for h in range(HG_HEADS)], axis=0)

    def stream(off, f_ref, lb, w_ref, m_ref, exit_row, slot):
        rows = pl.ds(off, c)
        f = lb + (1.0 - lb) * jax.nn.sigmoid(f_ref[0, rows, :])
        kk = 1.0 - f
        decay = jnp.exp(_dot(w_ref[...], jnp.concatenate(_split_bf16(jnp.log(f)), axis=0)))
        q = _silu(q_ref[0, rows, :].astype(F32))
        v = i_ref[0, rows, :]
        a2 = [0.0] * HG_HEADS
        for lvl in range(HG_LEVELS + 1):
            if lvl < HG_LEVELS:
                y = decay[lvl * c:(lvl + 1) * c, :]
                qs, ks = heads_on_rows((q * y).astype(BF16)), heads_on_rows((kk * y).astype(BF16))
            else:
                qs, ks = heads_on_rows(q.astype(BF16)), heads_on_rows(kk.astype(BF16))
            p = _dot_nt(qs, ks)
            for h in range(HG_HEADS):
                tile = (h * c) // LANES
                slab = p[h * c:(h + 1) * c, tile * LANES:(tile + 1) * LANES]
                a2[h] = a2[h] + m_ref[lvl, h % 2] * slab
        e_cum = decay[HG_LEVELS * c:(HG_LEVELS + 1) * c, :]
        e_rest = decay[(HG_LEVELS + 1) * c:(HG_LEVELS + 2) * c, :]
        qe = (q * e_cum).astype(BF16)
        kd = (kk * e_rest).astype(BF16)
        outs = []
        for h in range(HG_HEADS):
            cols = slice(h * dk, (h + 1) * dk)
            st = st_scr[slot, cols, :]
            v_h = v[:, cols]
            v2 = jnp.concatenate([v_h, v_h], axis=0)
            outs.append(_dot(a2[h].astype(BF16), v2) + _dot_nt(qe[:, cols], st.astype(BF16)))
            st_scr[slot, cols, :] = st * e_cum[exit_row:exit_row + 1, cols] + _dot_tn(v_h, kd[:, cols])
        o_scr[rows, :] += jnp.concatenate(outs, axis=1)

    lb_f = lower_bound(0)
    lb_b = lower_bound(1)
    st_scr[...] = jnp.zeros_like(st_scr)
    o_scr[...] = jnp.zeros_like(o_scr)

    unroll = 4
    assert n % unroll == 0

    def body(k2, carry):
        for u in range(unroll):
            k = k2 * unroll + u
            off_f = pl.multiple_of(k * c, c)
            kb = jnp.where(k < nc, nc - 1 - k, n + nc - 1 - k)
            off_b = pl.multiple_of(kb * c, c)
            stream(off_f, ff_ref, lb_f, wf_ref, mf_ref, c - 1, 0)
            stream(off_b, fb_ref, lb_b, wb_ref, mb_ref, 0, 1)
        return carry

    lax.fori_loop(0, n // unroll, body, 0)

    ng = ng_ref[...]

    def readout(r, carry):
        rows = pl.ds(pl.multiple_of(r * ROW_TILE, ROW_TILE), ROW_TILE)
        for h in range(HG_HEADS):
            cols = slice(h * dk, (h + 1) * dk)
            y = (_rms(o_scr[rows, cols]) * ng) * _silu(g_ref[0, rows, cols].astype(F32))
            o_ref[0, rows, cols] = y.astype(o_ref.dtype)
        return carry

    lax.fori_loop(0, t // ROW_TILE, readout, 0)


def _hgrn(z, zf, lb_logits, norm_g, layer, n_ctx):
    b, t, _ = z.shape
    depth = lb_logits.shape[0]
    w = HG_WIDTH
    lbl = lb_logits.astype(F32).reshape(depth, 2, 1, w)
    wf, wb = _hgrn_sum_matrices()
    mf, mb = _hgrn_pair_masks()

    def zcol(base):
        return pl.BlockSpec((1, t, w), lambda bi: (bi, 0, base // w))

    def whole(shape):
        return pl.BlockSpec(shape, lambda bi: (0,) * len(shape))

    return pl.pallas_call(
        functools.partial(_hgrn_kernel, layer=layer, n_ctx=n_ctx),
        out_shape=jax.ShapeDtypeStruct((b, t, w), BF16),
        grid=(b,),
        in_specs=[
            zcol(COL_HG_Q),
            pl.BlockSpec((1, t, w), lambda bi: (bi, 0, 0)),
            pl.BlockSpec((1, t, w), lambda bi: (bi, 0, 1)),
            zcol(COL_HG_I),
            zcol(COL_HG_G),
            whole(lbl.shape), whole((1, HG_DK)), whole(wf.shape), whole(wb.shape),
            whole(mf.shape), whole(mb.shape),
        ],
        out_specs=pl.BlockSpec((1, t, w), lambda bi: (bi, 0, 0)),
        scratch_shapes=[pltpu.VMEM((t, w), F32), pltpu.VMEM((2, w, HG_DK), F32)],
        compiler_params=_params(("arbitrary",)),
        name="hgrn2_scan",
    )(z, zf, zf, z, z, lbl, norm_g.reshape(1, HG_DK).astype(F32), wf, wb, mf, mb)


def _merge_kernel(s_ref, a_ref, b_ref, c_ref, ga_ref, gb_ref, gc_ref, mod_ref, modc_ref,
                  wb_ref, wo_ref, gf_ref, wr_ref, br_ref, s_out, f_out, route_out):
    first = pl.program_id(1) == 0
    tm = s_ref.shape[1]

    y = 0.0
    for idx, (br, gr) in enumerate(((a_ref, ga_ref), (b_ref, gb_ref), (c_ref, gc_ref))):
        y = y + jax.nn.sigmoid(gr[0].astype(F32)) * _dot(br[0], wb_ref[idx])
    proj = _dot(y.astype(BF16), wo_ref[...])
    w_hi, w_lo = _split_bf16(wr_ref[...])
    for r in range(tm // ROW_TILE):
        rows = slice(r * ROW_TILE, (r + 1) * ROW_TILE)
        is_ctx = jnp.logical_and(first, r == 0)
        _merge_rows(s_ref, proj[rows, :], rows, r, is_ctx, mod_ref, modc_ref, gf_ref, w_hi, w_lo, br_ref,
                    s_out, f_out, route_out)


def _merge_rows(s_ref, proj, rows, r, is_ctx, mod_ref, modc_ref, gf_ref, w_hi, w_lo, br_ref,
                s_out, f_out, route_out):
    def mod_row(k):
        return jnp.where(is_ctx, modc_ref[k:k + 1, :], mod_ref[0, k:k + 1, :])

    x = s_ref[0, rows, :] + mod_row(2) * proj
    s_out[0, rows, :] = x
    f = (_rms(x) * gf_ref[...]) * (1.0 + mod_row(4)) + mod_row(3)
    _rows_to_tiles(f_out, r * ROW_TILE, f)

    f_hi, f_lo = _split_bf16(f)
    logits = _dot(f_hi, w_hi) + _dot(f_lo, w_hi) + _dot(f_hi, w_lo) + br_ref[...]

    lane = lax.broadcasted_iota(jnp.int32, logits.shape, 1)
    is_group = jnp.logical_and(lane >= N_EXPERTS, lane < N_EXPERTS + N_GROUPS)

    def first_argmax(x, x_max):
        return jnp.min(jnp.where(x == x_max, lane, LANES), axis=-1, keepdims=True)

    gl = jnp.where(is_group, logits, MASK_NEG)
    g_max = jnp.max(gl, axis=-1, keepdims=True)
    g_idx = first_argmax(gl, g_max) - N_EXPERTS
    p_group = 1.0 / jnp.sum(jnp.exp(gl - g_max), axis=-1, keepdims=True)
    in_group = jnp.logical_and(lane < N_EXPERTS, (lane // EXPERTS_PER_GROUP) == g_idx)
    e1 = jnp.where(in_group, logits, MASK_NEG)
    v1 = jnp.max(e1, axis=-1, keepdims=True)
    i1 = first_argmax(e1, v1)
    e2 = jnp.where(lane == i1, MASK_NEG, e1)
    v2 = jnp.max(e2, axis=-1, keepdims=True)
    i2 = first_argmax(e2, v2)
    r21 = jnp.exp(v2 - v1)
    w1 = 1.0 / (1.0 + r21)
    w2 = r21 * w1
    route = jnp.where(lane == 0, i1.astype(F32), 0.0)
    route = jnp.where(lane == 1, i2.astype(F32), route)
    route = jnp.where(lane == 2, w1 * p_group, route)
    route = jnp.where(lane == 3, w2 * p_group, route)
    route_out[0, :, rows] = route.T[0:SUBLANES, :]


def _merge(s, a, bb, cc, z, mod, modc, wb, wo, gf, wr, br):
    b, t, d = s.shape
    tm = WIDE_ROWS
    bw = a.shape[-1]
    gate0 = COL_GATES // d

    def rows(width, colblk=0):
        return pl.BlockSpec((1, tm, width), lambda bi, j: (bi, j, colblk))

    def whole(shape):
        return pl.BlockSpec(shape, lambda bi, j: (0,) * len(shape))

    return pl.pallas_call(
        _merge_kernel,
        out_shape=(jax.ShapeDtypeStruct((b, t, d), F32),
                   jax.ShapeDtypeStruct((b, t * SUBLANES, LANES), F32),
                   jax.ShapeDtypeStruct((b, SUBLANES, t), F32)),
        grid=(b, t // tm),
        in_specs=[
            rows(d), rows(bw), rows(bw), rows(bw),
            rows(d, gate0), rows(d, gate0 + 1), rows(d, gate0 + 2),
            pl.BlockSpec((1, N_MOD, d), lambda bi, j: (bi, 0, 0)),
            whole((N_MOD, d)),
            whole(wb.shape), whole(wo.shape), whole((1, d)), whole(wr.shape), whole((1, LANES)),
        ],
        out_specs=(rows(d), pl.BlockSpec((1, tm * SUBLANES, LANES), lambda bi, j: (bi, j, 0)),
                   pl.BlockSpec((1, SUBLANES, tm), lambda bi, j: (bi, 0, j))),
        compiler_params=_params(("arbitrary", "arbitrary")),
        name="merge_router",
    )(s, a, bb, cc, z, z, z, mod, modc, wb, wo, gf, wr, br)


MOE_CHUNK = 320
MOE_PAD_TOKENS = SUBLANES


def _moe_kernel(tok_ref, wgt_ref, cnt_ref, off_ref, f_ref, wg_ref, wu_ref, wd_ref, y_ref,
                xs_scr, ys_scr):
    e = pl.program_id(1)
    sub = SUBLANES
    n_col = f_ref.shape[2]
    t_dummy = f_ref.shape[1]

    @pl.when(e == 0)
    def _():
        y_ref[...] = jnp.zeros_like(y_ref)
        xs_scr[...] = jnp.zeros_like(xs_scr)

    cnt = cnt_ref[0, 0, e]
    off = off_ref[0, 0, e]
    wg = wg_ref[0, 0].astype(BF16)
    wu = wu_ref[0, 0].astype(BF16)
    wd = wd_ref[0, 0].astype(BF16)

    def chunk(ci, carry):
        base = off + ci * MOE_CHUNK
        m = jnp.minimum(MOE_CHUNK, cnt - ci * MOE_CHUNK)
        n_grp = (m + sub - 1) // sub

        def gather(gi, c2):
            for u in range(sub):
                r = gi * sub + u
                tok = tok_ref[0, 0, base + r]
                xs_scr[pl.ds(pl.multiple_of(r * n_col, n_col), n_col), :] = f_ref[0, tok]
            return c2

        lax.fori_loop(0, n_grp, gather, 0)
        x = jnp.concatenate(
            [xs_scr[pl.ds(s, MOE_CHUNK, stride=n_col), :] for s in range(n_col)], axis=1).astype(BF16)
        he = (_silu(_dot(x, wg)) * _dot(x, wu)).astype(BF16)
        y = _dot(he, wd)
        for s in range(n_col):
            ys_scr[pl.ds(s, MOE_CHUNK, stride=n_col), :] = y[:, s * LANES:(s + 1) * LANES]

        def scatter(gi, c2):
            toks, vals = [], []
            for u in range(sub):
                r = gi * sub + u
                tok = jnp.where(r < m, tok_ref[0, 0, base + r], t_dummy)
                contrib = wgt_ref[0, 0, base + r] * ys_scr[pl.ds(pl.multiple_of(r * n_col, n_col), n_col), :]
                toks.append(tok)
                vals.append(y_ref[0, tok] + contrib)
            for tok, val in zip(toks, vals):
                y_ref[0, tok] = val
            return c2

        lax.fori_loop(0, n_grp, scatter, 0)
        return carry

    lax.fori_loop(0, (cnt + MOE_CHUNK - 1) // MOE_CHUNK, chunk, 0)


def _moe(f_tiles, route, w_gate, w_up, w_down, layer):
    b, _, t = route.shape
    d, hid = w_gate.shape[-2:]
    n_col = d // LANES
    assert n_col == SUBLANES and f_tiles.shape == (b, t * n_col, LANES)
    n_slot = TOP_K * t

    eid = route[:, 0:TOP_K, :].astype(jnp.int32).reshape(b, n_slot)
    wts = route[:, TOP_K:2 * TOP_K, :].reshape(b, n_slot)
    perm = jnp.argsort(eid, axis=1).astype(jnp.int32)
    pad = jnp.zeros((b, MOE_CHUNK), jnp.int32)
    tok_sorted = jnp.concatenate([perm % t, pad], axis=1).reshape(b, 1, n_slot + MOE_CHUNK)
    wgt_sorted = jnp.concatenate([jnp.take_along_axis(wts, perm, axis=1), pad.astype(F32)],
                                 axis=1).reshape(b, 1, n_slot + MOE_CHUNK)
    counts = jnp.sum(eid[:, :, None] == jnp.arange(N_EXPERTS)[None, None, :], axis=1).astype(jnp.int32)
    offs = (jnp.cumsum(counts, axis=1) - counts).astype(jnp.int32)
    counts = counts.reshape(b, 1, N_EXPERTS)
    offs = offs.reshape(b, 1, N_EXPERTS)

    def smem(n):
        return pl.BlockSpec((1, 1, n), lambda bi, e: (bi, 0, 0), memory_space=pltpu.SMEM)

    t_out = t + MOE_PAD_TOKENS
    y4 = pl.pallas_call(
        _moe_kernel,
        out_shape=jax.ShapeDtypeStruct((b, t_out, n_col, LANES), F32),
        grid=(b, N_EXPERTS),
        in_specs=[
            smem(n_slot + MOE_CHUNK), smem(n_slot + MOE_CHUNK), smem(N_EXPERTS), smem(N_EXPERTS),
            pl.BlockSpec((1, t, n_col, LANES), lambda bi, e: (bi, 0, 0, 0),
                         pipeline_mode=pl.Buffered(1)),
            pl.BlockSpec((1, 1, d, hid), lambda bi, e: (layer, e, 0, 0)),
            pl.BlockSpec((1, 1, d, hid), lambda bi, e: (layer, e, 0, 0)),
            pl.BlockSpec((1, 1, hid, d), lambda bi, e: (layer, e, 0, 0)),
        ],
        out_specs=pl.BlockSpec((1, t_out, n_col, LANES), lambda bi, e: (bi, 0, 0, 0)),
        scratch_shapes=[pltpu.VMEM((MOE_CHUNK * n_col, LANES), F32),
                        pltpu.VMEM((MOE_CHUNK * n_col, LANES), F32)],
        compiler_params=_params(("arbitrary", "arbitrary")),
        name="moe_experts",
    )(tok_sorted, wgt_sorted, counts, offs, f_tiles.reshape(b, t, n_col, LANES), w_gate, w_up, w_down)
    return y4.reshape(b, t_out * n_col, LANES)


def _final_kernel(s_ref, y_ref, mod_ref, g_ref, o_ref):
    x = s_ref[0] + mod_ref[0, 5:6, :] * _tiles_to_rows(y_ref, 0, s_ref.shape[1])
    o_ref[0] = _rms(x) * g_ref[...]


def _final_norm(s, y, mod, g, n_ctx):
    b, t, d = s.shape
    tm = ROW_TILE
    skip = n_ctx // tm
    lat_rows = pl.BlockSpec((1, tm, d), lambda bi, j: (bi, j + skip, 0))
    return pl.pallas_call(
        _final_kernel,
        out_shape=jax.ShapeDtypeStruct((b, t - n_ctx, d), F32),
        grid=(b, (t - n_ctx) // tm),
        in_specs=[lat_rows,
                  pl.BlockSpec((1, tm * SUBLANES, LANES), lambda bi, j: (bi, j + skip, 0)),
                  pl.BlockSpec((1, N_MOD, d), lambda bi, j: (bi, 0, 0)),
                  pl.BlockSpec((1, d), lambda bi, j: (0, 0))],
        out_specs=pl.BlockSpec((1, tm, d), lambda bi, j: (bi, j, 0)),
        compiler_params=_params(("arbitrary", "arbitrary")),
        name="final_norm",
    )(s, y, mod, g)


def kernel(x, c, ctx, c_ctx, w_ada, b_ada, g_mix, g_ffn, w_in, na_rpb, hg_lb_logits, hg_norm_g,
           gqa_qnorm_g, gqa_knorm_g, w_branch, w_out, w_group_router, b_group_router,
           w_expert_router, b_expert_router, w_exp_gate, w_exp_up, w_exp_down, g_final):
    b, n_lat, d = x.shape
    n_ctx = ctx.shape[1]
    depth = w_in.shape[0]
    assert n_ctx == ROW_TILE and n_lat % ROW_TILE == 0 and (n_ctx + n_lat) % WIDE_ROWS == 0
    rows = n_lat // GRID_W

    s = (ctx, x)

    c_rows = 16
    c_all = jnp.concatenate([c, c_ctx[None, :], jnp.zeros((c_rows - b - 1, d), c.dtype)], axis=0)
    mod_all = _ada(c_all, w_ada, b_ada).reshape(depth, c_rows, N_MOD, d)

    tables = _rope_tables(n_lat)
    rep = LANES // GQA_HEAD_DIM

    w_pad = _prep_w_in(w_in)

    prev = None
    for l in range(depth):
        mod = mod_all[l, :b]
        modc = mod_all[l, b]
        z, zf, s = _inproj(s, g_mix[l].reshape(1, d), modc, mod, w_pad, l, n_ctx, prev)

        a = _na_attention(z, _na_bias_table(na_rpb[l], rows), n_ctx)
        cc = _gqa_attention(z, tables,
                            jnp.tile(gqa_qnorm_g[l].astype(F32), rep).reshape(1, LANES),
                            jnp.tile(gqa_knorm_g[l].astype(F32), rep).reshape(1, LANES), n_ctx)
        bb = _hgrn(z, zf, hg_lb_logits, hg_norm_g[l], l, n_ctx)

        wr = jnp.concatenate(
            [w_expert_router[l], w_group_router[l],
             jnp.zeros((d, LANES - N_EXPERTS - N_GROUPS), F32)], axis=1)
        br = jnp.concatenate(
            [b_expert_router[l], b_group_router[l],
             jnp.zeros((LANES - N_EXPERTS - N_GROUPS,), F32)]).reshape(1, LANES)
        s, f, route = _merge(s, a, bb, cc, z, mod, modc, w_branch[l].astype(BF16),
                             w_out[l].astype(BF16), g_ffn[l].reshape(1, d), wr, br)
        y = _moe(f, route, w_exp_gate, w_exp_up, w_exp_down, l)
        prev = (y, mod, modc)

    return _final_norm(s, prev[0], prev[1], g_final.reshape(1, d), n_ctx)
```

```python
import functools

import numpy as np
import jax
import jax.numpy as jnp
from jax import lax
from jax.experimental import pallas as pl
from jax.experimental.pallas import tpu as pltpu

F32 = jnp.float32
BF16 = jnp.bfloat16

RMS_EPS = 1e-6
N_MOD = 6
GRID_W = 64

NA_HEADS = 8
NA_HEAD_DIM = 64
NA_WIDTH = NA_HEADS * NA_HEAD_DIM
WIN_ROWS = 8
WIN_COLS = 16
NA_QROWS = 4
NA_KROWS = 12

HG_HEADS = 4
HG_DK = 128
HG_WIDTH = HG_HEADS * HG_DK
HG_CHUNK = 64

GQA_Q_HEADS = 8
GQA_KV_HEADS = 2
GQA_HEAD_DIM = 64
GQA_Q_WIDTH = GQA_Q_HEADS * GQA_HEAD_DIM
GQA_KV_WIDTH = GQA_KV_HEADS * GQA_HEAD_DIM
ROPE_THETA = 10000.0

N_GROUPS = 4
EXPERTS_PER_GROUP = 4
N_EXPERTS = N_GROUPS * EXPERTS_PER_GROUP
TOP_K = 2

LANES = 128
SUBLANES = 8
ROW_TILE = 256
WIDE_ROWS = 768
MASK_NEG = -1e30

COL_NA_Q = 0
COL_NA_K = 512
COL_NA_V = 1024
COL_HG_Q = 1536
COL_HG_FF = 2048
COL_HG_FB = 2560
COL_HG_I = 3072
COL_HG_G = 3584
COL_GQA_Q = 4096
COL_GQA_K = 4608
COL_GQA_V = 4736
COL_RAW_GATES = 4864
COL_GATES = 5120
IN_COLS_PAD = 8192
IN_TILE = 2048

VMEM_LIMIT = 56 * 1024 * 1024


def _dot(a, b):
    return jnp.dot(a, b, preferred_element_type=F32)


def _dot_nt(a, b):
    return lax.dot_general(a, b, (((1,), (1,)), ((), ())), preferred_element_type=F32)


def _dot_tn(a, b):
    return lax.dot_general(a, b, (((0,), (0,)), ((), ())), preferred_element_type=F32)


def _split_bf16(x):
    hi = x.astype(BF16)
    lo = (x - hi.astype(F32)).astype(BF16)
    return hi, lo


def _silu(x):
    return x * jax.nn.sigmoid(x)


def _rms(x):
    return x * lax.rsqrt(jnp.mean(x * x, axis=-1, keepdims=True) + RMS_EPS)


def _tiles_to_rows(ref, tok0, n):
    return jnp.concatenate(
        [ref[0, pl.ds(tok0 * SUBLANES + s, n, stride=SUBLANES), :] for s in range(SUBLANES)], axis=1)


def _rows_to_tiles(ref, tok0, x):
    n = x.shape[0]
    for s in range(SUBLANES):
        ref[0, pl.ds(tok0 * SUBLANES + s, n, stride=SUBLANES), :] = x[:, s * LANES:(s + 1) * LANES]


def _params(semantics, vmem=VMEM_LIMIT):
    return pltpu.CompilerParams(dimension_semantics=semantics, vmem_limit_bytes=vmem)


def _ada_kernel(c_ref, w_ref, b_ref, o_ref):
    sc = _silu(c_ref[...]).astype(BF16)
    o_ref[0] = _dot(sc, w_ref[0].astype(BF16)) + b_ref[0]


def _ada(c_all, w_ada, b_ada):
    depth, d, n = w_ada.shape
    rows = c_all.shape[0]
    tn = 1536
    return pl.pallas_call(
        _ada_kernel,
        out_shape=jax.ShapeDtypeStruct((depth, rows, n), F32),
        grid=(depth, n // tn),
        in_specs=[
            pl.BlockSpec((rows, d), lambda l, j: (0, 0)),
            pl.BlockSpec((1, d, tn), lambda l, j: (l, 0, j)),
            pl.BlockSpec((1, 1, tn), lambda l, j: (l, 0, j)),
        ],
        out_specs=pl.BlockSpec((1, rows, tn), lambda l, j: (l, 0, j)),
        compiler_params=_params(("arbitrary", "arbitrary")),
        name="ada_mod",
    )(c_all, w_ada, b_ada.reshape(depth, 1, n))


PREP_ROWS = 128


def _wprep_kernel(w_ref, o_ref):
    rows = w_ref.shape[1]
    o_ref[0, :, :COL_RAW_GATES] = w_ref[0, :, :COL_RAW_GATES].astype(o_ref.dtype)
    o_ref[0, :, COL_RAW_GATES:COL_GATES] = jnp.zeros((rows, COL_GATES - COL_RAW_GATES), o_ref.dtype)
    o_ref[0, :, COL_GATES:] = w_ref[0, :, COL_RAW_GATES:].astype(o_ref.dtype)


def _prep_w_in(w_in):
    depth, d, n = w_in.shape
    assert n + COL_GATES - COL_RAW_GATES == IN_COLS_PAD and d % PREP_ROWS == 0
    assert COL_RAW_GATES % LANES == 0 and COL_GATES % LANES == 0
    return pl.pallas_call(
        _wprep_kernel,
        out_shape=jax.ShapeDtypeStruct((depth, d, IN_COLS_PAD), BF16),
        grid=(depth, d // PREP_ROWS),
        in_specs=[pl.BlockSpec((1, PREP_ROWS, n), lambda l, r: (l, r, 0))],
        out_specs=pl.BlockSpec((1, PREP_ROWS, IN_COLS_PAD), lambda l, r: (l, r, 0)),
        compiler_params=_params(("arbitrary", "arbitrary")),
        name="w_in_prep",
    )(w_in)


def _inproj_kernel(*refs, n_ctx, tiles_per_batch, residual):
    n_sub = WIDE_ROWS // ROW_TILE
    if residual:
        (s_ref, y_ref, modp_ref, modcp_ref, g_ref, modc_ref, mod_ref, w_ref,
         z_ref, zf_ref, s_out, h_scr) = refs
    else:
        ctx_ref, *lat_refs = refs[:1 + n_sub]
        g_ref, modc_ref, mod_ref, w_ref, z_ref, zf_ref, s_out, h_scr = refs[1 + n_sub:]
    i = pl.program_id(0)
    j = pl.program_id(1)

    @pl.when(j == 0)
    def _():
        g = g_ref[...]
        first = (i % tiles_per_batch) == 0
        for r in range(n_sub):
            rows = slice(r * ROW_TILE, (r + 1) * ROW_TILE)
            is_ctx = jnp.logical_and(first, r * ROW_TILE < n_ctx)

            def pick(ctx_ref, lat_ref, k):
                return jnp.where(is_ctx, ctx_ref[k:k + 1, :], lat_ref[0, k:k + 1, :])

            if residual:
                x = s_ref[0, rows, :]
                x = x + pick(modcp_ref, modp_ref, 5) * _tiles_to_rows(y_ref, r * ROW_TILE, ROW_TILE)
            elif r == 0:
                x = jnp.where(is_ctx, ctx_ref[0], lat_refs[0][0])
            else:
                x = lat_refs[r][0]
            s_out[0, rows, :] = x
            h = (_rms(x) * g) * (1.0 + pick(modc_ref, mod_ref, 1)) + pick(modc_ref, mod_ref, 0)
            h_scr[rows, :] = h.astype(BF16)

    acc = _dot(h_scr[...], w_ref[0])
    z_ref[0] = acc.astype(BF16)

    @pl.when(j == COL_HG_FF // IN_TILE)
    def _():
        lo = COL_HG_FF % IN_TILE
        zf_ref[0] = acc[:, lo:lo + zf_ref.shape[2]]


def _inproj(stream, g, modc, mod, w_pad, layer, n_ctx, prev=None):
    residual = prev is not None
    if residual:
        b, t, d = stream.shape
    else:
        ctx, x = stream
        b, n_lat, d = x.shape
        t = n_ctx + n_lat
        assert ctx.shape == (b, n_ctx, d) and n_ctx == ROW_TILE
    n = w_pad.shape[2]
    tm = WIDE_ROWS
    tpb = t // tm
    n_sub = tm // ROW_TILE
    f_cols = 2 * HG_WIDTH
    assert COL_HG_FB == COL_HG_FF + HG_WIDTH and d == f_cols
    assert COL_HG_FF // IN_TILE == (COL_HG_FF + f_cols - 1) // IN_TILE

    row_spec = pl.BlockSpec((1, tm, d), lambda i, j: (i // tpb, i % tpb, 0))
    mod_spec = pl.BlockSpec((1, N_MOD, d), lambda i, j: (i // tpb, 0, 0))
    modc_spec = pl.BlockSpec((N_MOD, d), lambda i, j: (0, 0))
    if residual:
        y_spec = pl.BlockSpec((1, tm * SUBLANES, LANES), lambda i, j: (i // tpb, i % tpb, 0))
        in_specs = [row_spec, y_spec, mod_spec, modc_spec]
        args = [stream, prev[0], prev[1], prev[2]]
    else:
        def lat_spec(r):
            return pl.BlockSpec(
                (1, ROW_TILE, d), lambda i, j: (i // tpb, jnp.maximum((i % tpb) * n_sub + r - 1, 0), 0))

        in_specs = [pl.BlockSpec((1, ROW_TILE, d), lambda i, j: (i // tpb, 0, 0))]
        in_specs += [lat_spec(r) for r in range(n_sub)]
        args = [ctx] + [x] * n_sub
    in_specs += [pl.BlockSpec((1, d), lambda i, j: (0, 0)), modc_spec, mod_spec,
                 pl.BlockSpec((1, d, IN_TILE), lambda i, j: (layer, 0, j))]
    args += [g, modc, mod, w_pad]
    out_shape = [jax.ShapeDtypeStruct((b, t, n), BF16),
                 jax.ShapeDtypeStruct((b, t, f_cols), F32),
                 jax.ShapeDtypeStruct((b, t, d), F32)]
    out_specs = [
        pl.BlockSpec((1, tm, IN_TILE), lambda i, j: (i // tpb, i % tpb, j)),
        row_spec,
        row_spec,
    ]
    return pl.pallas_call(
        functools.partial(_inproj_kernel, n_ctx=n_ctx, tiles_per_batch=tpb, residual=residual),
        out_shape=tuple(out_shape),
        grid=(b * tpb, n // IN_TILE),
        in_specs=in_specs,
        out_specs=tuple(out_specs),
        scratch_shapes=[pltpu.VMEM((tm, d), BF16)],
        compiler_params=_params(("arbitrary", "arbitrary")),
        name="in_proj",
    )(*args)


def _na_block_start(blk, rows):
    return jnp.clip(NA_QROWS * blk - WIN_ROWS // 2, 0, rows - NA_KROWS)


def _na_bias_table(rpb, rows):
    n_blk = rows // NA_QROWS
    assert rows % NA_QROWS == 0 and n_blk >= 3 and rows >= NA_KROWS and NA_KROWS % 2 == 0
    assert 2 * GRID_W == LANES
    qc = np.arange(GRID_W)[:, None]
    kc = np.arange(GRID_W)[None, :]
    c0 = np.clip(qc - WIN_COLS // 2, 0, GRID_W - WIN_COLS)
    col_ok = (kc >= c0) & (kc < c0 + WIN_COLS)
    dcol = np.clip(kc - qc + WIN_COLS - 1, 0, 2 * WIN_COLS - 2)
    col_sel = (np.arange(2 * WIN_COLS - 1)[:, None, None] == dcol[None]).astype(np.float32)
    per_row = jnp.einsum('hab,bqk->haqk', rpb.astype(F32), col_sel, precision=lax.Precision.HIGHEST)
    per_row = jnp.where(col_ok, per_row, MASK_NEG)
    per_row = jnp.concatenate([per_row, per_row], axis=-1)
    h, n_dr = per_row.shape[:2]
    return pl.pallas_call(
        functools.partial(_na_bias_kernel, rows=rows),
        out_shape=jax.ShapeDtypeStruct((h, 3, NA_QROWS * GRID_W, NA_KROWS * GRID_W), F32),
        grid=(h, 3),
        in_specs=[pl.BlockSpec((1, n_dr, GRID_W, LANES), lambda hi, p: (hi, 0, 0, 0))],
        out_specs=pl.BlockSpec((1, 1, NA_QROWS * GRID_W, NA_KROWS * GRID_W), lambda hi, p: (hi, p, 0, 0)),
        compiler_params=_params(("arbitrary", "arbitrary")),
        name="na_bias_table",
    )(per_row)


def _na_bias_kernel(t_ref, o_ref, *, rows):
    p = pl.program_id(1)
    n_blk = rows // NA_QROWS
    lo_half = lax.broadcasted_iota(jnp.int32, (GRID_W, LANES), 1) < GRID_W
    masked = jnp.full((GRID_W, LANES), MASK_NEG, F32)
    for pat, blk in enumerate((0, 1, n_blk - 1)):

        @pl.when(p == pat)
        def _():
            u0 = int(np.clip(NA_QROWS * blk - WIN_ROWS // 2, 0, rows - NA_KROWS))
            for j in range(NA_QROWS):
                r = NA_QROWS * blk + j
                r0 = int(np.clip(r - WIN_ROWS // 2, 0, rows - WIN_ROWS))
                for pair in range(NA_KROWS // 2):
                    halves = []
                    for i in (2 * pair, 2 * pair + 1):
                        krow = u0 + i
                        inside = r0 <= krow < r0 + WIN_ROWS
                        halves.append(t_ref[0, krow - r + WIN_ROWS - 1] if inside else masked)
                    o_ref[0, 0, j * GRID_W:(j + 1) * GRID_W, pair * LANES:(pair + 1) * LANES] = (
                        jnp.where(lo_half, halves[0], halves[1]))


def _na_kernel(q_ref, k_ref, v_ref, bias_ref, o_ref, *, n_ctx, rows):
    i = pl.program_id(1)
    tq = q_ref.shape[1]
    lane = lax.broadcasted_iota(jnp.int32, (tq, LANES), 1)
    lo_half = lane < NA_HEAD_DIM

    scale = NA_HEAD_DIM ** -0.5

    def pair_scores(hp, key_rows, with_bias):
        cols = slice(hp * LANES, (hp + 1) * LANES)
        q2 = q_ref[0, :, cols] * scale
        out = []
        for hh in range(2):
            keep = lo_half if hh == 0 else jnp.logical_not(lo_half)
            qm = jnp.where(keep, q2, jnp.zeros_like(q2))
            blocks = []
            for n, kr in enumerate(key_rows):
                s = _dot_nt(qm, k_ref[0, kr, cols])
                if with_bias and n == 0:
                    s = s + bias_ref[2 * hp + hh, 0]
                blocks.append(s)
            out.append(blocks)
        return out

    def pair_finish(hp, scores, key_rows):
        cols = slice(hp * LANES, (hp + 1) * LANES)
        outs = []
        for blocks in scores:
            m = functools.reduce(jnp.maximum, [jnp.max(s, axis=-1, keepdims=True) for s in blocks])
            den = 0.0
            acc = 0.0
            for s, kr in zip(blocks, key_rows):
                p = jnp.exp(s - m)
                den = den + jnp.sum(p, axis=-1, keepdims=True)
                acc = acc + _dot(p.astype(BF16), v_ref[0, kr, cols])
            outs.append(acc / den)
        o_ref[0, :, cols] = jnp.where(lo_half, outs[0], outs[1]).astype(o_ref.dtype)

    def attend(key_rows, with_bias):
        n_pairs = NA_HEADS // 2
        nxt = pair_scores(0, key_rows, with_bias)
        for hp in range(n_pairs):
            cur = nxt
            if hp + 1 < n_pairs:
                nxt = pair_scores(hp + 1, key_rows, with_bias)
            pair_finish(hp, cur, key_rows)

    ctx_rows = slice(0, n_ctx)

    @pl.when(i == 0)
    def _():
        attend([ctx_rows], False)

    @pl.when(i > 0)
    def _():
        u0 = _na_block_start(i - 1, rows)
        local_rows = pl.ds(pl.multiple_of(n_ctx + u0 * GRID_W, GRID_W), NA_KROWS * GRID_W)
        attend([local_rows, ctx_rows], True)


def _na_attention(z, bias_tab, n_ctx):
    b, t, _ = z.shape
    rows = (t - n_ctx) // GRID_W
    tq = NA_QROWS * GRID_W
    assert n_ctx == tq
    n_blk = rows // NA_QROWS
    wk = NA_KROWS * GRID_W

    def pattern(i):
        return jnp.where(i <= 1, 0, jnp.where(i == n_blk, 2, 1))

    return pl.pallas_call(
        functools.partial(_na_kernel, n_ctx=n_ctx, rows=rows),
        out_shape=jax.ShapeDtypeStruct((b, t, NA_WIDTH), BF16),
        grid=(b, 1 + n_blk),
        in_specs=[
            pl.BlockSpec((1, tq, NA_WIDTH), lambda bi, i: (bi, i, COL_NA_Q // NA_WIDTH)),
            pl.BlockSpec((1, t, NA_WIDTH), lambda bi, i: (bi, 0, COL_NA_K // NA_WIDTH)),
            pl.BlockSpec((1, t, NA_WIDTH), lambda bi, i: (bi, 0, COL_NA_V // NA_WIDTH)),
            pl.BlockSpec((NA_HEADS, 1, tq, wk), lambda bi, i: (0, pattern(i), 0, 0)),
        ],
        out_specs=pl.BlockSpec((1, tq, NA_WIDTH), lambda bi, i: (bi, i, 0)),
        compiler_params=_params(("arbitrary", "arbitrary")),
        name="na_attention",
    )(z, z, z, bias_tab)


def _rope_tables(n_tokens):
    t = jnp.arange(n_tokens)
    pos = jnp.stack([t // GRID_W, t % GRID_W], axis=-1).astype(F32)
    n_freq = GQA_HEAD_DIM // 4
    inv_freq = jnp.power(ROPE_THETA, -jnp.arange(n_freq, dtype=F32) / n_freq)
    ang = pos[:, :, None] * inv_freq
    ang = jnp.concatenate([ang, ang], axis=-1).reshape(n_tokens, GQA_HEAD_DIM)
    cos, sin = jnp.cos(ang), jnp.sin(ang)
    first = (np.arange(GQA_HEAD_DIM) % (2 * n_freq)) < n_freq
    sin_a = jnp.where(first, -sin, 0.0)
    sin_b = jnp.where(first, 0.0, sin)
    rep = LANES // GQA_HEAD_DIM
    return tuple(jnp.tile(a, (1, rep)) for a in (cos, sin_a, sin_b))


def _gqa_kernel(q_ref, k_ref, v_ref, cos_ref, sa_ref, sb_ref, gq_ref, gk_ref, o_ref,
                kk_scr, vv_scr, *, n_ctx):
    j = pl.program_id(1)
    t = k_ref.shape[1]
    tq = q_ref.shape[1]
    hd = GQA_HEAD_DIM
    quarter = hd // 4

    r_i = lax.broadcasted_iota(jnp.int32, (LANES, LANES), 0)
    c_i = lax.broadcasted_iota(jnp.int32, (LANES, LANES), 1)
    head_ones = jnp.where((r_i // hd) == (c_i // hd), 1.0, 0.0).astype(BF16)
    lane = lax.broadcasted_iota(jnp.int32, (tq, LANES), 1)
    lo_half = lane < hd

    def head_rms(x, g):
        hi, lo = _split_bf16(x * x)
        ms = (_dot(hi, head_ones) + _dot(lo, head_ones)) * (1.0 / hd)
        return (x * lax.rsqrt(ms + RMS_EPS)) * g

    def rope(x, rows):
        return (x * cos_ref[rows, :]
                + pltpu.roll(x, LANES - quarter, 1) * sa_ref[rows, :]
                + pltpu.roll(x, quarter, 1) * sb_ref[rows, :])

    @pl.when(j == 0)
    def _():
        zero = jnp.zeros((tq, LANES), F32)
        for r in range(t // tq):
            rows = slice(r * tq, (r + 1) * tq)
            k = head_rms(k_ref[0, rows, :].astype(F32), gk_ref[...])
            if r * tq >= n_ctx:
                k = rope(k, slice(r * tq - n_ctx, (r + 1) * tq - n_ctx))
            v = v_ref[0, rows, :].astype(F32)
            k_sw = pltpu.roll(k, hd, 1)
            v_sw = pltpu.roll(v, hd, 1)
            v_lo_rest = jnp.where(lane == hd, 1.0, zero)
            v_hi_rest = jnp.where(lane == 0, 1.0, zero)
            for scr, a, a_sw, lo_rest, hi_rest in ((kk_scr, k, k_sw, zero, zero),
                                                   (vv_scr, v, v_sw, v_lo_rest, v_hi_rest)):
                scr[0, rows, :] = jnp.where(lo_half, a, lo_rest).astype(BF16)
                scr[1, rows, :] = jnp.where(lo_half, hi_rest, a_sw).astype(BF16)
                scr[2, rows, :] = jnp.where(lo_half, a_sw, lo_rest).astype(BF16)
                scr[3, rows, :] = jnp.where(lo_half, hi_rest, a).astype(BF16)

    scale = hd ** -0.5

    n_chunks = GQA_Q_WIDTH // LANES

    def tile(nk, rope_rows):
        def scores(c):
            qc = head_rms(q_ref[0, :, c * LANES:(c + 1) * LANES].astype(F32), gq_ref[...])
            if rope_rows is not None:
                qc = rope(qc, rope_rows)
            qc = (qc * scale).astype(BF16)
            grp = (2 * c) // (GQA_Q_HEADS // GQA_KV_HEADS)
            return [_dot_nt(qc, kk_scr[2 * grp + hh, 0:nk, :]) for hh in range(2)]

        ahead = 2
        queue = [scores(c) for c in range(min(ahead, n_chunks))]
        for c in range(n_chunks):
            s_pair = queue.pop(0)
            if c + ahead < n_chunks:
                queue.append(scores(c + ahead))
            grp = (2 * c) // (GQA_Q_HEADS // GQA_KV_HEADS)
            outs = []
            for hh in range(2):
                s = s_pair[hh]
                m = jnp.max(s, axis=-1, keepdims=True)
                p = jnp.exp((s - m).astype(BF16))
                o = _dot(p, vv_scr[2 * grp + hh, 0:nk, :])
                sum_lane = hd if hh == 0 else 0
                den = jnp.sum(jnp.where(lane == sum_lane, o, 0.0), axis=-1, keepdims=True)
                outs.append(o / den)
            o_ref[0, :, c * LANES:(c + 1) * LANES] = jnp.where(lo_half, outs[0], outs[1]).astype(o_ref.dtype)

    @pl.when(j == 0)
    def _():
        tile(n_ctx, None)

    @pl.when(j > 0)
    def _():
        tile(t, pl.ds(pl.multiple_of((j - 1) * tq, tq), tq))


def _gqa_attention(z, tables, gq, gk, n_ctx):
    b, t, _ = z.shape
    tq = ROW_TILE
    assert n_ctx == tq
    n_lat = t - n_ctx
    cos, sa, sb = tables
    tab_spec = pl.BlockSpec((n_lat, LANES), lambda bi, j: (0, 0))
    g_spec = pl.BlockSpec((1, LANES), lambda bi, j: (0, 0))
    return pl.pallas_call(
        functools.partial(_gqa_kernel, n_ctx=n_ctx),
        out_shape=jax.ShapeDtypeStruct((b, t, GQA_Q_WIDTH), BF16),
        grid=(b, t // tq),
        in_specs=[
            pl.BlockSpec((1, tq, GQA_Q_WIDTH), lambda bi, j: (bi, j, COL_GQA_Q // GQA_Q_WIDTH)),
            pl.BlockSpec((1, t, GQA_KV_WIDTH), lambda bi, j: (bi, 0, COL_GQA_K // GQA_KV_WIDTH)),
            pl.BlockSpec((1, t, GQA_KV_WIDTH), lambda bi, j: (bi, 0, COL_GQA_V // GQA_KV_WIDTH)),
            tab_spec, tab_spec, tab_spec, g_spec, g_spec,
        ],
        out_specs=pl.BlockSpec((1, tq, GQA_Q_WIDTH), lambda bi, j: (bi, j, 0)),
        scratch_shapes=[pltpu.VMEM((4, t, LANES), BF16), pltpu.VMEM((4, t, LANES), BF16)],
        compiler_params=_params(("arbitrary", "arbitrary")),
        name="gqa_attention",
    )(z, z, z, cos, sa, sb, gq, gk)


HG_LEVELS = 6


def _hgrn_sum_matrices():
    c = HG_CHUNK
    u = np.arange(c)[None, :]
    r = np.arange(c)[:, None]
    fw, bw = [], []
    for lvl in range(HG_LEVELS):
        hs = c >> (lvl + 1)
        blk = (r // (2 * hs)) * (2 * hs)
        upper = (r % (2 * hs)) >= hs
        last_lower = blk + hs - 1
        first_upper = blk + hs
        fw.append(np.where(upper, (u > last_lower) & (u <= r), (u > r) & (u <= last_lower)))
        bw.append(np.where(upper, (u >= first_upper) & (u < r), (u >= r) & (u < first_upper)))
    fw += [u <= r, u > r]
    bw += [u >= r, u < r]

    def twice(parts):
        w = np.concatenate(parts).astype(np.float32)
        return jnp.asarray(np.concatenate([w, w], axis=1), BF16)

    return twice(fw), twice(bw)


def _hgrn_pair_masks():
    c = HG_CHUNK
    row = np.arange(c)[:, None]
    col = np.arange(c)[None, :]
    fw, bw = [], []
    for lvl in range(HG_LEVELS):
        hs = c >> (lvl + 1)
        same = (row // (2 * hs)) == (col // (2 * hs))
        row_up = (row % (2 * hs)) >= hs
        col_up = (col % (2 * hs)) >= hs
        fw.append(same & row_up & ~col_up)
        bw.append(same & ~row_up & col_up)
    fw.append(row == col)
    bw.append(row == col)

    def in_lane_halves(masks):
        m = np.stack(masks).astype(np.float32)
        z = np.zeros_like(m)
        return jnp.asarray(np.stack([np.concatenate([m, z], axis=-1),
                                     np.concatenate([z, m], axis=-1)], axis=1))

    assert 2 * c == LANES
    return in_lane_halves(fw), in_lane_halves(bw)


def _hgrn_kernel(q_ref, ff_ref, fb_ref, i_ref, g_ref, lbl_ref, ng_ref, wf_ref, wb_ref, mf_ref, mb_ref,
                 o_ref, o_scr, st_scr, *, layer, n_ctx):
    t = q_ref.shape[1]
    c = HG_CHUNK
    dk = HG_DK
    n = t // c
    nc = n_ctx // c
    depth = lbl_ref.shape[0]

    def lower_bound(direction):
        logits = [lbl_ref[d, direction] for d in range(depth)]
        m = functools.reduce(jnp.maximum, logits)
        e = [jnp.exp(x - m) for x in logits]
        tot = functools.reduce(lambda a, b: a + b, e)
        p = [x / tot for x in e]
        cum = functools.reduce(lambda a, b: a + b, p[:layer + 1])
        return cum - p[0]

    def heads_on_rows(x):
        return jnp.concatenate([x[:, h * dk:(h + 1) * dk] for h in range(HG_HEADS)], axis=0)

    def stream(off, f_ref, lb, w_ref, m_ref, exit_row, slot):
        rows = pl.ds(off, c)
        f = lb + (1.0 - lb) * jax.nn.sigmoid(f_ref[0, rows, :])
        kk = 1.0 - f
        decay = jnp.exp(_dot(w_ref[...], jnp.concatenate(_split_bf16(jnp.log(f)), axis=0)))
        q = _silu(q_ref[0, rows, :].astype(F32))
        v = i_ref[0, rows, :]
        a2 = [0.0] * HG_HEADS
        for lvl in range(HG_LEVELS + 1):
            if lvl < HG_LEVELS:
                y = decay[lvl * c:(lvl + 1) * c, :]
                qs, ks = heads_on_rows((q * y).astype(BF16)), heads_on_rows((kk * y).astype(BF16))
            else:
                qs, ks = heads_on_rows(q.astype(BF16)), heads_on_rows(kk.astype(BF16))
            p = _dot_nt(qs, ks)
            for h in range(HG_HEADS):
                tile = (h * c) // LANES
                slab = p[h * c:(h + 1) * c, tile * LANES:(tile + 1) * LANES]
                a2[h] = a2[h] + m_ref[lvl, h % 2] * slab
        e_cum = decay[HG_LEVELS * c:(HG_LEVELS + 1) * c, :]
        e_rest = decay[(HG_LEVELS + 1) * c:(HG_LEVELS + 2) * c, :]
        qe = (q * e_cum).astype(BF16)
        kd = (kk * e_rest).astype(BF16)
        outs = []
        for h in range(HG_HEADS):
            cols = slice(h * dk, (h + 1) * dk)
            st = st_scr[slot, cols, :]
            v_h = v[:, cols]
            v2 = jnp.concatenate([v_h, v_h], axis=0)
            outs.append(_dot(a2[h].astype(BF16), v2) + _dot_nt(qe[:, cols], st.astype(BF16)))
            st_scr[slot, cols, :] = st * e_cum[exit_row:exit_row + 1, cols] + _dot_tn(v_h, kd[:, cols])
        o_scr[rows, :] += jnp.concatenate(outs, axis=1)

    lb_f = lower_bound(0)
    lb_b = lower_bound(1)
    st_scr[...] = jnp.zeros_like(st_scr)
    o_scr[...] = jnp.zeros_like(o_scr)

    unroll = 4
    assert n % unroll == 0

    def body(k2, carry):
        for u in range(unroll):
            k = k2 * unroll + u
            off_f = pl.multiple_of(k * c, c)
            kb = jnp.where(k < nc, nc - 1 - k, n + nc - 1 - k)
            off_b = pl.multiple_of(kb * c, c)
            stream(off_f, ff_ref, lb_f, wf_ref, mf_ref, c - 1, 0)
            stream(off_b, fb_ref, lb_b, wb_ref, mb_ref, 0, 1)
        return carry

    lax.fori_loop(0, n // unroll, body, 0)

    ng = ng_ref[...]

    def readout(r, carry):
        rows = pl.ds(pl.multiple_of(r * ROW_TILE, ROW_TILE), ROW_TILE)
        for h in range(HG_HEADS):
            cols = slice(h * dk, (h + 1) * dk)
            y = (_rms(o_scr[rows, cols]) * ng) * _silu(g_ref[0, rows, cols].astype(F32))
            o_ref[0, rows, cols] = y.astype(o_ref.dtype)
        return carry

    lax.fori_loop(0, t // ROW_TILE, readout, 0)


def _hgrn(z, zf, lb_logits, norm_g, layer, n_ctx):
    b, t, _ = z.shape
    depth = lb_logits.shape[0]
    w = HG_WIDTH
    lbl = lb_logits.astype(F32).reshape(depth, 2, 1, w)
    wf, wb = _hgrn_sum_matrices()
    mf, mb = _hgrn_pair_masks()

    def zcol(base):
        return pl.BlockSpec((1, t, w), lambda bi: (bi, 0, base // w))

    def whole(shape):
        return pl.BlockSpec(shape, lambda bi: (0,) * len(shape))

    return pl.pallas_call(
        functools.partial(_hgrn_kernel, layer=layer, n_ctx=n_ctx),
        out_shape=jax.ShapeDtypeStruct((b, t, w), BF16),
        grid=(b,),
        in_specs=[
            zcol(COL_HG_Q),
            pl.BlockSpec((1, t, w), lambda bi: (bi, 0, 0)),
            pl.BlockSpec((1, t, w), lambda bi: (bi, 0, 1)),
            zcol(COL_HG_I),
            zcol(COL_HG_G),
            whole(lbl.shape), whole((1, HG_DK)), whole(wf.shape), whole(wb.shape),
            whole(mf.shape), whole(mb.shape),
        ],
        out_specs=pl.BlockSpec((1, t, w), lambda bi: (bi, 0, 0)),
        scratch_shapes=[pltpu.VMEM((t, w), F32), pltpu.VMEM((2, w, HG_DK), F32)],
        compiler_params=_params(("arbitrary",)),
        name="hgrn2_scan",
    )(z, zf, zf, z, z, lbl, norm_g.reshape(1, HG_DK).astype(F32), wf, wb, mf, mb)


def _merge_kernel(s_ref, a_ref, b_ref, c_ref, ga_ref, gb_ref, gc_ref, mod_ref, modc_ref,
                  wb_ref, wo_ref, gf_ref, wr_ref, br_ref, s_out, f_out, route_out):
    first = pl.program_id(1) == 0
    tm = s_ref.shape[1]

    y = 0.0
    for idx, (br, gr) in enumerate(((a_ref, ga_ref), (b_ref, gb_ref), (c_ref, gc_ref))):
        y = y + jax.nn.sigmoid(gr[0].astype(F32)) * _dot(br[0], wb_ref[idx])
    proj = _dot(y.astype(BF16), wo_ref[...])
    w_hi, w_lo = _split_bf16(wr_ref[...])
    for r in range(tm // ROW_TILE):
        rows = slice(r * ROW_TILE, (r + 1) * ROW_TILE)
        is_ctx = jnp.logical_and(first, r == 0)
        _merge_rows(s_ref, proj[rows, :], rows, r, is_ctx, mod_ref, modc_ref, gf_ref, w_hi, w_lo, br_ref,
                    s_out, f_out, route_out)


def _merge_rows(s_ref, proj, rows, r, is_ctx, mod_ref, modc_ref, gf_ref, w_hi, w_lo, br_ref,
                s_out, f_out, route_out):
    def mod_row(k):
        return jnp.where(is_ctx, modc_ref[k:k + 1, :], mod_ref[0, k:k + 1, :])

    x = s_ref[0, rows, :] + mod_row(2) * proj
    s_out[0, rows, :] = x
    f = (_rms(x) * gf_ref[...]) * (1.0 + mod_row(4)) + mod_row(3)
    _rows_to_tiles(f_out, r * ROW_TILE, f)

    f_hi, f_lo = _split_bf16(f)
    logits = _dot(f_hi, w_hi) + _dot(f_lo, w_hi) + _dot(f_hi, w_lo) + br_ref[...]

    lane = lax.broadcasted_iota(jnp.int32, logits.shape, 1)
    is_group = jnp.logical_and(lane >= N_EXPERTS, lane < N_EXPERTS + N_GROUPS)

    def first_argmax(x, x_max):
        return jnp.min(jnp.where(x == x_max, lane, LANES), axis=-1, keepdims=True)

    gl = jnp.where(is_group, logits, MASK_NEG)
    g_max = jnp.max(gl, axis=-1, keepdims=True)
    g_idx = first_argmax(gl, g_max) - N_EXPERTS
    p_group = 1.0 / jnp.sum(jnp.exp(gl - g_max), axis=-1, keepdims=True)
    in_group = jnp.logical_and(lane < N_EXPERTS, (lane // EXPERTS_PER_GROUP) == g_idx)
    e1 = jnp.where(in_group, logits, MASK_NEG)
    v1 = jnp.max(e1, axis=-1, keepdims=True)
    i1 = first_argmax(e1, v1)
    e2 = jnp.where(lane == i1, MASK_NEG, e1)
    v2 = jnp.max(e2, axis=-1, keepdims=True)
    i2 = first_argmax(e2, v2)
    r21 = jnp.exp(v2 - v1)
    w1 = 1.0 / (1.0 + r21)
    w2 = r21 * w1
    route = jnp.where(lane == 0, i1.astype(F32), 0.0)
    route = jnp.where(lane == 1, i2.astype(F32), route)
    route = jnp.where(lane == 2, w1 * p_group, route)
    route = jnp.where(lane == 3, w2 * p_group, route)
    route_out[0, :, rows] = route.T[0:SUBLANES, :]


def _merge(s, a, bb, cc, z, mod, modc, wb, wo, gf, wr, br):
    b, t, d = s.shape
    tm = WIDE_ROWS
    bw = a.shape[-1]
    gate0 = COL_GATES // d

    def rows(width, colblk=0):
        return pl.BlockSpec((1, tm, width), lambda bi, j: (bi, j, colblk))

    def whole(shape):
        return pl.BlockSpec(shape, lambda bi, j: (0,) * len(shape))

    return pl.pallas_call(
        _merge_kernel,
        out_shape=(jax.ShapeDtypeStruct((b, t, d), F32),
                   jax.ShapeDtypeStruct((b, t * SUBLANES, LANES), F32),
                   jax.ShapeDtypeStruct((b, SUBLANES, t), F32)),
        grid=(b, t // tm),
        in_specs=[
            rows(d), rows(bw), rows(bw), rows(bw),
            rows(d, gate0), rows(d, gate0 + 1), rows(d, gate0 + 2),
            pl.BlockSpec((1, N_MOD, d), lambda bi, j: (bi, 0, 0)),
            whole((N_MOD, d)),
            whole(wb.shape), whole(wo.shape), whole((1, d)), whole(wr.shape), whole((1, LANES)),
        ],
        out_specs=(rows(d), pl.BlockSpec((1, tm * SUBLANES, LANES), lambda bi, j: (bi, j, 0)),
                   pl.BlockSpec((1, SUBLANES, tm), lambda bi, j: (bi, 0, j))),
        compiler_params=_params(("arbitrary", "arbitrary")),
        name="merge_router",
    )(s, a, bb, cc, z, z, z, mod, modc, wb, wo, gf, wr, br)


MOE_CHUNK = 320
MOE_PAD_TOKENS = SUBLANES


MOE_W_SLOTS = 3


def _moe_kernel(tok_ref, wgt_ref, cnt_ref, off_ref, f_ref, wg_hbm, wu_hbm, wd_hbm, y_ref,
                xs_scr, ys_scr, wg_buf, wu_buf, wd_buf, w_sem, *, layer):
    e = pl.program_id(1)
    sub = SUBLANES
    n_col = f_ref.shape[2]
    t_dummy = f_ref.shape[1]
    step = pl.program_id(0) * N_EXPERTS + e
    n_steps = pl.num_programs(0) * N_EXPERTS
    ahead = MOE_W_SLOTS - 1

    def weight_copies(s):
        slot = s % MOE_W_SLOTS
        ex = s % N_EXPERTS
        return [pltpu.make_async_copy(hbm.at[layer, ex], buf.at[slot], w_sem.at[k, slot])
                for k, (hbm, buf) in enumerate(((wg_hbm, wg_buf), (wu_hbm, wu_buf), (wd_hbm, wd_buf)))]

    @pl.when(step == 0)
    def _():
        for s in range(ahead):
            for cp in weight_copies(s):
                cp.start()

    @pl.when(step + ahead < n_steps)
    def _():
        for cp in weight_copies(step + ahead):
            cp.start()

    @pl.when(e == 0)
    def _():
        y_ref[...] = jnp.zeros_like(y_ref)
        xs_scr[...] = jnp.zeros_like(xs_scr)

    cnt = cnt_ref[0, 0, e]
    off = off_ref[0, 0, e]
    for cp in weight_copies(step):
        cp.wait()
    slot = step % MOE_W_SLOTS
    wg = wg_buf[slot].astype(BF16)
    wu = wu_buf[slot].astype(BF16)
    wd = wd_buf[slot].astype(BF16)

    def chunk(ci, carry):
        base = off + ci * MOE_CHUNK
        m = jnp.minimum(MOE_CHUNK, cnt - ci * MOE_CHUNK)
        n_grp = (m + sub - 1) // sub

        def gather(gi, c2):
            for u in range(sub):
                r = gi * sub + u
                tok = tok_ref[0, 0, base + r]
                xs_scr[pl.ds(pl.multiple_of(r * n_col, n_col), n_col), :] = f_ref[0, tok]
            return c2

        lax.fori_loop(0, n_grp, gather, 0)
        x = jnp.concatenate(
            [xs_scr[pl.ds(s, MOE_CHUNK, stride=n_col), :] for s in range(n_col)], axis=1).astype(BF16)
        he = (_silu(_dot(x, wg)) * _dot(x, wu)).astype(BF16)
        y = _dot(he, wd)
        for s in range(n_col):
            ys_scr[pl.ds(s, MOE_CHUNK, stride=n_col), :] = y[:, s * LANES:(s + 1) * LANES]

        def scatter(gi, c2):
            toks, vals = [], []
            for u in range(sub):
                r = gi * sub + u
                tok = jnp.where(r < m, tok_ref[0, 0, base + r], t_dummy)
                contrib = wgt_ref[0, 0, base + r] * ys_scr[pl.ds(pl.multiple_of(r * n_col, n_col), n_col), :]
                toks.append(tok)
                vals.append(y_ref[0, tok] + contrib)
            for tok, val in zip(toks, vals):
                y_ref[0, tok] = val
            return c2

        lax.fori_loop(0, n_grp, scatter, 0)
        return carry

    lax.fori_loop(0, (cnt + MOE_CHUNK - 1) // MOE_CHUNK, chunk, 0)


def _moe(f_tiles, route, w_gate, w_up, w_down, layer):
    b, _, t = route.shape
    d, hid = w_gate.shape[-2:]
    n_col = d // LANES
    assert n_col == SUBLANES and f_tiles.shape == (b, t * n_col, LANES)
    n_slot = TOP_K * t

    eid = route[:, 0:TOP_K, :].astype(jnp.int32).reshape(b, n_slot)
    wts = route[:, TOP_K:2 * TOP_K, :].reshape(b, n_slot)
    perm = jnp.argsort(eid, axis=1).astype(jnp.int32)
    pad = jnp.zeros((b, MOE_CHUNK), jnp.int32)
    tok_sorted = jnp.concatenate([perm % t, pad], axis=1).reshape(b, 1, n_slot + MOE_CHUNK)
    wgt_sorted = jnp.concatenate([jnp.take_along_axis(wts, perm, axis=1), pad.astype(F32)],
                                 axis=1).reshape(b, 1, n_slot + MOE_CHUNK)
    counts = jnp.sum(eid[:, :, None] == jnp.arange(N_EXPERTS)[None, None, :], axis=1).astype(jnp.int32)
    offs = (jnp.cumsum(counts, axis=1) - counts).astype(jnp.int32)
    counts = counts.reshape(b, 1, N_EXPERTS)
    offs = offs.reshape(b, 1, N_EXPERTS)

    def smem(n):
        return pl.BlockSpec((1, 1, n), lambda bi, e: (bi, 0, 0), memory_space=pltpu.SMEM)

    t_out = t + MOE_PAD_TOKENS
    y4 = pl.pallas_call(
        functools.partial(_moe_kernel, layer=layer),
        out_shape=jax.ShapeDtypeStruct((b, t_out, n_col, LANES), F32),
        grid=(b, N_EXPERTS),
        in_specs=[
            smem(n_slot + MOE_CHUNK), smem(n_slot + MOE_CHUNK), smem(N_EXPERTS), smem(N_EXPERTS),
            pl.BlockSpec((1, t, n_col, LANES), lambda bi, e: (bi, 0, 0, 0),
                         pipeline_mode=pl.Buffered(1)),
            pl.BlockSpec(memory_space=pl.ANY), pl.BlockSpec(memory_space=pl.ANY),
            pl.BlockSpec(memory_space=pl.ANY),
        ],
        out_specs=pl.BlockSpec((1, t_out, n_col, LANES), lambda bi, e: (bi, 0, 0, 0)),
        scratch_shapes=[pltpu.VMEM((MOE_CHUNK * n_col, LANES), F32),
                        pltpu.VMEM((MOE_CHUNK * n_col, LANES), F32),
                        pltpu.VMEM((MOE_W_SLOTS, d, hid), F32),
                        pltpu.VMEM((MOE_W_SLOTS, d, hid), F32),
                        pltpu.VMEM((MOE_W_SLOTS, hid, d), F32),
                        pltpu.SemaphoreType.DMA((3, MOE_W_SLOTS))],
        compiler_params=_params(("arbitrary", "arbitrary")),
        name="moe_experts",
    )(tok_sorted, wgt_sorted, counts, offs, f_tiles.reshape(b, t, n_col, LANES), w_gate, w_up, w_down)
    return y4.reshape(b, t_out * n_col, LANES)


def _final_kernel(s_ref, y_ref, mod_ref, g_ref, o_ref):
    x = s_ref[0] + mod_ref[0, 5:6, :] * _tiles_to_rows(y_ref, 0, s_ref.shape[1])
    o_ref[0] = _rms(x) * g_ref[...]


def _final_norm(s, y, mod, g, n_ctx):
    b, t, d = s.shape
    tm = ROW_TILE
    skip = n_ctx // tm
    lat_rows = pl.BlockSpec((1, tm, d), lambda bi, j: (bi, j + skip, 0))
    return pl.pallas_call(
        _final_kernel,
        out_shape=jax.ShapeDtypeStruct((b, t - n_ctx, d), F32),
        grid=(b, (t - n_ctx) // tm),
        in_specs=[lat_rows,
                  pl.BlockSpec((1, tm * SUBLANES, LANES), lambda bi, j: (bi, j + skip, 0)),
                  pl.BlockSpec((1, N_MOD, d), lambda bi, j: (bi, 0, 0)),
                  pl.BlockSpec((1, d), lambda bi, j: (0, 0))],
        out_specs=pl.BlockSpec((1, tm, d), lambda bi, j: (bi, j, 0)),
        compiler_params=_params(("arbitrary", "arbitrary")),
        name="final_norm",
    )(s, y, mod, g)


def kernel(x, c, ctx, c_ctx, w_ada, b_ada, g_mix, g_ffn, w_in, na_rpb, hg_lb_logits, hg_norm_g,
           gqa_qnorm_g, gqa_knorm_g, w_branch, w_out, w_group_router, b_group_router,
           w_expert_router, b_expert_router, w_exp_gate, w_exp_up, w_exp_down, g_final):
    b, n_lat, d = x.shape
    n_ctx = ctx.shape[1]
    depth = w_in.shape[0]
    assert n_ctx == ROW_TILE and n_lat % ROW_TILE == 0 and (n_ctx + n_lat) % WIDE_ROWS == 0
    rows = n_lat // GRID_W

    s = (ctx, x)

    c_rows = 16
    c_all = jnp.concatenate([c, c_ctx[None, :], jnp.zeros((c_rows - b - 1, d), c.dtype)], axis=0)
    mod_all = _ada(c_all, w_ada, b_ada).reshape(depth, c_rows, N_MOD, d)

    tables = _rope_tables(n_lat)
    rep = LANES // GQA_HEAD_DIM

    w_pad = _prep_w_in(w_in)

    prev = None
    for l in range(depth):
        mod = mod_all[l, :b]
        modc = mod_all[l, b]
        z, zf, s = _inproj(s, g_mix[l].reshape(1, d), modc, mod, w_pad, l, n_ctx, prev)

        a = _na_attention(z, _na_bias_table(na_rpb[l], rows), n_ctx)
        cc = _gqa_attention(z, tables,
                            jnp.tile(gqa_qnorm_g[l].astype(F32), rep).reshape(1, LANES),
                            jnp.tile(gqa_knorm_g[l].astype(F32), rep).reshape(1, LANES), n_ctx)
        bb = _hgrn(z, zf, hg_lb_logits, hg_norm_g[l], l, n_ctx)

        wr = jnp.concatenate(
            [w_expert_router[l], w_group_router[l],
             jnp.zeros((d, LANES - N_EXPERTS - N_GROUPS), F32)], axis=1)
        br = jnp.concatenate(
            [b_expert_router[l], b_group_router[l],
             jnp.zeros((LANES - N_EXPERTS - N_GROUPS,), F32)]).reshape(1, LANES)
        s, f, route = _merge(s, a, bb, cc, z, mod, modc, w_branch[l].astype(BF16),
                             w_out[l].astype(BF16), g_ffn[l].reshape(1, d), wr, br)
        y = _moe(f, route, w_exp_gate, w_exp_up, w_exp_down, l)
        prev = (y, mod, modc)

    return _final_norm(s, prev[0], prev[1], g_final.reshape(1, d), n_ctx)
```
